```python
import math
import jax, jax.numpy as jnp
from jax import lax
import numpy as np


D_MODEL = 2048
BATCH = 8
SEQ = 2048
DEPTH = 1

GRID_W = 64
CTX_LEN = 256
N_MOD = 9
D_FF = ((8 * D_MODEL // 3 + 255) // 256) * 256
DN_HEAD_DIM = 128
DN_HEADS = (3 * D_MODEL // 4) // DN_HEAD_DIM
DN_QK = DN_HEADS * DN_HEAD_DIM
DN_V = DN_HEADS * DN_HEAD_DIM
DN_CONV = 2 * DN_QK + DN_V
DN_COLS = DN_CONV + DN_V + 4 * DN_HEADS
CHUNK = 64
CONV_K = 3
S5_WIDTH = D_MODEL - DN_V
S5_GROUP = 16
S5_GROUPS = S5_WIDTH // S5_GROUP
S5_STATE = 64
IN_COLS = DN_COLS + S5_WIDTH
EPS = 1e-6

kernel_name = 'hybrid_deltanet_s5_macaron_dit_block'


def rmsnorm(x, w):
    xf = x.astype(jnp.float32)
    y = xf * lax.rsqrt(jnp.mean(xf * xf, axis=-1, keepdims=True) + EPS)
    return (y * w.astype(jnp.float32)).astype(x.dtype)


def ada_params(cond, w_mod, b_mod):
    m = jax.nn.silu(cond) @ w_mod + b_mod
    return m.reshape(cond.shape[:-1] + (N_MOD, -1))


def modulated_norm(h, norm_w, m, j):
    return rmsnorm(h, norm_w) * (1.0 + m[:, 3 * j + 1, None]) + m[:, 3 * j, None]


def swiglu(h, w_up, w_down):
    g, u = jnp.split(h @ w_up, 2, axis=-1)
    return (jax.nn.silu(g) * u) @ w_down


def orient(t, d):
    return t if d == 0 else jnp.flip(t, axis=1)


def short_conv(t, w, rows):
    b, l, ch = t.shape
    y = lax.conv_general_dilated(
        t.astype(jnp.float32).reshape(b, rows, l // rows, ch),
        w.astype(jnp.float32)[:, :, None, :],
        window_strides=(1, 1), padding='SAME',
        dimension_numbers=('NHWC', 'HWIO', 'NHWC'), feature_group_count=ch)
    return jax.nn.silu(y.reshape(b, l, ch))


def l2norm(t):
    return t * lax.rsqrt(jnp.sum(t * t, axis=-1, keepdims=True) + EPS)


def to_chunks(t, n):
    b, l, h = t.shape[:3]
    t = t.reshape((b, n, CHUNK, h) + t.shape[3:])
    return jnp.moveaxis(t, (1, 3), (0, 2))


def delta_rule_chunked(q, k, v, g, beta, s0, with_output):
    b, l, h, dk = q.shape
    n = l // CHUNK
    qc, kc, vc = to_chunks(q, n), to_chunks(k, n), to_chunks(v, n)
    bc = to_chunks(beta, n)
    gcum = jnp.cumsum(to_chunks(g, n), axis=-1)
    idx = jnp.arange(CHUNK)
    lower_incl = idx[:, None] >= idx[None, :]
    decay = jnp.exp(jnp.where(lower_incl, gcum[..., :, None] - gcum[..., None, :], -jnp.inf))
    a_mat = jnp.where(idx[:, None] > idx[None, :],
                      bc[..., :, None] * decay * jnp.einsum('nbhid,nbhjd->nbhij', kc, kc), 0.0)
    rhs = jnp.concatenate([(bc * jnp.exp(gcum))[..., None] * kc, bc[..., None] * vc], axis=-1)
    sol = lax.linalg.triangular_solve(a_mat + jnp.eye(CHUNK, dtype=jnp.float32), rhs,
                                      left_side=True, lower=True, unit_diagonal=True)
    w, u = sol[..., :dk], sol[..., dk:]
    k_dec = kc * jnp.exp(gcum[..., -1:] - gcum)[..., None]
    g_last = jnp.exp(gcum[..., -1])
    if with_output:
        qk = jnp.einsum('nbhid,nbhjd->nbhij', qc, kc) * decay
        q_dec = qc * jnp.exp(gcum)[..., None]
        xs = (w, u, k_dec, g_last, qk, q_dec)
    else:
        xs = (w, u, k_dec, g_last)

    def step(s, xs_i):
        u_i = xs_i[1] - jnp.einsum('bhck,bhkv->bhcv', xs_i[0], s)
        s_new = xs_i[3][..., None, None] * s + jnp.einsum('bhck,bhcv->bhkv', xs_i[2], u_i)
        if with_output:
            o = (jnp.einsum('bhck,bhkv->bhcv', xs_i[5], s)
                 + jnp.einsum('bhij,bhjv->bhiv', xs_i[4], u_i))
            return s_new, o
        return s_new, None

    s_fin, o = lax.scan(step, s0, xs)
    if not with_output:
        return None, s_fin
    o = jnp.moveaxis(o, (0, 2), (1, 3)).reshape(b, l, h, -1)
    return o, s_fin


def dn_streams(p, conv_w, a_log, dt_bias, rows):
    b, l, _ = p.shape
    qkv = short_conv(p[..., :DN_CONV], conv_w, rows)
    q = l2norm(qkv[..., :DN_QK].reshape(b, l, DN_HEADS, DN_HEAD_DIM)) * DN_HEAD_DIM ** -0.5
    k = l2norm(qkv[..., DN_QK:2 * DN_QK].reshape(b, l, DN_HEADS, DN_HEAD_DIM))
    v = qkv[..., 2 * DN_QK:].reshape(b, l, DN_HEADS, DN_HEAD_DIM)
    gb = p[..., DN_CONV + DN_V:].astype(jnp.float32).reshape(b, l, 4, DN_HEADS)
    g = -jnp.exp(a_log.astype(jnp.float32)) * jax.nn.softplus(gb[:, :, :2] + dt_bias.astype(jnp.float32))
    beta = jax.nn.sigmoid(gb[:, :, 2:])
    return q, k, v, g, beta


def run_direction(streams, d, s0, with_output):
    q, k, v, g, beta = streams
    return delta_rule_chunked(orient(q, d), orient(k, d), orient(v, d),
                              orient(g[:, :, d], d), orient(beta[:, :, d], d), s0, with_output)


def gated_head_norm(o, z, w):
    b, l = o.shape[:2]
    y = o * lax.rsqrt(jnp.mean(o * o, axis=-1, keepdims=True) + EPS) * w.astype(jnp.float32)
    y = y * jax.nn.silu(z.astype(jnp.float32).reshape(b, l, DN_HEADS, DN_HEAD_DIM))
    return y.reshape(b, l, DN_V).astype(z.dtype)


def gated_deltanet(p_lat, p_ctx, rows, conv_w, a_log, dt_bias, norm_w, ctx_out):
    lat = dn_streams(p_lat, conv_w, a_log, dt_bias, rows)
    cx = dn_streams(p_ctx, conv_w, a_log, dt_bias, 1)
    b = p_lat.shape[0]
    o_lat, o_ctx = 0.0, 0.0
    for d in range(2):
        s0 = jnp.zeros((b, DN_HEADS, DN_HEAD_DIM, DN_HEAD_DIM), jnp.float32)
        oc, s_ctx = run_direction(cx, d, s0, ctx_out)
        ol, _ = run_direction(lat, d, s_ctx, True)
        o_lat = o_lat + orient(ol, d)
        if ctx_out:
            o_ctx = o_ctx + orient(oc, d)
    out_lat = gated_head_norm(o_lat, p_lat[..., DN_CONV:DN_CONV + DN_V], norm_w)
    out_ctx = gated_head_norm(o_ctx, p_ctx[..., DN_CONV:DN_CONV + DN_V], norm_w) if ctx_out else None
    return out_lat, out_ctx


def ssm_combine(e1, e2):
    a1, b1 = e1
    a2, b2 = e2
    return a1 * a2, a2 * b1 + b2


def s5_discretise(a_re, a_im, log_dt, b_mat):
    lam = lax.complex(a_re.astype(jnp.float32), a_im.astype(jnp.float32))
    a_bar = jnp.exp(lam * jnp.exp(log_dt.astype(jnp.float32))[:, None])
    b_bar = ((a_bar - 1.0) / lam)[..., None] * b_mat
    return a_bar, b_bar


def s5_scan(u, a_bar, b_bar, h0):
    bu = jnp.einsum('gps,blgs->blgp', b_bar, u.astype(jnp.complex64))
    if h0 is not None:
        bu = bu.at[:, 0].add(a_bar * h0)
    a = jnp.broadcast_to(a_bar, (1, u.shape[1]) + a_bar.shape)
    return lax.associative_scan(ssm_combine, (a, bu), axis=1)[1]


def s5_readout(h, u, c_mat, d_skip, w_glu):
    b, l = u.shape[:2]
    y = (jnp.real(jnp.einsum('gsp,blgp->blgs', c_mat, h))
         + d_skip.astype(jnp.float32).reshape(S5_GROUPS, S5_GROUP) * u)
    y = jax.nn.gelu(y.reshape(b, l, S5_WIDTH))
    ya, yb = jnp.split(y @ w_glu.astype(jnp.float32), 2, axis=-1)
    return ya * jax.nn.sigmoid(yb)


def s5_mixer(u_lat, u_ctx, rows, a_re, a_im, log_dt, b_re, b_im, c_re, c_im, d_skip, w_glu, ctx_out):
    b, l, _ = u_lat.shape
    ul = (u_lat.astype(jnp.float32).reshape(b, rows, GRID_W, S5_WIDTH)
          .transpose(0, 2, 1, 3).reshape(b, l, S5_GROUPS, S5_GROUP))
    uc = u_ctx.astype(jnp.float32).reshape(b, u_ctx.shape[1], S5_GROUPS, S5_GROUP)
    b_mat = lax.complex(b_re.astype(jnp.float32), b_im.astype(jnp.float32))
    c_mat = lax.complex(c_re.astype(jnp.float32), c_im.astype(jnp.float32))
    h_lat, h_ctx = 0.0, 0.0
    for d in range(2):
        a_bar, b_bar = s5_discretise(a_re[d], a_im[d], log_dt[d], b_mat)
        hc = s5_scan(orient(uc, d), a_bar, b_bar, None)
        hl = s5_scan(orient(ul, d), a_bar, b_bar, hc[:, -1])
        h_lat = h_lat + orient(hl, d)
        if ctx_out:
            h_ctx = h_ctx + orient(hc, d)
    y_lat = s5_readout(h_lat, ul, c_mat, d_skip, w_glu)
    y_lat = (y_lat.reshape(b, GRID_W, rows, S5_WIDTH).transpose(0, 2, 1, 3)
             .reshape(b, l, S5_WIDTH).astype(u_lat.dtype))
    y_ctx = s5_readout(h_ctx, uc, c_mat, d_skip, w_glu).astype(u_ctx.dtype) if ctx_out else None
    return y_lat, y_ctx


def trunk_layer(x, ctx, c, c_ctx, update_ctx, w_mod, b_mod, norm_ffn1, ffn1_up, ffn1_down,
                norm_mix, w_in, dn_conv, dn_a_log, dn_dt_bias, dn_norm,
                s5_a_re, s5_a_im, s5_log_dt, s5_b_re, s5_b_im, s5_c_re, s5_c_im, s5_d, s5_glu,
                w_out, norm_ffn2, ffn2_up, ffn2_down):
    rows = x.shape[1] // GRID_W
    m_lat = ada_params(c, w_mod, b_mod)
    m_ctx = ada_params(c_ctx, w_mod, b_mod)[None]
    x = x + 0.5 * m_lat[:, 2, None] * swiglu(modulated_norm(x, norm_ffn1, m_lat, 0), ffn1_up, ffn1_down)
    ctx = ctx + 0.5 * m_ctx[:, 2, None] * swiglu(modulated_norm(ctx, norm_ffn1, m_ctx, 0), ffn1_up, ffn1_down)
    p_lat = modulated_norm(x, norm_mix, m_lat, 1) @ w_in
    p_ctx = modulated_norm(ctx, norm_mix, m_ctx, 1) @ w_in
    dn_lat, dn_ctx = gated_deltanet(p_lat[..., :DN_COLS], p_ctx[..., :DN_COLS], rows,
                                    dn_conv, dn_a_log, dn_dt_bias, dn_norm, update_ctx)
    s5_lat, s5_ctx = s5_mixer(p_lat[..., DN_COLS:], p_ctx[..., DN_COLS:], rows,
                              s5_a_re, s5_a_im, s5_log_dt, s5_b_re, s5_b_im, s5_c_re, s5_c_im,
                              s5_d, s5_glu, update_ctx)
    x = x + m_lat[:, 5, None] * (jnp.concatenate([dn_lat, s5_lat], axis=-1) @ w_out)
    x = x + 0.5 * m_lat[:, 8, None] * swiglu(modulated_norm(x, norm_ffn2, m_lat, 2), ffn2_up, ffn2_down)
    if update_ctx:
        ctx = ctx + m_ctx[:, 5, None] * (jnp.concatenate([dn_ctx, s5_ctx], axis=-1) @ w_out)
        ctx = ctx + 0.5 * m_ctx[:, 8, None] * swiglu(modulated_norm(ctx, norm_ffn2, m_ctx, 2), ffn2_up, ffn2_down)
    return x, ctx


def setup_inputs(seed: int = 0) -> dict:
    key = jax.random.key(seed)
    ks = iter(list(jax.random.split(key, 40)))
    f32 = jnp.float32
    L = DEPTH

    def nrm(shape, scale):
        return jax.random.normal(next(ks), shape, f32) * scale

    def unif(shape, lo, hi):
        return jax.random.uniform(next(ks), shape, f32, lo, hi)

    dt = jnp.exp(unif((L, 2, DN_HEADS), math.log(1e-3), math.log(1e-1)))
    n_idx = jnp.arange(S5_STATE, dtype=f32)
    return {
        'x': nrm((BATCH, SEQ, D_MODEL), 1.0),
        'c': nrm((BATCH, D_MODEL), 1.0),
        'ctx': nrm((BATCH, CTX_LEN, D_MODEL), 1.0),
        'c_ctx': nrm((D_MODEL,), 1.0),
        'w_mod': nrm((L, D_MODEL, N_MOD * D_MODEL), D_MODEL ** -0.5),
        'b_mod': nrm((L, N_MOD * D_MODEL), 0.02),
        'norm_ffn1': 1.0 + nrm((L, D_MODEL), 0.02),
        'ffn1_up': nrm((L, D_MODEL, 2 * D_FF), D_MODEL ** -0.5),
        'ffn1_down': nrm((L, D_FF, D_MODEL), D_FF ** -0.5),
        'norm_mix': 1.0 + nrm((L, D_MODEL), 0.02),
        'w_in': nrm((L, D_MODEL, IN_COLS), D_MODEL ** -0.5),
        'dn_conv': nrm((L, CONV_K, CONV_K, DN_CONV), 1.0 / CONV_K),
        'dn_a_log': jnp.log(unif((L, 2, DN_HEADS), 1.0, 16.0)),
        'dn_dt_bias': dt + jnp.log(-jnp.expm1(-dt)),
        'dn_norm': 1.0 + nrm((L, DN_HEAD_DIM), 0.02),
        's5_a_re': -0.5 + nrm((L, 2, S5_GROUPS, S5_STATE), 0.01),
        's5_a_im': jnp.pi * n_idx + nrm((L, 2, S5_GROUPS, S5_STATE), 0.01),
        's5_log_dt': unif((L, 2, S5_GROUPS), math.log(1e-3), math.log(1e-1)),
        's5_b_re': nrm((L, S5_GROUPS, S5_STATE, S5_GROUP), (2 * S5_GROUP) ** -0.5),
        's5_b_im': nrm((L, S5_GROUPS, S5_STATE, S5_GROUP), (2 * S5_GROUP) ** -0.5),
        's5_c_re': nrm((L, S5_GROUPS, S5_GROUP, S5_STATE), S5_STATE ** -0.5),
        's5_c_im': nrm((L, S5_GROUPS, S5_GROUP, S5_STATE), S5_STATE ** -0.5),
        's5_d': nrm((L, S5_WIDTH), 1.0),
        's5_glu': nrm((L, S5_WIDTH, 2 * S5_WIDTH), S5_WIDTH ** -0.5),
        'w_out': nrm((L, D_MODEL, D_MODEL), D_MODEL ** -0.5),
        'norm_ffn2': 1.0 + nrm((L, D_MODEL), 0.02),
        'ffn2_up': nrm((L, D_MODEL, 2 * D_FF), D_MODEL ** -0.5),
        'ffn2_down': nrm((L, D_FF, D_MODEL), D_FF ** -0.5),
        'final_norm': 1.0 + nrm((D_MODEL,), 0.02),
    }


def reference(x, c, ctx, c_ctx, w_mod, b_mod, norm_ffn1, ffn1_up, ffn1_down, norm_mix, w_in,
              dn_conv, dn_a_log, dn_dt_bias, dn_norm, s5_a_re, s5_a_im, s5_log_dt, s5_b_re, s5_b_im,
              s5_c_re, s5_c_im, s5_d, s5_glu, w_out, norm_ffn2, ffn2_up, ffn2_down, final_norm):
    for i in range(DEPTH):
        x, ctx = trunk_layer(x, ctx, c, c_ctx, i + 1 < DEPTH, w_mod[i], b_mod[i],
                             norm_ffn1[i], ffn1_up[i], ffn1_down[i], norm_mix[i], w_in[i],
                             dn_conv[i], dn_a_log[i], dn_dt_bias[i], dn_norm[i],
                             s5_a_re[i], s5_a_im[i], s5_log_dt[i], s5_b_re[i], s5_b_im[i],
                             s5_c_re[i], s5_c_im[i], s5_d[i], s5_glu[i],
                             w_out[i], norm_ffn2[i], ffn2_up[i], ffn2_down[i])
    return rmsnorm(x, final_norm)
```

```python
import functools

import jax
import jax.numpy as jnp
from jax import lax
from jax.experimental import pallas as pl
from jax.experimental.pallas import tpu as pltpu

F32 = jnp.float32
BF16 = jnp.bfloat16
EPS = 1e-6
N_MOD = 9
GRID_W = 64
CHUNK = 64
HEAD_DIM = 128
S5_GROUP = 16
S5_SUB = 8
LANES = 128
MIB = 1024 * 1024


def _params(semantics, vmem_mib):
    return pltpu.CompilerParams(dimension_semantics=semantics, vmem_limit_bytes=vmem_mib * MIB)


def _sigmoid(x):
    return 1.0 / (1.0 + jnp.exp(-x))


def _silu(x):
    return x * _sigmoid(x)


def _dot(a, b):
    return jnp.dot(a, b, preferred_element_type=F32)


def _dot_nt(a, b):
    return lax.dot_general(a, b, (((1,), (1,)), ((), ())), preferred_element_type=F32)


def _dot_tn(a, b):
    return lax.dot_general(a, b, (((0,), (0,)), ((), ())), preferred_element_type=F32)


def _modnorm(x, norm_w, scale, shift):
    y = x * lax.rsqrt(jnp.mean(x * x, axis=-1, keepdims=True) + EPS) * norm_w
    return y * (1.0 + scale) + shift


def _ada_kernel(c_ref, w_ref, b_ref, o_ref):
    a = _silu(c_ref[...]).astype(BF16)
    o_ref[...] = _dot(a, w_ref[...].astype(BF16)) + b_ref[...]


def _ada(cond, w_mod, b_mod):
    rows, d = cond.shape
    n = w_mod.shape[1]
    tn = 1024
    return pl.pallas_call(
        _ada_kernel,
        grid=(n // tn,),
        in_specs=[pl.BlockSpec((rows, d), lambda j: (0, 0)),
                  pl.BlockSpec((d, tn), lambda j: (0, j)),
                  pl.BlockSpec((1, tn), lambda j: (0, j))],
        out_specs=pl.BlockSpec((rows, tn), lambda j: (0, j)),
        out_shape=jax.ShapeDtypeStruct((rows, n), F32),
        compiler_params=_params(("parallel",), 40),
        name="ada_mod",
    )(cond, w_mod, b_mod.reshape(1, n))


def _ffn_kernel(x_ref, m_ref, nw_ref, wg_ref, wu_ref, wd_ref, *rest, sub, final):
    if final:
        fin_ref, o_ref, h_scr, acc_scr = rest
    else:
        o_ref, h_scr, acc_scr = rest
    f = pl.program_id(1)

    @pl.when(f == 0)
    def _():
        m = m_ref[0]
        h = _modnorm(x_ref[...], nw_ref[...], m[3 * sub + 1:3 * sub + 2], m[3 * sub:3 * sub + 1])
        h_scr[...] = h.astype(BF16)
        acc_scr[...] = jnp.zeros_like(acc_scr)

    h = h_scr[...]
    g = _dot(h, wg_ref[...])
    u = _dot(h, wu_ref[...])
    acc_scr[...] += _dot((_silu(g) * u).astype(BF16), wd_ref[...])

    @pl.when(f == pl.num_programs(1) - 1)
    def _():
        m = m_ref[0]
        y = x_ref[...] + 0.5 * m[3 * sub + 2:3 * sub + 3] * acc_scr[...]
        if final:
            y = y * lax.rsqrt(jnp.mean(y * y, axis=-1, keepdims=True) + EPS) * fin_ref[...]
        o_ref[...] = y


def _ffn(x2, mods, norm_w, w_up, w_down, sub, tokens_per_mod, final_w=None):
    t, d = x2.shape
    f = w_down.shape[0]
    tm, tf = 512, 512
    nf = f // tf
    tiles_per_mod = tokens_per_mod // tm
    final = final_w is not None
    in_specs = [pl.BlockSpec((tm, d), lambda i, j: (i, 0)),
                pl.BlockSpec((1, N_MOD, d), lambda i, j: (i // tiles_per_mod, 0, 0)),
                pl.BlockSpec((1, d), lambda i, j: (0, 0)),
                pl.BlockSpec((d, tf), lambda i, j: (0, j)),
                pl.BlockSpec((d, tf), lambda i, j: (0, nf + j)),
                pl.BlockSpec((tf, d), lambda i, j: (j, 0))]
    args = [x2, mods, norm_w.reshape(1, d), w_up, w_up, w_down]
    if final:
        in_specs.append(pl.BlockSpec((1, d), lambda i, j: (0, 0)))
        args.append(final_w.reshape(1, d))
    return pl.pallas_call(
        functools.partial(_ffn_kernel, sub=sub, final=final),
        grid=(t // tm, nf),
        in_specs=in_specs,
        out_specs=pl.BlockSpec((tm, d), lambda i, j: (i, 0)),
        out_shape=jax.ShapeDtypeStruct((t, d), F32),
        scratch_shapes=[pltpu.VMEM((tm, d), BF16), pltpu.VMEM((tm, d), F32)],
        compiler_params=_params(("parallel", "arbitrary"), 48),
        name="ffn_final" if final else "ffn",
    )(*args)


def _proj_kernel(x_ref, m_ref, nw_ref, w_ref, o_ref, h_scr, *, sub):
    @pl.when(pl.program_id(1) == 0)
    def _():
        m = m_ref[0]
        h = _modnorm(x_ref[...], nw_ref[...], m[3 * sub + 1:3 * sub + 2], m[3 * sub:3 * sub + 1])
        h_scr[...] = h.astype(BF16)

    o_ref[...] = _dot(h_scr[...], w_ref[...])


def _proj(x2, mods, norm_w, w, sub, tokens_per_mod):
    t, d = x2.shape
    n = w.shape[1]
    tm, tn = 512, 1024
    tiles_per_mod = tokens_per_mod // tm
    return pl.pallas_call(
        functools.partial(_proj_kernel, sub=sub),
        grid=(t // tm, n // tn),
        in_specs=[pl.BlockSpec((tm, d), lambda i, j: (i, 0)),
                  pl.BlockSpec((1, N_MOD, d), lambda i, j: (i // tiles_per_mod, 0, 0)),
                  pl.BlockSpec((1, d), lambda i, j: (0, 0)),
                  pl.BlockSpec((d, tn), lambda i, j: (0, j))],
        out_specs=pl.BlockSpec((tm, tn), lambda i, j: (i, j)),
        out_shape=jax.ShapeDtypeStruct((t, n), F32),
        scratch_shapes=[pltpu.VMEM((tm, d), BF16)],
        compiler_params=_params(("parallel", "arbitrary"), 40),
        name="in_proj",
    )(x2, mods, norm_w.reshape(1, d), w)


def _conv_kernel(p_ref, w_ref, o_ref, xp_scr, *, rows, width, strip, n_q_tiles, n_qk_tiles):
    length = rows * width
    pad = (xp_scr.shape[0] - length) // 2
    tc = p_ref.shape[2]
    j = pl.program_id(1)
    xp_scr[0:pad, :] = jnp.zeros((pad, tc), F32)
    xp_scr[pad + length:pad + length + pad, :] = jnp.zeros((pad, tc), F32)
    xp_scr[pad:pad + length, :] = p_ref[0]
    w = w_ref[...]
    q_scale = jnp.where(j < n_q_tiles, HEAD_DIM ** -0.5, 1.0).astype(F32)
    is_qk = j < n_qk_tiles
    row_taps = (-1, 0, 1) if rows > 1 else (0,)
    for s in range(length // strip):
        t0 = s * strip
        col = (lax.broadcasted_iota(jnp.int32, (strip, 1), 0) + t0) & (width - 1)
        acc = jnp.zeros((strip, tc), F32)
        for dr in row_taps:
            for dc in (-1, 0, 1):
                off = pad + t0 + dr * width + dc
                xs = xp_scr[off:off + strip, :]
                if dc == -1:
                    xs = jnp.where(col >= 1, xs, 0.0)
                elif dc == 1:
                    xs = jnp.where(col <= width - 2, xs, 0.0)
                tap = (dr + 1) * 3 + (dc + 1)
                acc = acc + xs * w[tap:tap + 1, :]
        y = _silu(acc)
        for hh in range(tc // HEAD_DIM):
            yh = y[:, hh * HEAD_DIM:(hh + 1) * HEAD_DIM]
            inv = lax.rsqrt(jnp.sum(yh * yh, axis=-1, keepdims=True) + EPS) * q_scale
            o_ref[0, t0:t0 + strip, hh * HEAD_DIM:(hh + 1) * HEAD_DIM] = yh * jnp.where(is_qk, inv, 1.0)


def _conv(p3, conv_w9, rows, width, n_conv, n_qk):
    b, length, _ = p3.shape
    tc = 512
    pad = width + 8 if rows > 1 else 8
    strip = min(256, length)
    kern = functools.partial(_conv_kernel, rows=rows, width=width, strip=strip,
                             n_q_tiles=(n_qk // 2) // tc, n_qk_tiles=n_qk // tc)
    return pl.pallas_call(
        kern,
        grid=(b, n_conv // tc),
        in_specs=[pl.BlockSpec((1, length, tc), lambda i, j: (i, 0, j)),
                  pl.BlockSpec((9, tc), lambda i, j: (0, j))],
        out_specs=pl.BlockSpec((1, length, tc), lambda i, j: (i, 0, j)),
        out_shape=jax.ShapeDtypeStruct((b, length, n_conv), F32),
        scratch_shapes=[pltpu.VMEM((length + 2 * pad, tc), F32)],
        compiler_params=_params(("parallel", "parallel"), 40),
        name="grid_conv",
    )(p3, conv_w9)


def _unit_tri_inverse(a, strict, row, col):
    n = 1
    e = None
    while n < CHUNK:
        k = n.bit_length() - 1
        pair = ((row >> (k + 1)) == (col >> (k + 1))) & ((row >> k) != (col >> k)) & strict
        l_n = jnp.where(pair, a, 0.0)
        if e is None:
            e = -l_n
        else:
            e16 = e.astype(BF16)
            y = l_n + _dot(l_n.astype(BF16), e16)
            e = e - y - _dot(e16, y.astype(BF16))
        n *= 2
    return e


def _delta_kernel(*refs, n_heads, with_init, with_output):
    q_ref, k_ref, v_ref, gb_ref, alog_ref, dtb_ref = refs[:6]
    rest = refs[6:]
    if with_init:
        s0_ref, rest = rest[0], rest[1:]
    if with_output:
        o_ref, s_scr = rest
    else:
        sfin_ref, s_scr = rest
    d = pl.program_id(1)
    c = pl.program_id(2)

    @pl.when(c == 0)
    def _():
        if with_init:
            s_scr[...] = s0_ref[0, 0]
        else:
            s_scr[...] = jnp.zeros_like(s_scr)

    row = lax.broadcasted_iota(jnp.int32, (CHUNK, CHUNK), 0)
    col = lax.broadcasted_iota(jnp.int32, (CHUNK, CHUNK), 1)
    fwd = d == 0
    later, earlier = jnp.where(fwd, row, col), jnp.where(fwd, col, row)
    incl = later >= earlier
    strict = later > earlier

    gb = gb_ref[0]
    x = gb + dtb_ref[0]
    softplus = jnp.maximum(x, 0.0) + jnp.log1p(jnp.exp(-jnp.abs(x)))
    g_all = -jnp.exp(alog_ref[0]) * softplus
    beta_all = _sigmoid(gb)
    gcum = jnp.dot(incl.astype(F32), g_all, precision=lax.Precision.HIGHEST,
                   preferred_element_type=F32)
    gtot = jnp.sum(g_all, axis=0, keepdims=True)
    gcum_t = jnp.concatenate([gcum, jnp.zeros_like(gcum)], axis=0).T

    for h in range(n_heads):
        sl = slice(h * HEAD_DIM, (h + 1) * HEAD_DIM)
        gc = gcum[:, h:h + 1]
        gr = gcum_t[h:h + 1, 0:CHUNK]
        gt = gtot[:, h:h + 1]
        bt = beta_all[:, n_heads + h:n_heads + h + 1]
        decay = jnp.where(incl, jnp.exp(jnp.where(incl, gc - gr, 0.0)), 0.0)
        kh = k_ref[0, :, sl]
        vh = v_ref[0, :, sl]
        k16 = kh.astype(BF16)
        a_mat = jnp.where(strict, bt * decay * _dot_nt(k16, k16), 0.0)
        e = _unit_tri_inverse(a_mat, strict, row, col)
        eg = jnp.exp(gc)
        rhs = jnp.concatenate([(bt * eg) * kh, bt * vh], axis=1)
        sol = rhs + _dot(e.astype(BF16), rhs.astype(BF16))
        w = sol[:, :HEAD_DIM]
        u = sol[:, HEAD_DIM:]
        s_h = s_scr[h]
        s16 = s_h.astype(BF16)
        u_i = u - _dot(w.astype(BF16), s16)
        u16 = u_i.astype(BF16)
        if with_output:
            qh = q_ref[0, :, sl]
            qk = _dot_nt(qh.astype(BF16), k16) * decay
            o = _dot((qh * eg).astype(BF16), s16) + _dot(qk.astype(BF16), u16)
            o_ref[0, 0, :, sl] = o
        k_dec = kh * jnp.exp(gt - gc)
        s_scr[h] = jnp.exp(gt) * s_h + _dot_tn(k_dec.astype(BF16), u16)

    if not with_output:
        @pl.when(c == pl.num_programs(2) - 1)
        def _():
            sfin_ref[0, 0] = s_scr[...]


def _delta(qkv, p3, gate_block0, alog_r, dtb_r, n_heads, s0=None):
    b, length, _ = qkv.shape
    nch = length // CHUNK
    hd = n_heads * HEAD_DIM
    with_init = s0 is not None
    with_output = with_init

    def pos(d, c):
        return jnp.where(d == 0, c, nch - 1 - c)

    in_specs = [pl.BlockSpec((1, CHUNK, hd), lambda i, d, c: (i, pos(d, c), 0)),
                pl.BlockSpec((1, CHUNK, hd), lambda i, d, c: (i, pos(d, c), 1)),
                pl.BlockSpec((1, CHUNK, hd), lambda i, d, c: (i, pos(d, c), 2)),
                pl.BlockSpec((1, CHUNK, LANES), lambda i, d, c: (i, pos(d, c), gate_block0 + d)),
                pl.BlockSpec((1, 1, LANES), lambda i, d, c: (d, 0, 0)),
                pl.BlockSpec((1, 1, LANES), lambda i, d, c: (d, 0, 0))]
    args = [qkv, qkv, qkv, p3, alog_r, dtb_r]
    state_spec = pl.BlockSpec((1, 1, n_heads, HEAD_DIM, HEAD_DIM), lambda i, d, c: (i, d, 0, 0, 0))
    if with_init:
        in_specs.append(state_spec)
        args.append(s0)
    if with_output:
        out_specs = pl.BlockSpec((1, 1, CHUNK, hd), lambda i, d, c: (d, i, pos(d, c), 0))
        out_shape = jax.ShapeDtypeStruct((2, b, length, hd), F32)
    else:
        out_specs = state_spec
        out_shape = jax.ShapeDtypeStruct((b, 2, n_heads, HEAD_DIM, HEAD_DIM), F32)
    return pl.pallas_call(
        functools.partial(_delta_kernel, n_heads=n_heads, with_init=with_init, with_output=with_output),
        grid=(b, 2, nch),
        in_specs=in_specs,
        out_specs=out_specs,
        out_shape=out_shape,
        scratch_shapes=[pltpu.VMEM((n_heads, HEAD_DIM, HEAD_DIM), F32)],
        compiler_params=_params(("parallel", "parallel", "arbitrary"), 32),
        name="delta_lat" if with_output else "delta_ctx",
    )(*args)


def _cmul(ar, ai, br, bi):
    return ar * br - ai * bi, ar * bi + ai * br


def _s5_operators(a_re, a_im, log_dt, b_re, b_im, c_re, c_im):
    g, p, s = b_re.shape
    dt = jnp.exp(log_dt.astype(F32))[..., None]
    lr, li = a_re.astype(F32) * dt, a_im.astype(F32) * dt

    def powers(n):
        n = n.astype(F32)
        shape = lr.shape + (1,) * n.ndim
        mag = jnp.exp(lr.reshape(shape) * n)
        ang = li.reshape(shape) * n
        return mag * jnp.cos(ang), mag * jnp.sin(ang)

    abr, abi = powers(jnp.ones((), F32))
    nr, ni = abr - 1.0, abi
    cr, ci = a_re.astype(F32), a_im.astype(F32)
    den = cr * cr + ci * ci
    fr, fi = (nr * cr + ni * ci) / den, (ni * cr - nr * ci) / den
    bbr, bbi = _cmul(fr[..., None], fi[..., None], b_re.astype(F32)[None], b_im.astype(F32)[None])
    ccr = jnp.swapaxes(c_re.astype(F32), 1, 2)
    cci = jnp.swapaxes(c_im.astype(F32), 1, 2)

    nsub = CHUNK // S5_SUB
    dl = jnp.arange(nsub)[:, None]
    jj = jnp.arange(S5_SUB)[None, :]
    ef = powers(S5_SUB * dl - jj)
    eb = powers(S5_SUB * dl + jj)
    yf = powers(jnp.arange(S5_SUB))
    yb = powers(-jnp.arange(S5_SUB))

    def xz_table(e, d):
        er, ei = e[0][d], e[1][d]
        xr, xi = _cmul(bbr[d][:, :, None, None, :], bbi[d][:, :, None, None, :], er[..., None], ei[..., None])
        xr = jnp.transpose(xr, (0, 2, 3, 4, 1)).reshape(g, nsub, S5_SUB * s, p)
        xi = jnp.transpose(xi, (0, 2, 3, 4, 1)).reshape(g, nsub, S5_SUB * s, p)
        return jnp.concatenate([xr, xi], axis=-1)

    def yq_table(y, d):
        yr, yi = _cmul(ccr[:, :, None, :], cci[:, :, None, :], y[0][d][..., None], y[1][d][..., None])
        yr = yr.reshape(g, p, S5_SUB * s)
        yi = yi.reshape(g, p, S5_SUB * s)
        return jnp.concatenate([yr, -yi], axis=1)

    xzf, yqf = xz_table(ef, 0), yq_table(yf, 0)
    xzb, yqb = xz_table(eb, 1), yq_table(yb, 1)

    tok = jnp.arange(CHUNK)

    def state_in(e, d):
        er, ei = e[0][d], e[1][d]
        xr, xi = _cmul(bbr[d][:, :, None, :], bbi[d][:, :, None, :], er[..., None], ei[..., None])
        xr = jnp.transpose(xr, (0, 2, 3, 1)).reshape(g, CHUNK * s, p)
        xi = jnp.transpose(xi, (0, 2, 3, 1)).reshape(g, CHUNK * s, p)
        return xr, xi

    def state_out(e, d):
        er, ei = e[0][d], e[1][d]
        yr, yi = _cmul(ccr[:, :, None, :], cci[:, :, None, :], er[..., None], ei[..., None])
        return yr.reshape(g, p, CHUNK * s), -yi.reshape(g, p, CHUNK * s)

    bst = jnp.concatenate(state_in(powers(CHUNK - 1 - tok), 0) + state_in(powers(tok), 1), axis=-1)
    cst = jnp.concatenate(state_out(powers(tok + 1), 0) + state_out(powers(CHUNK - tok), 1), axis=1)
    a64r, a64i = powers(jnp.full((), CHUNK, F32))
    rot = jnp.stack([jnp.concatenate([a64r[0], a64r[0]], -1), jnp.concatenate([-a64i[0], a64i[0]], -1),
                     jnp.concatenate([a64r[1], a64r[1]], -1), jnp.concatenate([-a64i[1], a64i[1]], -1)], axis=1)
    return xzf, yqf, xzb, yqb, bst.astype(BF16), cst.astype(BF16), rot


def _s5_kernel(uc_ref, ul_ref, xzf_ref, yqf_ref, xzb_ref, yqb_ref, dsk_ref, bst_ref, cst_ref, rot_ref,
               y_ref, m_scr, hin_scr, *, nb):
    nsub = CHUNK // S5_SUB
    bst = bst_ref[0]
    rot = rot_ref[0]
    fa, fb, ba, bb = rot[0:1], rot[1:2], rot[2:3], rot[3:4]

    def step(h, mul_a, mul_b, inp):
        return mul_a * h + mul_b * pltpu.roll(h, LANES // 2, axis=1) + inp

    hc = _dot(uc_ref[0], bst)
    n_ctx = uc_ref.shape[1] // nb
    hf = jnp.zeros((nb, LANES), F32)
    hb = jnp.zeros((nb, LANES), F32)
    for c in range(n_ctx):
        hf = step(hf, fa, fb, hc[c * nb:(c + 1) * nb, 0:LANES])
    for c in reversed(range(n_ctx)):
        hb = step(hb, ba, bb, hc[c * nb:(c + 1) * nb, LANES:2 * LANES])

    ul = ul_ref[0]
    hl = _dot(ul, bst)
    n_lat = ul_ref.shape[1] // nb
    for c in range(n_lat):
        hin_scr[c * nb:(c + 1) * nb, 0:LANES] = hf
        hf = step(hf, fa, fb, hl[c * nb:(c + 1) * nb, 0:LANES])
    for c in reversed(range(n_lat)):
        hin_scr[c * nb:(c + 1) * nb, LANES:2 * LANES] = hb
        hb = step(hb, ba, bb, hl[c * nb:(c + 1) * nb, LANES:2 * LANES])

    row = lax.broadcasted_iota(jnp.int32, (LANES, LANES), 0)
    col = lax.broadcasted_iota(jnp.int32, (LANES, LANES), 1)
    yqf = yqf_ref[0]
    yqb = yqb_ref[0]

    def tap(xz_ref, yq, delta):
        return jnp.dot(xz_ref[0, delta], yq, precision=lax.Precision.HIGHEST, preferred_element_type=F32)

    jj, ii = row >> 4, col >> 4
    diag = (jnp.where(ii >= jj, tap(xzf_ref, yqf, 0), 0.0) + jnp.where(jj >= ii, tap(xzb_ref, yqb, 0), 0.0)
            + jnp.where(row == col, dsk_ref[0], 0.0))
    taps_f = [None] + [tap(xzf_ref, yqf, dl).astype(BF16) for dl in range(1, nsub)]
    taps_b = [None] + [tap(xzb_ref, yqb, dl).astype(BF16) for dl in range(1, nsub)]
    diag = diag.astype(BF16)
    for bj in range(nsub):
        for bi in range(nsub):
            blk = diag if bi == bj else (taps_f[bi - bj] if bi > bj else taps_b[bj - bi])
            m_scr[bj * LANES:(bj + 1) * LANES, bi * LANES:(bi + 1) * LANES] = blk

    y_ref[0] = _dot(ul, m_scr[...]) + _dot(hin_scr[...].astype(BF16), cst_ref[0])


def _s5(uc, ul, ops, d_skip, nb):
    xzf, yqf, xzb, yqb, bst, cst, rot = ops
    g, rows_l, kdim = ul.shape
    rows_c = uc.shape[1]
    nsub = CHUNK // S5_SUB
    p2 = xzf.shape[-1]
    dsk = jnp.tile(d_skip.astype(F32).reshape(g, 1, S5_GROUP), (1, 1, S5_SUB))

    def spec(shape):
        nd = len(shape)
        return pl.BlockSpec((1,) + tuple(shape[1:]), lambda i: (i,) + (0,) * (nd - 1))

    arrays = [uc, ul, xzf, yqf, xzb, yqb, dsk, bst, cst, rot]
    return pl.pallas_call(
        functools.partial(_s5_kernel, nb=nb),
        grid=(g,),
        in_specs=[spec(a.shape) for a in arrays],
        out_specs=pl.BlockSpec((1, rows_l, kdim), lambda i: (i, 0, 0)),
        out_shape=jax.ShapeDtypeStruct((g, rows_l, kdim), F32),
        scratch_shapes=[pltpu.VMEM((kdim, kdim), BF16), pltpu.VMEM((rows_l, 4 * (p2 // 2)), F32)],
        compiler_params=_params(("parallel",), 32),
        name="s5_scan",
    )(*arrays)


def _gelu_tanh(x):
    return 0.5 * x * (1.0 + jnp.tanh(0.7978845608028654 * (x + 0.044715 * (x * x * x))))


def _out_kernel(o_ref, z_ref, y_ref, x_ref, m_ref, dnw_ref, wglu_ref, wdn_ref, ws5_ref, out_ref, *, n_heads):
    o = o_ref[0] + o_ref[1]
    z = z_ref[...]
    dnw = dnw_ref[...]
    parts = []
    for h in range(n_heads):
        sl = slice(h * HEAD_DIM, (h + 1) * HEAD_DIM)
        oh = o[:, sl]
        yh = oh * lax.rsqrt(jnp.mean(oh * oh, axis=-1, keepdims=True) + EPS) * dnw
        parts.append((yh * _silu(z[:, sl])).astype(BF16))
    dn = jnp.concatenate(parts, axis=1)
    t = _dot(_gelu_tanh(y_ref[...]).astype(BF16), wglu_ref[...])
    half = t.shape[1] // 2
    s5 = (t[:, :half] * _sigmoid(t[:, half:])).astype(BF16)
    acc = _dot(dn, wdn_ref[...]) + _dot(s5, ws5_ref[...])
    out_ref[...] = x_ref[...] + m_ref[0][5:6] * acc


def _out_proj(o2, p2, z_block, y2, x2, mods, dn_norm, w_glu, w_dn, w_s5, n_heads, tokens_per_mod):
    t, d = x2.shape
    hd = n_heads * HEAD_DIM
    s5w = y2.shape[1]
    tm = 256
    tiles_per_mod = tokens_per_mod // tm
    return pl.pallas_call(
        functools.partial(_out_kernel, n_heads=n_heads),
        grid=(t // tm,),
        in_specs=[pl.BlockSpec((2, tm, hd), lambda i: (0, i, 0)),
                  pl.BlockSpec((tm, hd), lambda i: (i, z_block)),
                  pl.BlockSpec((tm, s5w), lambda i: (i, 0)),
                  pl.BlockSpec((tm, d), lambda i: (i, 0)),
                  pl.BlockSpec((1, N_MOD, d), lambda i: (i // tiles_per_mod, 0, 0)),
                  pl.BlockSpec((1, HEAD_DIM), lambda i: (0, 0)),
                  pl.BlockSpec(w_glu.shape, lambda i: (0, 0)),
                  pl.BlockSpec(w_dn.shape, lambda i: (0, 0)),
                  pl.BlockSpec(w_s5.shape, lambda i: (0, 0))],
        out_specs=pl.BlockSpec((tm, d), lambda i: (i, 0)),
        out_shape=jax.ShapeDtypeStruct((t, d), F32),
        compiler_params=_params(("parallel",), 48),
        name="out_proj",
    )(o2, p2, y2, x2, mods, dn_norm.reshape(1, HEAD_DIM), w_glu, w_dn, w_s5)


def _layer(x, ctx, m_lat, m_ctx, norm_ffn1, ffn1_up, ffn1_down, norm_mix, w_in, dn_conv, dn_a_log,
           dn_dt_bias, dn_norm, s5_a_re, s5_a_im, s5_log_dt, s5_b_re, s5_b_im, s5_c_re, s5_c_im, s5_d,
           s5_glu, w_out, norm_ffn2, ffn2_up, ffn2_down, final_norm):
    b, length, d = x.shape
    lc = ctx.shape[1]
    n_heads = dn_a_log.shape[1]
    hd = n_heads * HEAD_DIM
    n_conv = dn_conv.shape[-1]
    n_qk = n_conv - hd
    s5w = s5_d.shape[0]
    groups = s5w // S5_GROUP
    rows = length // GRID_W

    gate0 = n_conv + hd
    w_gates = w_in[:, gate0:gate0 + 4 * n_heads].reshape(d, 4, n_heads)
    zpad = jnp.zeros((d, LANES - 2 * n_heads), w_in.dtype)
    w_gate_dirs = [jnp.concatenate([w_gates[:, dr], w_gates[:, 2 + dr], zpad], axis=1) for dr in range(2)]
    n_used = gate0 + s5w + 2 * LANES
    n_pad = -n_used % 1024
    w_in_r = jnp.concatenate([w_in[:, :gate0], w_in[:, gate0 + 4 * n_heads:]] + w_gate_dirs
                             + [jnp.zeros((d, n_pad), w_in.dtype)], axis=1).astype(BF16)
    n_proj = w_in_r.shape[1]
    z_block = n_conv // hd
    s5_col0 = gate0
    gate_block0 = (gate0 + s5w) // LANES
    lane_pad = jnp.zeros((2, LANES - n_heads), F32)
    alog_r = jnp.concatenate([dn_a_log.astype(F32), lane_pad], axis=1).reshape(2, 1, LANES)
    dtb_r = jnp.concatenate([dn_dt_bias.astype(F32), lane_pad], axis=1).reshape(2, 1, LANES)
    conv_w9 = dn_conv.reshape(9, n_conv)

    x2 = x.reshape(b * length, d)
    c2 = ctx.reshape(b * lc, d)

    up1, down1 = ffn1_up.astype(BF16), ffn1_down.astype(BF16)
    x2 = _ffn(x2, m_lat, norm_ffn1, up1, down1, 0, length)
    c2 = _ffn(c2, m_ctx, norm_ffn1, up1, down1, 0, b * lc)

    p_lat = _proj(x2, m_lat, norm_mix, w_in_r, 1, length)
    p_ctx = _proj(c2, m_ctx, norm_mix, w_in_r, 1, b * lc)
    p_lat3 = p_lat.reshape(b, length, n_proj)
    p_ctx3 = p_ctx.reshape(b, lc, n_proj)

    qkv_lat = _conv(p_lat3, conv_w9, rows, GRID_W, n_conv, n_qk)
    qkv_ctx = _conv(p_ctx3, conv_w9, 1, lc, n_conv, n_qk)
    s_ctx = _delta(qkv_ctx, p_ctx3, gate_block0, alog_r, dtb_r, n_heads)
    o_dirs = _delta(qkv_lat, p_lat3, gate_block0, alog_r, dtb_r, n_heads, s0=s_ctx)

    ops = _s5_operators(s5_a_re, s5_a_im, s5_log_dt, s5_b_re, s5_b_im, s5_c_re, s5_c_im)
    cols_per_chunk = CHUNK // rows
    u_lat = p_lat3[:, :, s5_col0:s5_col0 + s5w].astype(BF16)
    u_lat = u_lat.reshape(b, rows, GRID_W // cols_per_chunk, cols_per_chunk, groups, S5_GROUP)
    u_lat = jnp.transpose(u_lat, (4, 2, 0, 3, 1, 5)).reshape(groups, (length // CHUNK) * b, CHUNK * S5_GROUP)
    u_ctx = p_ctx3[:, :, s5_col0:s5_col0 + s5w].astype(BF16)
    u_ctx = u_ctx.reshape(b, lc // CHUNK, CHUNK, groups, S5_GROUP)
    u_ctx = jnp.transpose(u_ctx, (3, 1, 0, 2, 4)).reshape(groups, (lc // CHUNK) * b, CHUNK * S5_GROUP)
    y = _s5(u_ctx, u_lat, ops, s5_d, b)
    y = y.reshape(groups, GRID_W // cols_per_chunk, b, cols_per_chunk, rows, S5_GROUP)
    y2 = jnp.transpose(y, (2, 4, 1, 3, 0, 5)).reshape(b * length, s5w)

    w_out16 = w_out.astype(BF16)
    x2 = _out_proj(o_dirs.reshape(2, b * length, hd), p_lat, z_block, y2, x2, m_lat, dn_norm,
                   s5_glu.astype(BF16), w_out16[:hd], w_out16[hd:], n_heads, length)

    x2 = _ffn(x2, m_lat, norm_ffn2, ffn2_up.astype(BF16), ffn2_down.astype(BF16), 2, length,
              final_w=final_norm)
    return x2.reshape(b, length, d)


def kernel(x, c, ctx, c_ctx, w_mod, b_mod, norm_ffn1, ffn1_up, ffn1_down, norm_mix, w_in, dn_conv, dn_a_log, dn_dt_bias, dn_norm, s5_a_re, s5_a_im, s5_log_dt, s5_b_re, s5_b_im, s5_c_re, s5_c_im, s5_d, s5_glu, w_out, norm_ffn2, ffn2_up, ffn2_down, final_norm):
    depth = w_mod.shape[0]
    assert depth == 1, "the context stream update of deeper stacks is not implemented"
    b, _, d = x.shape
    cond = jnp.concatenate([c, c_ctx[None], jnp.zeros((16 - b - 1, d), c.dtype)], axis=0)
    m = _ada(cond, w_mod[0], b_mod[0]).reshape(16, N_MOD, d)
    return _layer(x, ctx, m[:b], m[b:b + 1], norm_ffn1[0], ffn1_up[0], ffn1_down[0], norm_mix[0], w_in[0],
                  dn_conv[0], dn_a_log[0], dn_dt_bias[0], dn_norm[0], s5_a_re[0], s5_a_im[0], s5_log_dt[0],
                  s5_b_re[0], s5_b_im[0], s5_c_re[0], s5_c_im[0], s5_d[0], s5_glu[0], w_out[0],
                  norm_ffn2[0], ffn2_up[0], ffn2_down[0], final_norm)
```

```python
import functools

import jax
import jax.numpy as jnp
from jax import lax
from jax.experimental import pallas as pl
from jax.experimental.pallas import tpu as pltpu

F32 = jnp.float32
BF16 = jnp.bfloat16
EPS = 1e-6
N_MOD = 9
GRID_W = 64
CHUNK = 64
HEAD_DIM = 128
S5_GROUP = 16
S5_SUB = 8
LANES = 128
MIB = 1024 * 1024


def _params(semantics, vmem_mib):
    return pltpu.CompilerParams(dimension_semantics=semantics, vmem_limit_bytes=vmem_mib * MIB)


def _sigmoid(x):
    return 1.0 / (1.0 + jnp.exp(-x))


def _silu(x):
    return x * _sigmoid(x)


def _dot(a, b):
    return jnp.dot(a, b, preferred_element_type=F32)


def _dot_nt(a, b):
    return lax.dot_general(a, b, (((1,), (1,)), ((), ())), preferred_element_type=F32)


def _dot_tn(a, b):
    return lax.dot_general(a, b, (((0,), (0,)), ((), ())), preferred_element_type=F32)


def _modnorm(x, norm_w, scale, shift):
    y = x * lax.rsqrt(jnp.mean(x * x, axis=-1, keepdims=True) + EPS) * norm_w
    return y * (1.0 + scale) + shift


def _ada_kernel(c_ref, w_ref, b_ref, o_ref):
    a = _silu(c_ref[...]).astype(BF16)
    o_ref[...] = _dot(a, w_ref[...].astype(BF16)) + b_ref[...]


def _ada(cond, w_mod, b_mod):
    rows, d = cond.shape
    n = w_mod.shape[1]
    tn = 1024
    return pl.pallas_call(
        _ada_kernel,
        grid=(n // tn,),
        in_specs=[pl.BlockSpec((rows, d), lambda j: (0, 0)),
                  pl.BlockSpec((d, tn), lambda j: (0, j)),
                  pl.BlockSpec((1, tn), lambda j: (0, j))],
        out_specs=pl.BlockSpec((rows, tn), lambda j: (0, j)),
        out_shape=jax.ShapeDtypeStruct((rows, n), F32),
        compiler_params=_params(("parallel",), 40),
        name="ada_mod",
    )(cond, w_mod, b_mod.reshape(1, n))


def _ffn_kernel(x_ref, m_ref, nw_ref, wg_ref, wu_ref, wd_ref, *rest, sub, final):
    if final:
        fin_ref, o_ref, h_scr, acc_scr = rest
    else:
        o_ref, h_scr, acc_scr = rest
    f = pl.program_id(1)

    @pl.when(f == 0)
    def _():
        m = m_ref[0]
        h = _modnorm(x_ref[...], nw_ref[...], m[3 * sub + 1:3 * sub + 2], m[3 * sub:3 * sub + 1])
        h_scr[...] = h.astype(BF16)
        acc_scr[...] = jnp.zeros_like(acc_scr)

    h = h_scr[...]
    g = _dot(h, wg_ref[...])
    u = _dot(h, wu_ref[...])
    acc_scr[...] += _dot((_silu(g) * u).astype(BF16), wd_ref[...])

    @pl.when(f == pl.num_programs(1) - 1)
    def _():
        m = m_ref[0]
        y = x_ref[...] + 0.5 * m[3 * sub + 2:3 * sub + 3] * acc_scr[...]
        if final:
            y = y * lax.rsqrt(jnp.mean(y * y, axis=-1, keepdims=True) + EPS) * fin_ref[...]
        o_ref[...] = y


def _ffn(x2, mods, norm_w, w_up, w_down, sub, tokens_per_mod, final_w=None):
    t, d = x2.shape
    f = w_down.shape[0]
    tm, tf = 512, 512
    nf = f // tf
    tiles_per_mod = tokens_per_mod // tm
    final = final_w is not None
    in_specs = [pl.BlockSpec((tm, d), lambda i, j: (i, 0)),
                pl.BlockSpec((1, N_MOD, d), lambda i, j: (i // tiles_per_mod, 0, 0)),
                pl.BlockSpec((1, d), lambda i, j: (0, 0)),
                pl.BlockSpec((d, tf), lambda i, j: (0, j)),
                pl.BlockSpec((d, tf), lambda i, j: (0, nf + j)),
                pl.BlockSpec((tf, d), lambda i, j: (j, 0))]
    args = [x2, mods, norm_w.reshape(1, d), w_up, w_up, w_down]
    if final:
        in_specs.append(pl.BlockSpec((1, d), lambda i, j: (0, 0)))
        args.append(final_w.reshape(1, d))
    return pl.pallas_call(
        functools.partial(_ffn_kernel, sub=sub, final=final),
        grid=(t // tm, nf),
        in_specs=in_specs,
        out_specs=pl.BlockSpec((tm, d), lambda i, j: (i, 0)),
        out_shape=jax.ShapeDtypeStruct((t, d), F32),
        scratch_shapes=[pltpu.VMEM((tm, d), BF16), pltpu.VMEM((tm, d), F32)],
        compiler_params=_params(("parallel", "arbitrary"), 48),
        name="ffn_final" if final else "ffn",
    )(*args)


def _proj_kernel(x_ref, m_ref, nw_ref, w_ref, o_ref, h_scr, *, sub):
    @pl.when(pl.program_id(1) == 0)
    def _():
        m = m_ref[0]
        h = _modnorm(x_ref[...], nw_ref[...], m[3 * sub + 1:3 * sub + 2], m[3 * sub:3 * sub + 1])
        h_scr[...] = h.astype(BF16)

    o_ref[...] = _dot(h_scr[...], w_ref[...])


def _proj(x2, mods, norm_w, w, sub, tokens_per_mod):
    t, d = x2.shape
    n = w.shape[1]
    tm, tn = 512, 1024
    tiles_per_mod = tokens_per_mod // tm
    return pl.pallas_call(
        functools.partial(_proj_kernel, sub=sub),
        grid=(t // tm, n // tn),
        in_specs=[pl.BlockSpec((tm, d), lambda i, j: (i, 0)),
                  pl.BlockSpec((1, N_MOD, d), lambda i, j: (i // tiles_per_mod, 0, 0)),
                  pl.BlockSpec((1, d), lambda i, j: (0, 0)),
                  pl.BlockSpec((d, tn), lambda i, j: (0, j))],
        out_specs=pl.BlockSpec((tm, tn), lambda i, j: (i, j)),
        out_shape=jax.ShapeDtypeStruct((t, n), F32),
        scratch_shapes=[pltpu.VMEM((tm, d), BF16)],
        compiler_params=_params(("parallel", "arbitrary"), 40),
        name="in_proj",
    )(x2, mods, norm_w.reshape(1, d), w)


def _conv_kernel(p_ref, w_ref, o_ref, xp_scr, *, rows, width, strip, n_q_tiles, n_qk_tiles):
    length = rows * width
    pad = (xp_scr.shape[0] - length) // 2
    tc = p_ref.shape[2]
    j = pl.program_id(1)
    xp_scr[0:pad, :] = jnp.zeros((pad, tc), F32)
    xp_scr[pad + length:pad + length + pad, :] = jnp.zeros((pad, tc), F32)
    xp_scr[pad:pad + length, :] = p_ref[0]
    w = w_ref[...]
    q_scale = jnp.where(j < n_q_tiles, HEAD_DIM ** -0.5, 1.0).astype(F32)
    is_qk = j < n_qk_tiles
    row_taps = (-1, 0, 1) if rows > 1 else (0,)
    for s in range(length // strip):
        t0 = s * strip
        col = (lax.broadcasted_iota(jnp.int32, (strip, 1), 0) + t0) & (width - 1)
        acc = jnp.zeros((strip, tc), F32)
        for dr in row_taps:
            for dc in (-1, 0, 1):
                off = pad + t0 + dr * width + dc
                xs = xp_scr[off:off + strip, :]
                if dc == -1:
                    xs = jnp.where(col >= 1, xs, 0.0)
                elif dc == 1:
                    xs = jnp.where(col <= width - 2, xs, 0.0)
                tap = (dr + 1) * 3 + (dc + 1)
                acc = acc + xs * w[tap:tap + 1, :]
        y = _silu(acc)
        for hh in range(tc // HEAD_DIM):
            yh = y[:, hh * HEAD_DIM:(hh + 1) * HEAD_DIM]
            inv = lax.rsqrt(jnp.sum(yh * yh, axis=-1, keepdims=True) + EPS) * q_scale
            o_ref[0, t0:t0 + strip, hh * HEAD_DIM:(hh + 1) * HEAD_DIM] = yh * jnp.where(is_qk, inv, 1.0)


def _conv(p3, conv_w9, rows, width, n_conv, n_qk):
    b, length, _ = p3.shape
    tc = 512
    pad = width + 8 if rows > 1 else 8
    strip = min(256, length)
    kern = functools.partial(_conv_kernel, rows=rows, width=width, strip=strip,
                             n_q_tiles=(n_qk // 2) // tc, n_qk_tiles=n_qk // tc)
    return pl.pallas_call(
        kern,
        grid=(b, n_conv // tc),
        in_specs=[pl.BlockSpec((1, length, tc), lambda i, j: (i, 0, j)),
                  pl.BlockSpec((9, tc), lambda i, j: (0, j))],
        out_specs=pl.BlockSpec((1, length, tc), lambda i, j: (i, 0, j)),
        out_shape=jax.ShapeDtypeStruct((b, length, n_conv), F32),
        scratch_shapes=[pltpu.VMEM((length + 2 * pad, tc), F32)],
        compiler_params=_params(("parallel", "parallel"), 40),
        name="grid_conv",
    )(p3, conv_w9)


def _unit_tri_inverse(a_list, strict_list, row, col):
    n = 1
    e = None
    while n < CHUNK:
        k = n.bit_length() - 1
        pair = ((row >> (k + 1)) == (col >> (k + 1))) & ((row >> k) != (col >> k))
        l_n = [jnp.where(pair & st, a, 0.0) for a, st in zip(a_list, strict_list)]
        if e is None:
            e = [-l for l in l_n]
        else:
            e16 = [x.astype(BF16) for x in e]
            y = [l + _dot(l.astype(BF16), x16) for l, x16 in zip(l_n, e16)]
            e = [x - yy - _dot(x16, yy.astype(BF16)) for x, x16, yy in zip(e, e16, y)]
        n *= 2
    return e


def _delta_kernel(*refs, n_heads, with_init, with_output):
    chunk_refs = (refs[0:4], refs[4:8])
    alog_ref, dtb_ref = refs[8:10]
    rest = refs[10:]
    if with_init:
        s0_ref, rest = rest[0], rest[1:]
    if with_output:
        o_refs, s_scr = rest[0:2], rest[2]
    else:
        sfin_ref, s_scr = rest
    c = pl.program_id(1)

    @pl.when(c == 0)
    def _():
        if with_init:
            s_scr[...] = s0_ref[0]
        else:
            s_scr[...] = jnp.zeros_like(s_scr)

    row = lax.broadcasted_iota(jnp.int32, (CHUNK, CHUNK), 0)
    col = lax.broadcasted_iota(jnp.int32, (CHUNK, CHUNK), 1)
    incl_d = (row >= col, row <= col)
    strict_d = (row > col, row < col)

    chains = [(d, h) for d in range(2) for h in range(n_heads)]
    gc, gt, bt, decay, kh, vh, qh, strict = [], [], [], [], [], [], [], []
    for d in range(2):
        q_ref, k_ref, v_ref, gb_ref = chunk_refs[d]
        gb = gb_ref[0]
        x = gb + dtb_ref[d]
        softplus = jnp.maximum(x, 0.0) + jnp.log1p(jnp.exp(-jnp.abs(x)))
        g_all = -jnp.exp(alog_ref[d]) * softplus
        beta_all = _sigmoid(gb)
        gcum = jnp.dot(incl_d[d].astype(F32), g_all, precision=lax.Precision.HIGHEST,
                       preferred_element_type=F32)
        gtot = jnp.sum(g_all, axis=0, keepdims=True)
        gcum_t = jnp.concatenate([gcum, jnp.zeros_like(gcum)], axis=0).T
        for h in range(n_heads):
            sl = slice(h * HEAD_DIM, (h + 1) * HEAD_DIM)
            gc.append(gcum[:, h:h + 1])
            gt.append(gtot[:, h:h + 1])
            bt.append(beta_all[:, n_heads + h:n_heads + h + 1])
            decay.append(jnp.where(incl_d[d], jnp.exp(jnp.where(incl_d[d], gc[-1] - gcum_t[h:h + 1, 0:CHUNK], 0.0)), 0.0))
            kh.append(k_ref[0, :, sl])
            vh.append(v_ref[0, :, sl])
            if with_output:
                qh.append(q_ref[0, :, sl])
            strict.append(strict_d[d])
    n = range(len(chains))
    k16 = [x.astype(BF16) for x in kh]
    if with_output:
        kq = [_dot_nt(jnp.concatenate([k16[i], qh[i].astype(BF16)], axis=0), k16[i]) for i in n]
        kk = [x[:CHUNK] for x in kq]
        qk = [x[CHUNK:] for x in kq]
    else:
        kk = [_dot_nt(x, x) for x in k16]
    a_mat = [jnp.where(strict[i], bt[i] * decay[i] * kk[i], 0.0) for i in n]
    e = _unit_tri_inverse(a_mat, strict, row, col)
    eg = [jnp.exp(x) for x in gc]
    rhs = [jnp.concatenate([(bt[i] * eg[i]) * kh[i], bt[i] * vh[i]], axis=1) for i in n]
    sol = [rhs[i] + _dot(e[i].astype(BF16), rhs[i].astype(BF16)) for i in n]
    s_h = [s_scr[d, h] for d, h in chains]
    s16 = [x.astype(BF16) for x in s_h]
    if with_output:
        wq = [_dot(jnp.concatenate([sol[i][:, :HEAD_DIM].astype(BF16), (qh[i] * eg[i]).astype(BF16)], axis=0), s16[i])
              for i in n]
        ws = [x[:CHUNK] for x in wq]
        qs = [x[CHUNK:] for x in wq]
    else:
        ws = [_dot(sol[i][:, :HEAD_DIM].astype(BF16), s16[i]) for i in n]
    u16 = [(sol[i][:, HEAD_DIM:] - ws[i]).astype(BF16) for i in n]
    if with_output:
        for i, (d, h) in enumerate(chains):
            o_refs[d][0, :, h * HEAD_DIM:(h + 1) * HEAD_DIM] = qs[i] + _dot((qk[i] * decay[i]).astype(BF16), u16[i])
    for i, (d, h) in enumerate(chains):
        k_dec = kh[i] * jnp.exp(gt[i] - gc[i])
        s_scr[d, h] = jnp.exp(gt[i]) * s_h[i] + _dot_tn(k_dec.astype(BF16), u16[i])

    if not with_output:
        @pl.when(c == pl.num_programs(1) - 1)
        def _():
            sfin_ref[0] = s_scr[...]


def _delta(qkv, p3, gate_block0, alog_r, dtb_r, n_heads, s0=None):
    b, length, _ = qkv.shape
    nch = length // CHUNK
    hd = n_heads * HEAD_DIM
    with_init = s0 is not None
    with_output = with_init

    in_specs, args = [], []
    for d in range(2):
        pos = (lambda c: c) if d == 0 else (lambda c: nch - 1 - c)
        in_specs += [pl.BlockSpec((1, CHUNK, hd), lambda i, c, pos=pos: (i, pos(c), 0)),
                     pl.BlockSpec((1, CHUNK, hd), lambda i, c, pos=pos: (i, pos(c), 1)),
                     pl.BlockSpec((1, CHUNK, hd), lambda i, c, pos=pos: (i, pos(c), 2)),
                     pl.BlockSpec((1, CHUNK, LANES), lambda i, c, pos=pos, d=d: (i, pos(c), gate_block0 + d))]
        args += [qkv, qkv, qkv, p3]
    in_specs += [pl.BlockSpec((2, 1, LANES), lambda i, c: (0, 0, 0))] * 2
    args += [alog_r, dtb_r]
    state_spec = pl.BlockSpec((1, 2, n_heads, HEAD_DIM, HEAD_DIM), lambda i, c: (i, 0, 0, 0, 0))
    if with_init:
        in_specs.append(state_spec)
        args.append(s0)
    if with_output:
        out_specs = [pl.BlockSpec((1, CHUNK, hd), lambda i, c: (i, c, 0)),
                     pl.BlockSpec((1, CHUNK, hd), lambda i, c: (i, nch - 1 - c, 0))]
        out_shape = [jax.ShapeDtypeStruct((b, length, hd), F32)] * 2
    else:
        out_specs = state_spec
        out_shape = jax.ShapeDtypeStruct((b, 2, n_heads, HEAD_DIM, HEAD_DIM), F32)
    return pl.pallas_call(
        functools.partial(_delta_kernel, n_heads=n_heads, with_init=with_init, with_output=with_output),
        grid=(b, nch),
        in_specs=in_specs,
        out_specs=out_specs,
        out_shape=out_shape,
        scratch_shapes=[pltpu.VMEM((2, n_heads, HEAD_DIM, HEAD_DIM), F32)],
        compiler_params=_params(("parallel", "arbitrary"), 40),
        name="delta_lat" if with_output else "delta_ctx",
    )(*args)


def _cmul(ar, ai, br, bi):
    return ar * br - ai * bi, ar * bi + ai * br


def _s5_operators(a_re, a_im, log_dt, b_re, b_im, c_re, c_im):
    g, p, s = b_re.shape
    dt = jnp.exp(log_dt.astype(F32))[..., None]
    lr, li = a_re.astype(F32) * dt, a_im.astype(F32) * dt

    def powers(n):
        n = n.astype(F32)
        shape = lr.shape + (1,) * n.ndim
        mag = jnp.exp(lr.reshape(shape) * n)
        ang = li.reshape(shape) * n
        return mag * jnp.cos(ang), mag * jnp.sin(ang)

    abr, abi = powers(jnp.ones((), F32))
    nr, ni = abr - 1.0, abi
    cr, ci = a_re.astype(F32), a_im.astype(F32)
    den = cr * cr + ci * ci
    fr, fi = (nr * cr + ni * ci) / den, (ni * cr - nr * ci) / den
    bbr, bbi = _cmul(fr[..., None], fi[..., None], b_re.astype(F32)[None], b_im.astype(F32)[None])
    ccr = jnp.swapaxes(c_re.astype(F32), 1, 2)
    cci = jnp.swapaxes(c_im.astype(F32), 1, 2)

    nsub = CHUNK // S5_SUB
    dl = jnp.arange(nsub)[:, None]
    jj = jnp.arange(S5_SUB)[None, :]
    ef = powers(S5_SUB * dl - jj)
    eb = powers(S5_SUB * dl + jj)
    yf = powers(jnp.arange(S5_SUB))
    yb = powers(-jnp.arange(S5_SUB))

    def xz_table(e, d):
        er, ei = e[0][d], e[1][d]
        xr, xi = _cmul(bbr[d][:, :, None, None, :], bbi[d][:, :, None, None, :], er[..., None], ei[..., None])
        xr = jnp.transpose(xr, (0, 2, 3, 4, 1)).reshape(g, nsub, S5_SUB * s, p)
        xi = jnp.transpose(xi, (0, 2, 3, 4, 1)).reshape(g, nsub, S5_SUB * s, p)
        return jnp.concatenate([xr, xi], axis=-1)

    def yq_table(y, d):
        yr, yi = _cmul(ccr[:, :, None, :], cci[:, :, None, :], y[0][d][..., None], y[1][d][..., None])
        yr = yr.reshape(g, p, S5_SUB * s)
        yi = yi.reshape(g, p, S5_SUB * s)
        return jnp.concatenate([yr, -yi], axis=1)

    xzf, yqf = xz_table(ef, 0), yq_table(yf, 0)
    xzb, yqb = xz_table(eb, 1), yq_table(yb, 1)

    tok = jnp.arange(CHUNK)

    def state_in(e, d):
        er, ei = e[0][d], e[1][d]
        xr, xi = _cmul(bbr[d][:, :, None, :], bbi[d][:, :, None, :], er[..., None], ei[..., None])
        xr = jnp.transpose(xr, (0, 2, 3, 1)).reshape(g, CHUNK * s, p)
        xi = jnp.transpose(xi, (0, 2, 3, 1)).reshape(g, CHUNK * s, p)
        return xr, xi

    def state_out(e, d):
        er, ei = e[0][d], e[1][d]
        yr, yi = _cmul(ccr[:, :, None, :], cci[:, :, None, :], er[..., None], ei[..., None])
        return yr.reshape(g, p, CHUNK * s), -yi.reshape(g, p, CHUNK * s)

    bst = jnp.concatenate(state_in(powers(CHUNK - 1 - tok), 0) + state_in(powers(tok), 1), axis=-1)
    cst = jnp.concatenate(state_out(powers(tok + 1), 0) + state_out(powers(CHUNK - tok), 1), axis=1)
    a64r, a64i = powers(jnp.full((), CHUNK, F32))
    rot = jnp.stack([jnp.concatenate([a64r[0], a64r[0]], -1), jnp.concatenate([-a64i[0], a64i[0]], -1),
                     jnp.concatenate([a64r[1], a64r[1]], -1), jnp.concatenate([-a64i[1], a64i[1]], -1)], axis=1)
    return xzf, yqf, xzb, yqb, bst.astype(BF16), cst.astype(BF16), rot


def _s5_kernel(uc_ref, ul_ref, xzf_ref, yqf_ref, xzb_ref, yqb_ref, dsk_ref, bst_ref, cst_ref, rot_ref,
               y_ref, m_scr, hin_scr, *, nb):
    nsub = CHUNK // S5_SUB
    bst = bst_ref[0]
    rot = rot_ref[0]
    fa, fb, ba, bb = rot[0:1], rot[1:2], rot[2:3], rot[3:4]

    def step(h, mul_a, mul_b, inp):
        return mul_a * h + mul_b * pltpu.roll(h, LANES // 2, axis=1) + inp

    hc = _dot(uc_ref[0], bst)
    n_ctx = uc_ref.shape[1] // nb
    hf = jnp.zeros((nb, LANES), F32)
    hb = jnp.zeros((nb, LANES), F32)
    for c in range(n_ctx):
        hf = step(hf, fa, fb, hc[c * nb:(c + 1) * nb, 0:LANES])
    for c in reversed(range(n_ctx)):
        hb = step(hb, ba, bb, hc[c * nb:(c + 1) * nb, LANES:2 * LANES])

    ul = ul_ref[0]
    hl = _dot(ul, bst)
    n_lat = ul_ref.shape[1] // nb
    for c in range(n_lat):
        hin_scr[c * nb:(c + 1) * nb, 0:LANES] = hf
        hf = step(hf, fa, fb, hl[c * nb:(c + 1) * nb, 0:LANES])
    for c in reversed(range(n_lat)):
        hin_scr[c * nb:(c + 1) * nb, LANES:2 * LANES] = hb
        hb = step(hb, ba, bb, hl[c * nb:(c + 1) * nb, LANES:2 * LANES])

    row = lax.broadcasted_iota(jnp.int32, (LANES, LANES), 0)
    col = lax.broadcasted_iota(jnp.int32, (LANES, LANES), 1)
    yqf = yqf_ref[0]
    yqb = yqb_ref[0]

    def tap(xz_ref, yq, delta):
        return jnp.dot(xz_ref[0, delta], yq, precision=lax.Precision.HIGHEST, preferred_element_type=F32)

    jj, ii = row >> 4, col >> 4
    diag = (jnp.where(ii >= jj, tap(xzf_ref, yqf, 0), 0.0) + jnp.where(jj >= ii, tap(xzb_ref, yqb, 0), 0.0)
            + jnp.where(row == col, dsk_ref[0], 0.0))
    taps_f = [None] + [tap(xzf_ref, yqf, dl).astype(BF16) for dl in range(1, nsub)]
    taps_b = [None] + [tap(xzb_ref, yqb, dl).astype(BF16) for dl in range(1, nsub)]
    diag = diag.astype(BF16)
    for bj in range(nsub):
        for bi in range(nsub):
            blk = diag if bi == bj else (taps_f[bi - bj] if bi > bj else taps_b[bj - bi])
            m_scr[bj * LANES:(bj + 1) * LANES, bi * LANES:(bi + 1) * LANES] = blk

    y_ref[0] = _dot(ul, m_scr[...]) + _dot(hin_scr[...].astype(BF16), cst_ref[0])


def _s5(uc, ul, ops, d_skip, nb):
    xzf, yqf, xzb, yqb, bst, cst, rot = ops
    g, rows_l, kdim = ul.shape
    rows_c = uc.shape[1]
    nsub = CHUNK // S5_SUB
    p2 = xzf.shape[-1]
    dsk = jnp.tile(d_skip.astype(F32).reshape(g, 1, S5_GROUP), (1, 1, S5_SUB))

    def spec(shape):
        nd = len(shape)
        return pl.BlockSpec((1,) + tuple(shape[1:]), lambda i: (i,) + (0,) * (nd - 1))

    arrays = [uc, ul, xzf, yqf, xzb, yqb, dsk, bst, cst, rot]
    return pl.pallas_call(
        functools.partial(_s5_kernel, nb=nb),
        grid=(g,),
        in_specs=[spec(a.shape) for a in arrays],
        out_specs=pl.BlockSpec((1, rows_l, kdim), lambda i: (i, 0, 0)),
        out_shape=jax.ShapeDtypeStruct((g, rows_l, kdim), F32),
        scratch_shapes=[pltpu.VMEM((kdim, kdim), BF16), pltpu.VMEM((rows_l, 4 * (p2 // 2)), F32)],
        compiler_params=_params(("parallel",), 32),
        name="s5_scan",
    )(*arrays)


def _gelu_tanh(x):
    return 0.5 * x * (1.0 + jnp.tanh(0.7978845608028654 * (x + 0.044715 * (x * x * x))))


def _out_kernel(of_ref, ob_ref, z_ref, y_ref, x_ref, m_ref, dnw_ref, wglu_ref, wdn_ref, ws5_ref, out_ref, *, n_heads):
    o = of_ref[...] + ob_ref[...]
    z = z_ref[...]
    dnw = dnw_ref[...]
    parts = []
    for h in range(n_heads):
        sl = slice(h * HEAD_DIM, (h + 1) * HEAD_DIM)
        oh = o[:, sl]
        yh = oh * lax.rsqrt(jnp.mean(oh * oh, axis=-1, keepdims=True) + EPS) * dnw
        parts.append((yh * _silu(z[:, sl])).astype(BF16))
    dn = jnp.concatenate(parts, axis=1)
    t = _dot(_gelu_tanh(y_ref[...]).astype(BF16), wglu_ref[...])
    half = t.shape[1] // 2
    s5 = (t[:, :half] * _sigmoid(t[:, half:])).astype(BF16)
    acc = _dot(dn, wdn_ref[...]) + _dot(s5, ws5_ref[...])
    out_ref[...] = x_ref[...] + m_ref[0][5:6] * acc


def _out_proj(o_fwd, o_bwd, p2, z_block, y2, x2, mods, dn_norm, w_glu, w_dn, w_s5, n_heads, tokens_per_mod):
    t, d = x2.shape
    hd = n_heads * HEAD_DIM
    s5w = y2.shape[1]
    tm = 256
    tiles_per_mod = tokens_per_mod // tm
    return pl.pallas_call(
        functools.partial(_out_kernel, n_heads=n_heads),
        grid=(t // tm,),
        in_specs=[pl.BlockSpec((tm, hd), lambda i: (i, 0)),
                  pl.BlockSpec((tm, hd), lambda i: (i, 0)),
                  pl.BlockSpec((tm, hd), lambda i: (i, z_block)),
                  pl.BlockSpec((tm, s5w), lambda i: (i, 0)),
                  pl.BlockSpec((tm, d), lambda i: (i, 0)),
                  pl.BlockSpec((1, N_MOD, d), lambda i: (i // tiles_per_mod, 0, 0)),
                  pl.BlockSpec((1, HEAD_DIM), lambda i: (0, 0)),
                  pl.BlockSpec(w_glu.shape, lambda i: (0, 0)),
                  pl.BlockSpec(w_dn.shape, lambda i: (0, 0)),
                  pl.BlockSpec(w_s5.shape, lambda i: (0, 0))],
        out_specs=pl.BlockSpec((tm, d), lambda i: (i, 0)),
        out_shape=jax.ShapeDtypeStruct((t, d), F32),
        compiler_params=_params(("parallel",), 48),
        name="out_proj",
    )(o_fwd, o_bwd, p2, y2, x2, mods, dn_norm.reshape(1, HEAD_DIM), w_glu, w_dn, w_s5)


def _layer(x, ctx, m_lat, m_ctx, norm_ffn1, ffn1_up, ffn1_down, norm_mix, w_in, dn_conv, dn_a_log,
           dn_dt_bias, dn_norm, s5_a_re, s5_a_im, s5_log_dt, s5_b_re, s5_b_im, s5_c_re, s5_c_im, s5_d,
           s5_glu, w_out, norm_ffn2, ffn2_up, ffn2_down, final_norm):
    b, length, d = x.shape
    lc = ctx.shape[1]
    n_heads = dn_a_log.shape[1]
    hd = n_heads * HEAD_DIM
    n_conv = dn_conv.shape[-1]
    n_qk = n_conv - hd
    s5w = s5_d.shape[0]
    groups = s5w // S5_GROUP
    rows = length // GRID_W

    gate0 = n_conv + hd
    w_gates = w_in[:, gate0:gate0 + 4 * n_heads].reshape(d, 4, n_heads)
    zpad = jnp.zeros((d, LANES - 2 * n_heads), w_in.dtype)
    w_gate_dirs = [jnp.concatenate([w_gates[:, dr], w_gates[:, 2 + dr], zpad], axis=1) for dr in range(2)]
    n_used = gate0 + s5w + 2 * LANES
    n_pad = -n_used % 1024
    w_in_r = jnp.concatenate([w_in[:, :gate0], w_in[:, gate0 + 4 * n_heads:]] + w_gate_dirs
                             + [jnp.zeros((d, n_pad), w_in.dtype)], axis=1).astype(BF16)
    n_proj = w_in_r.shape[1]
    z_block = n_conv // hd
    s5_col0 = gate0
    gate_block0 = (gate0 + s5w) // LANES
    lane_pad = jnp.zeros((2, LANES - n_heads), F32)
    alog_r = jnp.concatenate([dn_a_log.astype(F32), lane_pad], axis=1).reshape(2, 1, LANES)
    dtb_r = jnp.concatenate([dn_dt_bias.astype(F32), lane_pad], axis=1).reshape(2, 1, LANES)
    conv_w9 = dn_conv.reshape(9, n_conv)

    x2 = x.reshape(b * length, d)
    c2 = ctx.reshape(b * lc, d)

    up1, down1 = ffn1_up.astype(BF16), ffn1_down.astype(BF16)
    x2 = _ffn(x2, m_lat, norm_ffn1, up1, down1, 0, length)
    c2 = _ffn(c2, m_ctx, norm_ffn1, up1, down1, 0, b * lc)

    p_lat = _proj(x2, m_lat, norm_mix, w_in_r, 1, length)
    p_ctx = _proj(c2, m_ctx, norm_mix, w_in_r, 1, b * lc)
    p_lat3 = p_lat.reshape(b, length, n_proj)
    p_ctx3 = p_ctx.reshape(b, lc, n_proj)

    qkv_lat = _conv(p_lat3, conv_w9, rows, GRID_W, n_conv, n_qk)
    qkv_ctx = _conv(p_ctx3, conv_w9, 1, lc, n_conv, n_qk)
    s_ctx = _delta(qkv_ctx, p_ctx3, gate_block0, alog_r, dtb_r, n_heads)
    o_fwd, o_bwd = _delta(qkv_lat, p_lat3, gate_block0, alog_r, dtb_r, n_heads, s0=s_ctx)

    ops = _s5_operators(s5_a_re, s5_a_im, s5_log_dt, s5_b_re, s5_b_im, s5_c_re, s5_c_im)
    cols_per_chunk = CHUNK // rows
    u_lat = p_lat3[:, :, s5_col0:s5_col0 + s5w].astype(BF16)
    u_lat = u_lat.reshape(b, rows, GRID_W // cols_per_chunk, cols_per_chunk, groups, S5_GROUP)
    u_lat = jnp.transpose(u_lat, (4, 2, 0, 3, 1, 5)).reshape(groups, (length // CHUNK) * b, CHUNK * S5_GROUP)
    u_ctx = p_ctx3[:, :, s5_col0:s5_col0 + s5w].astype(BF16)
    u_ctx = u_ctx.reshape(b, lc // CHUNK, CHUNK, groups, S5_GROUP)
    u_ctx = jnp.transpose(u_ctx, (3, 1, 0, 2, 4)).reshape(groups, (lc // CHUNK) * b, CHUNK * S5_GROUP)
    y = _s5(u_ctx, u_lat, ops, s5_d, b)
    y = y.reshape(groups, GRID_W // cols_per_chunk, b, cols_per_chunk, rows, S5_GROUP)
    y2 = jnp.transpose(y, (2, 4, 1, 3, 0, 5)).reshape(b * length, s5w)

    w_out16 = w_out.astype(BF16)
    x2 = _out_proj(o_fwd.reshape(b * length, hd), o_bwd.reshape(b * length, hd), p_lat, z_block, y2, x2, m_lat, dn_norm,
                   s5_glu.astype(BF16), w_out16[:hd], w_out16[hd:], n_heads, length)

    x2 = _ffn(x2, m_lat, norm_ffn2, ffn2_up.astype(BF16), ffn2_down.astype(BF16), 2, length,
              final_w=final_norm)
    return x2.reshape(b, length, d)


def kernel(x, c, ctx, c_ctx, w_mod, b_mod, norm_ffn1, ffn1_up, ffn1_down, norm_mix, w_in, dn_conv, dn_a_log, dn_dt_bias, dn_norm, s5_a_re, s5_a_im, s5_log_dt, s5_b_re, s5_b_im, s5_c_re, s5_c_im, s5_d, s5_glu, w_out, norm_ffn2, ffn2_up, ffn2_down, final_norm):
    depth = w_mod.shape[0]
    assert depth == 1, "the context stream update of deeper stacks is not implemented"
    b, _, d = x.shape
    cond = jnp.concatenate([c, c_ctx[None], jnp.zeros((16 - b - 1, d), c.dtype)], axis=0)
    m = _ada(cond, w_mod[0], b_mod[0]).reshape(16, N_MOD, d)
    return _layer(x, ctx, m[:b], m[b:b + 1], norm_ffn1[0], ffn1_up[0], ffn1_down[0], norm_mix[0], w_in[0],
                  dn_conv[0], dn_a_log[0], dn_dt_bias[0], dn_norm[0], s5_a_re[0], s5_a_im[0], s5_log_dt[0],
                  s5_b_re[0], s5_b_im[0], s5_c_re[0], s5_c_im[0], s5_d[0], s5_glu[0], w_out[0],
                  norm_ffn2[0], ffn2_up[0], ffn2_down[0], final_norm)
```

```python
import functools

import jax
import jax.numpy as jnp
from jax import lax
from jax.experimental import pallas as pl
from jax.experimental.pallas import tpu as pltpu

F32 = jnp.float32
BF16 = jnp.bfloat16
EPS = 1e-6
N_MOD = 9
GRID_W = 64
CHUNK = 64
HEAD_DIM = 128
S5_GROUP = 16
S5_SUB = 8
LANES = 128
MIB = 1024 * 1024


def _params(semantics, vmem_mib):
    return pltpu.CompilerParams(dimension_semantics=semantics, vmem_limit_bytes=vmem_mib * MIB)


def _sigmoid(x):
    return 1.0 / (1.0 + jnp.exp(-x))


def _silu(x):
    return x * _sigmoid(x)


def _dot(a, b):
    return jnp.dot(a, b, preferred_element_type=F32)


def _dot_nt(a, b):
    return lax.dot_general(a, b, (((1,), (1,)), ((), ())), preferred_element_type=F32)


def _dot_tn(a, b):
    return lax.dot_general(a, b, (((0,), (0,)), ((), ())), preferred_element_type=F32)


def _modnorm(x, norm_w, scale, shift):
    y = x * lax.rsqrt(jnp.mean(x * x, axis=-1, keepdims=True) + EPS) * norm_w
    return y * (1.0 + scale) + shift


def _ada_kernel(c_ref, w_ref, b_ref, o_ref):
    a = _silu(c_ref[...]).astype(BF16)
    o_ref[...] = _dot(a, w_ref[...].astype(BF16)) + b_ref[...]


def _ada(cond, w_mod, b_mod):
    rows, d = cond.shape
    n = w_mod.shape[1]
    tn = 1024
    return pl.pallas_call(
        _ada_kernel,
        grid=(n // tn,),
        in_specs=[pl.BlockSpec((rows, d), lambda j: (0, 0)),
                  pl.BlockSpec((d, tn), lambda j: (0, j)),
                  pl.BlockSpec((1, tn), lambda j: (0, j))],
        out_specs=pl.BlockSpec((rows, tn), lambda j: (0, j)),
        out_shape=jax.ShapeDtypeStruct((rows, n), F32),
        compiler_params=_params(("parallel",), 40),
        name="ada_mod",
    )(cond, w_mod, b_mod.reshape(1, n))


def _ffn_kernel(x_ref, m_ref, nw_ref, wg_ref, wu_ref, wd_ref, *rest, sub, final):
    if final:
        fin_ref, o_ref, h_scr, acc_scr = rest
    else:
        o_ref, h_scr, acc_scr = rest
    f = pl.program_id(1)

    @pl.when(f == 0)
    def _():
        m = m_ref[0]
        h = _modnorm(x_ref[...], nw_ref[...], m[3 * sub + 1:3 * sub + 2], m[3 * sub:3 * sub + 1])
        h_scr[...] = h.astype(BF16)
        acc_scr[...] = jnp.zeros_like(acc_scr)

    h = h_scr[...]
    g = _dot(h, wg_ref[...])
    u = _dot(h, wu_ref[...])
    acc_scr[...] += _dot((_silu(g) * u).astype(BF16), wd_ref[...])

    @pl.when(f == pl.num_programs(1) - 1)
    def _():
        m = m_ref[0]
        y = x_ref[...] + 0.5 * m[3 * sub + 2:3 * sub + 3] * acc_scr[...]
        if final:
            y = y * lax.rsqrt(jnp.mean(y * y, axis=-1, keepdims=True) + EPS) * fin_ref[...]
        o_ref[...] = y


def _ffn(x2, mods, norm_w, w_up, w_down, sub, tokens_per_mod, final_w=None):
    t, d = x2.shape
    f = w_down.shape[0]
    tm, tf = 512, 512
    nf = f // tf
    tiles_per_mod = tokens_per_mod // tm
    final = final_w is not None
    in_specs = [pl.BlockSpec((tm, d), lambda i, j: (i, 0)),
                pl.BlockSpec((1, N_MOD, d), lambda i, j: (i // tiles_per_mod, 0, 0)),
                pl.BlockSpec((1, d), lambda i, j: (0, 0)),
                pl.BlockSpec((d, tf), lambda i, j: (0, j)),
                pl.BlockSpec((d, tf), lambda i, j: (0, nf + j)),
                pl.BlockSpec((tf, d), lambda i, j: (j, 0))]
    args = [x2, mods, norm_w.reshape(1, d), w_up, w_up, w_down]
    if final:
        in_specs.append(pl.BlockSpec((1, d), lambda i, j: (0, 0)))
        args.append(final_w.reshape(1, d))
    return pl.pallas_call(
        functools.partial(_ffn_kernel, sub=sub, final=final),
        grid=(t // tm, nf),
        in_specs=in_specs,
        out_specs=pl.BlockSpec((tm, d), lambda i, j: (i, 0)),
        out_shape=jax.ShapeDtypeStruct((t, d), F32),
        scratch_shapes=[pltpu.VMEM((tm, d), BF16), pltpu.VMEM((tm, d), F32)],
        compiler_params=_params(("parallel", "arbitrary"), 48),
        name="ffn_final" if final else "ffn",
    )(*args)


def _proj_kernel(x_ref, m_ref, nw_ref, w_ref, *rest, sub, s5_layout):
    if s5_layout:
        ws5_ref, o_ref, u0_ref, u1_ref, h_scr, a_scr = rest
    else:
        o_ref, h_scr = rest

    @pl.when(pl.program_id(1) == 0)
    def _():
        m = m_ref[0]
        h = _modnorm(x_ref[...], nw_ref[...], m[3 * sub + 1:3 * sub + 2], m[3 * sub:3 * sub + 1])
        h_scr[...] = h.astype(BF16)
        if s5_layout:
            _s5_pack(_dot(h_scr[...], ws5_ref[...]), a_scr, (u0_ref, u1_ref))

    o_ref[...] = _dot(h_scr[...], w_ref[...])


def _s5_pack(a, a_scr, u_refs):
    tm, width = a.shape
    half = GRID_W // 2
    n_rl = tm // GRID_W
    for k in range(width // LANES):
        a_scr[k] = a[:, k * LANES:(k + 1) * LANES]
    for cc in range(2):
        acc = jnp.concatenate([a_scr[k, pl.ds(cc, tm // 2, stride=2), :] for k in range(width // LANES)], axis=1)
        for g in range(width // S5_GROUP):
            piece = jnp.concatenate([acc[rl * half:(rl + 1) * half, g * S5_GROUP:(g + 1) * S5_GROUP]
                                     for rl in range(n_rl)], axis=1)
            u_refs[cc][g] = piece.astype(BF16)


def _proj(x2, mods, norm_w, w, sub, tokens_per_mod, w_s5=None):
    t, d = x2.shape
    n = w.shape[1]
    tm = 1024
    tn = 1024 if n % 1024 == 0 else 640
    tiles_per_mod = tokens_per_mod // tm
    s5_layout = w_s5 is not None
    in_specs = [pl.BlockSpec((tm, d), lambda i, j: (i, 0)),
                pl.BlockSpec((1, N_MOD, d), lambda i, j: (i // tiles_per_mod, 0, 0)),
                pl.BlockSpec((1, d), lambda i, j: (0, 0)),
                pl.BlockSpec((d, tn), lambda i, j: (0, j))]
    args = [x2, mods, norm_w.reshape(1, d), w]
    out_specs = pl.BlockSpec((tm, tn), lambda i, j: (i, j))
    out_shape = jax.ShapeDtypeStruct((t, n), F32)
    scratch = [pltpu.VMEM((tm, d), BF16)]
    if s5_layout:
        s5w = w_s5.shape[1]
        groups = s5w // S5_GROUP
        half = GRID_W // 2
        lanes_per_tile = (tm // GRID_W) * S5_GROUP
        rows_total = tokens_per_mod // GRID_W
        n_b = t // tokens_per_mod
        in_specs.append(pl.BlockSpec((d, s5w), lambda i, j: (0, 0)))
        args.append(w_s5)
        u_spec = pl.BlockSpec((groups, half, lanes_per_tile), lambda i, j: (0, i // tiles_per_mod, i % tiles_per_mod))
        u_shape = jax.ShapeDtypeStruct((groups, n_b * half, rows_total * S5_GROUP), BF16)
        out_specs = [out_specs, u_spec, u_spec]
        out_shape = [out_shape, u_shape, u_shape]
        scratch.append(pltpu.VMEM((s5w // LANES, tm, LANES), F32))
    return pl.pallas_call(
        functools.partial(_proj_kernel, sub=sub, s5_layout=s5_layout),
        grid=(t // tm, n // tn),
        in_specs=in_specs,
        out_specs=out_specs,
        out_shape=out_shape,
        scratch_shapes=scratch,
        compiler_params=_params(("parallel", "arbitrary"), 52),
        name="in_proj_s5" if s5_layout else "in_proj",
    )(*args)


def _conv_kernel(p_ref, w_ref, o_ref, xp_scr, *, rows, width, strip, n_q_tiles, n_qk_tiles):
    length = rows * width
    pad = (xp_scr.shape[0] - length) // 2
    tc = p_ref.shape[2]
    j = pl.program_id(1)
    xp_scr[0:pad, :] = jnp.zeros((pad, tc), F32)
    xp_scr[pad + length:pad + length + pad, :] = jnp.zeros((pad, tc), F32)
    xp_scr[pad:pad + length, :] = p_ref[0]
    w = w_ref[...]
    q_scale = jnp.where(j < n_q_tiles, HEAD_DIM ** -0.5, 1.0).astype(F32)
    is_qk = j < n_qk_tiles
    row_taps = (-1, 0, 1) if rows > 1 else (0,)
    for s in range(length // strip):
        t0 = s * strip
        col = (lax.broadcasted_iota(jnp.int32, (strip, 1), 0) + t0) & (width - 1)
        acc = jnp.zeros((strip, tc), F32)
        for dr in row_taps:
            for dc in (-1, 0, 1):
                off = pad + t0 + dr * width + dc
                xs = xp_scr[off:off + strip, :]
                if dc == -1:
                    xs = jnp.where(col >= 1, xs, 0.0)
                elif dc == 1:
                    xs = jnp.where(col <= width - 2, xs, 0.0)
                tap = (dr + 1) * 3 + (dc + 1)
                acc = acc + xs * w[tap:tap + 1, :]
        y = _silu(acc)
        for hh in range(tc // HEAD_DIM):
            yh = y[:, hh * HEAD_DIM:(hh + 1) * HEAD_DIM]
            inv = lax.rsqrt(jnp.sum(yh * yh, axis=-1, keepdims=True) + EPS) * q_scale
            o_ref[0, t0:t0 + strip, hh * HEAD_DIM:(hh + 1) * HEAD_DIM] = yh * jnp.where(is_qk, inv, 1.0)


def _conv(p3, conv_w9, rows, width, n_conv, n_qk):
    b, length, _ = p3.shape
    tc = 512
    pad = width + 8 if rows > 1 else 8
    strip = min(256, length)
    kern = functools.partial(_conv_kernel, rows=rows, width=width, strip=strip,
                             n_q_tiles=(n_qk // 2) // tc, n_qk_tiles=n_qk // tc)
    return pl.pallas_call(
        kern,
        grid=(b, n_conv // tc),
        in_specs=[pl.BlockSpec((1, length, tc), lambda i, j: (i, 0, j)),
                  pl.BlockSpec((9, tc), lambda i, j: (0, j))],
        out_specs=pl.BlockSpec((1, length, tc), lambda i, j: (i, 0, j)),
        out_shape=jax.ShapeDtypeStruct((b, length, n_conv), F32),
        scratch_shapes=[pltpu.VMEM((length + 2 * pad, tc), F32)],
        compiler_params=_params(("parallel", "parallel"), 40),
        name="grid_conv",
    )(p3, conv_w9)


def _unit_tri_inverse(a_list, strict_list, row, col):
    n = 1
    e = None
    while n < CHUNK:
        k = n.bit_length() - 1
        pair = ((row >> (k + 1)) == (col >> (k + 1))) & ((row >> k) != (col >> k))
        l_n = [jnp.where(pair & st, a, 0.0) for a, st in zip(a_list, strict_list)]
        if e is None:
            e = [-l for l in l_n]
        else:
            e16 = [x.astype(BF16) for x in e]
            y = [l + _dot(l.astype(BF16), x16) for l, x16 in zip(l_n, e16)]
            e = [x - yy - _dot(x16, yy.astype(BF16)) for x, x16, yy in zip(e, e16, y)]
        n *= 2
    return e


def _delta_kernel(*refs, n_heads, with_init, with_output):
    chunk_refs = (refs[0:4], refs[4:8])
    alog_ref, dtb_ref = refs[8:10]
    rest = refs[10:]
    if with_init:
        s0_ref, rest = rest[0], rest[1:]
    if with_output:
        o_refs, s_scr = rest[0:2], rest[2]
    else:
        sfin_ref, s_scr = rest
    c = pl.program_id(1)

    @pl.when(c == 0)
    def _():
        if with_init:
            s_scr[...] = s0_ref[0]
        else:
            s_scr[...] = jnp.zeros_like(s_scr)

    row = lax.broadcasted_iota(jnp.int32, (CHUNK, CHUNK), 0)
    col = lax.broadcasted_iota(jnp.int32, (CHUNK, CHUNK), 1)
    incl_d = (row >= col, row <= col)
    strict_d = (row > col, row < col)

    chains = [(d, h) for d in range(2) for h in range(n_heads)]
    gc, gt, bt, decay, kh, vh, qh, strict = [], [], [], [], [], [], [], []
    for d in range(2):
        q_ref, k_ref, v_ref, gb_ref = chunk_refs[d]
        gb = gb_ref[0]
        x = gb + dtb_ref[d]
        softplus = jnp.maximum(x, 0.0) + jnp.log1p(jnp.exp(-jnp.abs(x)))
        g_all = -jnp.exp(alog_ref[d]) * softplus
        beta_all = _sigmoid(gb)
        gcum = jnp.dot(incl_d[d].astype(F32), g_all, precision=lax.Precision.HIGHEST,
                       preferred_element_type=F32)
        gtot = jnp.sum(g_all, axis=0, keepdims=True)
        gcum_t = jnp.concatenate([gcum, jnp.zeros_like(gcum)], axis=0).T
        for h in range(n_heads):
            sl = slice(h * HEAD_DIM, (h + 1) * HEAD_DIM)
            gc.append(gcum[:, h:h + 1])
            gt.append(gtot[:, h:h + 1])
            bt.append(beta_all[:, n_heads + h:n_heads + h + 1])
            decay.append(jnp.where(incl_d[d], jnp.exp(jnp.where(incl_d[d], gc[-1] - gcum_t[h:h + 1, 0:CHUNK], 0.0)), 0.0))
            kh.append(k_ref[0, :, sl])
            vh.append(v_ref[0, :, sl])
            if with_output:
                qh.append(q_ref[0, :, sl])
            strict.append(strict_d[d])
    n = range(len(chains))
    k16 = [x.astype(BF16) for x in kh]
    if with_output:
        kq = [_dot_nt(jnp.concatenate([k16[i], qh[i].astype(BF16)], axis=0), k16[i]) for i in n]
        kk = [x[:CHUNK] for x in kq]
        qk = [x[CHUNK:] for x in kq]
    else:
        kk = [_dot_nt(x, x) for x in k16]
    a_mat = [jnp.where(strict[i], bt[i] * decay[i] * kk[i], 0.0) for i in n]
    e = _unit_tri_inverse(a_mat, strict, row, col)
    eg = [jnp.exp(x) for x in gc]
    rhs = [jnp.concatenate([(bt[i] * eg[i]) * kh[i], bt[i] * vh[i]], axis=1) for i in n]
    sol = [rhs[i] + _dot(e[i].astype(BF16), rhs[i].astype(BF16)) for i in n]
    s_h = [s_scr[d, h] for d, h in chains]
    s16 = [x.astype(BF16) for x in s_h]
    if with_output:
        wq = [_dot(jnp.concatenate([sol[i][:, :HEAD_DIM].astype(BF16), (qh[i] * eg[i]).astype(BF16)], axis=0), s16[i])
              for i in n]
        ws = [x[:CHUNK] for x in wq]
        qs = [x[CHUNK:] for x in wq]
    else:
        ws = [_dot(sol[i][:, :HEAD_DIM].astype(BF16), s16[i]) for i in n]
    u16 = [(sol[i][:, HEAD_DIM:] - ws[i]).astype(BF16) for i in n]
    if with_output:
        for i, (d, h) in enumerate(chains):
            o_refs[d][0, :, h * HEAD_DIM:(h + 1) * HEAD_DIM] = qs[i] + _dot((qk[i] * decay[i]).astype(BF16), u16[i])
    for i, (d, h) in enumerate(chains):
        k_dec = kh[i] * jnp.exp(gt[i] - gc[i])
        s_scr[d, h] = jnp.exp(gt[i]) * s_h[i] + _dot_tn(k_dec.astype(BF16), u16[i])

    if not with_output:
        @pl.when(c == pl.num_programs(1) - 1)
        def _():
            sfin_ref[0] = s_scr[...]


def _delta(qkv, p3, gate_block0, alog_r, dtb_r, n_heads, s0=None):
    b, length, _ = qkv.shape
    nch = length // CHUNK
    hd = n_heads * HEAD_DIM
    with_init = s0 is not None
    with_output = with_init

    in_specs, args = [], []
    for d in range(2):
        pos = (lambda c: c) if d == 0 else (lambda c: nch - 1 - c)
        in_specs += [pl.BlockSpec((1, CHUNK, hd), lambda i, c, pos=pos: (i, pos(c), 0)),
                     pl.BlockSpec((1, CHUNK, hd), lambda i, c, pos=pos: (i, pos(c), 1)),
                     pl.BlockSpec((1, CHUNK, hd), lambda i, c, pos=pos: (i, pos(c), 2)),
                     pl.BlockSpec((1, CHUNK, LANES), lambda i, c, pos=pos, d=d: (i, pos(c), gate_block0 + d))]
        args += [qkv, qkv, qkv, p3]
    in_specs += [pl.BlockSpec((2, 1, LANES), lambda i, c: (0, 0, 0))] * 2
    args += [alog_r, dtb_r]
    state_spec = pl.BlockSpec((1, 2, n_heads, HEAD_DIM, HEAD_DIM), lambda i, c: (i, 0, 0, 0, 0))
    if with_init:
        in_specs.append(state_spec)
        args.append(s0)
    if with_output:
        out_specs = [pl.BlockSpec((1, CHUNK, hd), lambda i, c: (i, c, 0)),
                     pl.BlockSpec((1, CHUNK, hd), lambda i, c: (i, nch - 1 - c, 0))]
        out_shape = [jax.ShapeDtypeStruct((b, length, hd), F32)] * 2
    else:
        out_specs = state_spec
        out_shape = jax.ShapeDtypeStruct((b, 2, n_heads, HEAD_DIM, HEAD_DIM), F32)
    return pl.pallas_call(
        functools.partial(_delta_kernel, n_heads=n_heads, with_init=with_init, with_output=with_output),
        grid=(b, nch),
        in_specs=in_specs,
        out_specs=out_specs,
        out_shape=out_shape,
        scratch_shapes=[pltpu.VMEM((2, n_heads, HEAD_DIM, HEAD_DIM), F32)],
        compiler_params=_params(("parallel", "arbitrary"), 40),
        name="delta_lat" if with_output else "delta_ctx",
    )(*args)


def _cmul(ar, ai, br, bi):
    return ar * br - ai * bi, ar * bi + ai * br


def _s5_tables(a_re, a_im, log_dt, b_re, b_im, c_re, c_im):
    g, p, s = b_re.shape
    dt = jnp.exp(log_dt.astype(F32))[..., None]
    lr, li = a_re.astype(F32) * dt, a_im.astype(F32) * dt

    def powers(d, n):
        n = n.astype(F32)
        mag = jnp.exp(lr[d][..., None] * n)
        ang = li[d][..., None] * n
        return mag * jnp.cos(ang), mag * jnp.sin(ang)

    abr = jnp.exp(lr) * jnp.cos(li)
    abi = jnp.exp(lr) * jnp.sin(li)
    nr, ni = abr - 1.0, abi
    cr, ci = a_re.astype(F32), a_im.astype(F32)
    den = cr * cr + ci * ci
    fr, fi = (nr * cr + ni * ci) / den, (ni * cr - nr * ci) / den
    bbr, bbi = _cmul(fr[..., None], fi[..., None], b_re.astype(F32)[None], b_im.astype(F32)[None])
    ccr = jnp.swapaxes(c_re.astype(F32), 1, 2)
    cci = jnp.swapaxes(c_im.astype(F32), 1, 2)
    sub = jnp.arange(S5_SUB)

    def x_table(d, n):
        er, ei = powers(d, n)
        xr, xi = _cmul(bbr[d][:, :, None, :], bbi[d][:, :, None, :], er[..., None], ei[..., None])
        xr = jnp.transpose(xr, (0, 2, 3, 1)).reshape(g, S5_SUB * s, p)
        xi = jnp.transpose(xi, (0, 2, 3, 1)).reshape(g, S5_SUB * s, p)
        return jnp.concatenate([xr, xi], axis=-1)

    def y_table(d, n):
        er, ei = powers(d, n)
        yr, yi = _cmul(ccr[:, :, None, :], cci[:, :, None, :], er[..., None], ei[..., None])
        return jnp.concatenate([yr.reshape(g, p, S5_SUB * s), -yi.reshape(g, p, S5_SUB * s)], axis=1)

    xt = jnp.stack([x_table(0, S5_SUB - 1 - sub), x_table(1, sub)], axis=1)
    yt = jnp.stack([y_table(0, sub + 1), y_table(0, sub - (S5_SUB - 1)), y_table(1, -sub), y_table(1, S5_SUB - sub)], axis=1)
    rows = []
    for d in range(2):
        er, ei = powers(d, S5_SUB * jnp.arange(CHUNK // S5_SUB + 1))
        for k in range(CHUNK // S5_SUB + 1):
            rows += [jnp.concatenate([er[..., k], er[..., k]], -1), jnp.concatenate([-ei[..., k], ei[..., k]], -1)]
    pw = jnp.stack(rows, axis=1)
    return xt, yt, pw


def _s5_kernel(uc_ref, u0_ref, u1_ref, xt_ref, yt_ref, pw_ref, dsk_ref, y_ref, bst_scr, m_scr, h_scr, y_scr,
               *, nb, grid_rows):
    nsub = CHUNK // S5_SUB
    n_pw = 2 * (nsub + 1)
    n_lat = u0_ref.shape[1] // nb
    n_ctx = uc_ref.shape[1] // nb
    row = lax.broadcasted_iota(jnp.int32, (LANES, LANES), 0)
    col = lax.broadcasted_iota(jnp.int32, (LANES, LANES), 1)
    jj, ii = row >> 4, col >> 4

    def hdot(a, b):
        return jnp.dot(a, b, precision=lax.Precision.HIGHEST, preferred_element_type=F32)

    def one_group(gi, carry):
        pw = pw_ref[gi]

        def crot(x, d, k, pw=pw):
            r0 = d * n_pw + 2 * k
            return pw[r0:r0 + 1] * x + pw[r0 + 1:r0 + 2] * pltpu.roll(x, LANES // 2, axis=1)

        xf, xb = xt_ref[gi, 0], xt_ref[gi, 1]
        xfs = [xf] + [crot(xf, 0, k) for k in range(1, nsub)]
        xbs = [xb] + [crot(xb, 1, k) for k in range(1, nsub)]
        yf1, yf0, yb0, yb1 = yt_ref[gi, 0], yt_ref[gi, 1], yt_ref[gi, 2], yt_ref[gi, 3]

        for j in range(nsub):
            bst_scr[j * LANES:(j + 1) * LANES, 0:LANES] = xfs[nsub - 1 - j].astype(BF16)
            bst_scr[j * LANES:(j + 1) * LANES, LANES:2 * LANES] = xbs[j].astype(BF16)
        bst = bst_scr[...]
        ul = jnp.concatenate([u0_ref[gi], u1_ref[gi]], axis=1)
        hc = _dot(uc_ref[gi], bst)
        hl = _dot(ul, bst)
        h_scr[0] = hl[:, 0:LANES]
        h_scr[1] = hl[:, LANES:2 * LANES]

        hf = jnp.zeros((nb, LANES), F32)
        hb = jnp.zeros((nb, LANES), F32)
        for c in range(n_ctx):
            hf = crot(hf, 0, nsub) + hc[c * nb:(c + 1) * nb, 0:LANES]
        for c in reversed(range(n_ctx)):
            hb = crot(hb, 1, nsub) + hc[c * nb:(c + 1) * nb, LANES:2 * LANES]
        for c in range(n_lat):
            rows_c = pl.ds(c, nb, stride=n_lat)
            inp = h_scr[0, rows_c, :]
            h_scr[0, rows_c, :] = hf
            hf = crot(hf, 0, nsub) + inp
        for c in reversed(range(n_lat)):
            rows_c = pl.ds(c, nb, stride=n_lat)
            inp = h_scr[1, rows_c, :]
            h_scr[1, rows_c, :] = hb
            hb = crot(hb, 1, nsub) + inp

        diag = (jnp.where(ii >= jj, hdot(xf, yf0), 0.0) + jnp.where(jj >= ii, hdot(xb, yb0), 0.0)
                + jnp.where(row == col, dsk_ref[gi], 0.0)).astype(BF16)
        taps_f = [None] + [hdot(xfs[dl - 1], yf1).astype(BF16) for dl in range(1, nsub)]
        taps_b = [None] + [hdot(xbs[dl], yb0).astype(BF16) for dl in range(1, nsub)]
        for bj in range(nsub):
            for bi in range(nsub):
                blk = diag if bi == bj else (taps_f[bi - bj] if bi > bj else taps_b[bj - bi])
                m_scr[bj * LANES:(bj + 1) * LANES, bi * LANES:(bi + 1) * LANES] = blk
        y = _dot(ul, m_scr[...])
        hin_f, hin_b = h_scr[0], h_scr[1]
        yf16, yb16 = yf1.astype(BF16), yb1.astype(BF16)
        for bi in range(nsub):
            y_scr[gi, :, bi * LANES:(bi + 1) * LANES] = (
                y[:, bi * LANES:(bi + 1) * LANES]
                + _dot(crot(hin_f, 0, bi).astype(BF16), yf16)
                + _dot(crot(hin_b, 1, nsub - 1 - bi).astype(BF16), yb16))
        return carry

    lax.fori_loop(0, u0_ref.shape[0], one_group, 0)

    half = GRID_W // 2
    tokens = grid_rows * GRID_W
    for cc in range(2):
        for r in range(grid_rows):
            lane0 = (cc * grid_rows + r) * S5_GROUP
            piece = jnp.concatenate([y_scr[gi, :, lane0:lane0 + S5_GROUP] for gi in range(u0_ref.shape[0])], axis=1)
            for b in range(nb):
                y_ref[pl.ds(b * tokens + r * GRID_W + cc, half, stride=2), :] = piece[b * half:(b + 1) * half]


def _s5(uc, u0, u1, tables, d_skip, nb, grid_rows):
    xt, yt, pw = tables
    g = u0.shape[0]
    rows_l = u0.shape[1]
    kdim = 2 * u0.shape[2]
    tokens = grid_rows * GRID_W
    gstep = LANES // S5_GROUP
    dsk = jnp.tile(d_skip.astype(F32).reshape(g, 1, S5_GROUP), (1, 1, S5_SUB))

    def spec(a):
        nd = a.ndim
        return pl.BlockSpec((gstep,) + tuple(a.shape[1:]), lambda i: (i,) + (0,) * (nd - 1))

    arrays = [uc, u0, u1, xt, yt, pw, dsk]
    return pl.pallas_call(
        functools.partial(_s5_kernel, nb=nb, grid_rows=grid_rows),
        grid=(g // gstep,),
        in_specs=[spec(a) for a in arrays],
        out_specs=pl.BlockSpec((nb * tokens, LANES), lambda i: (0, i)),
        out_shape=jax.ShapeDtypeStruct((nb * tokens, g * S5_GROUP), F32),
        scratch_shapes=[pltpu.VMEM((kdim, 2 * LANES), BF16), pltpu.VMEM((kdim, kdim), BF16),
                        pltpu.VMEM((2, rows_l, LANES), F32), pltpu.VMEM((gstep, rows_l, kdim), F32)],
        compiler_params=_params(("parallel",), 48),
        name="s5_scan",
    )(*arrays)


def _gelu_tanh(x):
    return 0.5 * x * (1.0 + jnp.tanh(0.7978845608028654 * (x + 0.044715 * (x * x * x))))


def _out_kernel(of_ref, ob_ref, z_ref, y_ref, x_ref, m_ref, dnw_ref, wglu_ref, wdn_ref, ws5_ref, out_ref, *, n_heads):
    o = of_ref[...] + ob_ref[...]
    z = z_ref[...]
    dnw = dnw_ref[...]
    parts = []
    for h in range(n_heads):
        sl = slice(h * HEAD_DIM, (h + 1) * HEAD_DIM)
        oh = o[:, sl]
        yh = oh * lax.rsqrt(jnp.mean(oh * oh, axis=-1, keepdims=True) + EPS) * dnw
        parts.append((yh * _silu(z[:, sl])).astype(BF16))
    dn = jnp.concatenate(parts, axis=1)
    t = _dot(_gelu_tanh(y_ref[...]).astype(BF16), wglu_ref[...])
    half = t.shape[1] // 2
    s5 = (t[:, :half] * _sigmoid(t[:, half:])).astype(BF16)
    acc = _dot(dn, wdn_ref[...]) + _dot(s5, ws5_ref[...])
    out_ref[...] = x_ref[...] + m_ref[0][5:6] * acc


def _out_proj(o_fwd, o_bwd, p2, z_block, y2, x2, mods, dn_norm, w_glu, w_dn, w_s5, n_heads, tokens_per_mod):
    t, d = x2.shape
    hd = n_heads * HEAD_DIM
    s5w = y2.shape[1]
    tm = 256
    tiles_per_mod = tokens_per_mod // tm
    return pl.pallas_call(
        functools.partial(_out_kernel, n_heads=n_heads),
        grid=(t // tm,),
        in_specs=[pl.BlockSpec((tm, hd), lambda i: (i, 0)),
                  pl.BlockSpec((tm, hd), lambda i: (i, 0)),
                  pl.BlockSpec((tm, hd), lambda i: (i, z_block)),
                  pl.BlockSpec((tm, s5w), lambda i: (i, 0)),
                  pl.BlockSpec((tm, d), lambda i: (i, 0)),
                  pl.BlockSpec((1, N_MOD, d), lambda i: (i // tiles_per_mod, 0, 0)),
                  pl.BlockSpec((1, HEAD_DIM), lambda i: (0, 0)),
                  pl.BlockSpec(w_glu.shape, lambda i: (0, 0)),
                  pl.BlockSpec(w_dn.shape, lambda i: (0, 0)),
                  pl.BlockSpec(w_s5.shape, lambda i: (0, 0))],
        out_specs=pl.BlockSpec((tm, d), lambda i: (i, 0)),
        out_shape=jax.ShapeDtypeStruct((t, d), F32),
        compiler_params=_params(("parallel",), 48),
        name="out_proj",
    )(o_fwd, o_bwd, p2, y2, x2, mods, dn_norm.reshape(1, HEAD_DIM), w_glu, w_dn, w_s5)


def _layer(x, ctx, m_lat, m_ctx, norm_ffn1, ffn1_up, ffn1_down, norm_mix, w_in, dn_conv, dn_a_log,
           dn_dt_bias, dn_norm, s5_a_re, s5_a_im, s5_log_dt, s5_b_re, s5_b_im, s5_c_re, s5_c_im, s5_d,
           s5_glu, w_out, norm_ffn2, ffn2_up, ffn2_down, final_norm):
    b, length, d = x.shape
    lc = ctx.shape[1]
    n_heads = dn_a_log.shape[1]
    hd = n_heads * HEAD_DIM
    n_conv = dn_conv.shape[-1]
    n_qk = n_conv - hd
    s5w = s5_d.shape[0]
    groups = s5w // S5_GROUP
    rows = length // GRID_W

    gate0 = n_conv + hd
    w_gates = w_in[:, gate0:gate0 + 4 * n_heads].reshape(d, 4, n_heads)
    zpad = jnp.zeros((d, LANES - 2 * n_heads), w_in.dtype)
    w_gate_dirs = [jnp.concatenate([w_gates[:, dr], w_gates[:, 2 + dr], zpad], axis=1) for dr in range(2)]
    w_s5 = w_in[:, gate0 + 4 * n_heads:].astype(BF16)
    w_main = jnp.concatenate([w_in[:, :gate0]] + w_gate_dirs, axis=1).astype(BF16)
    n_pad = -(w_main.shape[1] + s5w) % 1024
    w_ctx = jnp.concatenate([w_main, w_s5, jnp.zeros((d, n_pad), BF16)], axis=1)
    z_block = n_conv // hd
    gate_block0 = gate0 // LANES
    s5_col0 = w_main.shape[1]
    lane_pad = jnp.zeros((2, LANES - n_heads), F32)
    alog_r = jnp.concatenate([dn_a_log.astype(F32), lane_pad], axis=1).reshape(2, 1, LANES)
    dtb_r = jnp.concatenate([dn_dt_bias.astype(F32), lane_pad], axis=1).reshape(2, 1, LANES)
    conv_w9 = dn_conv.reshape(9, n_conv)

    x2 = x.reshape(b * length, d)
    c2 = ctx.reshape(b * lc, d)

    up1, down1 = ffn1_up.astype(BF16), ffn1_down.astype(BF16)
    x2 = _ffn(x2, m_lat, norm_ffn1, up1, down1, 0, length)
    c2 = _ffn(c2, m_ctx, norm_ffn1, up1, down1, 0, b * lc)

    p_lat, u_lat0, u_lat1 = _proj(x2, m_lat, norm_mix, w_main, 1, length, w_s5=w_s5)
    p_ctx = _proj(c2, m_ctx, norm_mix, w_ctx, 1, b * lc)
    p_lat3 = p_lat.reshape(b, length, p_lat.shape[1])
    p_ctx3 = p_ctx.reshape(b, lc, p_ctx.shape[1])

    qkv_lat = _conv(p_lat3, conv_w9, rows, GRID_W, n_conv, n_qk)
    qkv_ctx = _conv(p_ctx3, conv_w9, 1, lc, n_conv, n_qk)
    s_ctx = _delta(qkv_ctx, p_ctx3, gate_block0, alog_r, dtb_r, n_heads)
    o_fwd, o_bwd = _delta(qkv_lat, p_lat3, gate_block0, alog_r, dtb_r, n_heads, s0=s_ctx)

    tables = _s5_tables(s5_a_re, s5_a_im, s5_log_dt, s5_b_re, s5_b_im, s5_c_re, s5_c_im)
    u_ctx = p_ctx3[:, :, s5_col0:s5_col0 + s5w].astype(BF16)
    u_ctx = u_ctx.reshape(b, lc // CHUNK, CHUNK, groups, S5_GROUP)
    u_ctx = jnp.transpose(u_ctx, (3, 1, 0, 2, 4)).reshape(groups, (lc // CHUNK) * b, CHUNK * S5_GROUP)
    y2 = _s5(u_ctx, u_lat0, u_lat1, tables, s5_d, b, rows)

    w_out16 = w_out.astype(BF16)
    x2 = _out_proj(o_fwd.reshape(b * length, hd), o_bwd.reshape(b * length, hd), p_lat, z_block, y2, x2, m_lat, dn_norm,
                   s5_glu.astype(BF16), w_out16[:hd], w_out16[hd:], n_heads, length)

    x2 = _ffn(x2, m_lat, norm_ffn2, ffn2_up.astype(BF16), ffn2_down.astype(BF16), 2, length,
              final_w=final_norm)
    return x2.reshape(b, length, d)


def kernel(x, c, ctx, c_ctx, w_mod, b_mod, norm_ffn1, ffn1_up, ffn1_down, norm_mix, w_in, dn_conv, dn_a_log, dn_dt_bias, dn_norm, s5_a_re, s5_a_im, s5_log_dt, s5_b_re, s5_b_im, s5_c_re, s5_c_im, s5_d, s5_glu, w_out, norm_ffn2, ffn2_up, ffn2_down, final_norm):
    depth = w_mod.shape[0]
    assert depth == 1, "the context stream update of deeper stacks is not implemented"
    b, _, d = x.shape
    cond = jnp.concatenate([c, c_ctx[None], jnp.zeros((16 - b - 1, d), c.dtype)], axis=0)
    m = _ada(cond, w_mod[0], b_mod[0]).reshape(16, N_MOD, d)
    return _layer(x, ctx, m[:b], m[b:b + 1], norm_ffn1[0], ffn1_up[0], ffn1_down[0], norm_mix[0], w_in[0],
                  dn_conv[0], dn_a_log[0], dn_dt_bias[0], dn_norm[0], s5_a_re[0], s5_a_im[0], s5_log_dt[0],
                  s5_b_re[0], s5_b_im[0], s5_c_re[0], s5_c_im[0], s5_d[0], s5_glu[0], w_out[0],
                  norm_ffn2[0], ffn2_up[0], ffn2_down[0], final_norm)
```

```python
import functools

import jax
import jax.numpy as jnp
from jax import lax
from jax.experimental import pallas as pl
from jax.experimental.pallas import tpu as pltpu

F32 = jnp.float32
BF16 = jnp.bfloat16
EPS = 1e-6
N_MOD = 9
GRID_W = 64
CHUNK = 64
HEAD_DIM = 128
S5_GROUP = 16
S5_SUB = 8
LANES = 128
MIB = 1024 * 1024


def _params(semantics, vmem_mib):
    return pltpu.CompilerParams(dimension_semantics=semantics, vmem_limit_bytes=vmem_mib * MIB)


def _sigmoid(x):
    return 1.0 / (1.0 + jnp.exp(-x))


def _silu(x):
    return x * _sigmoid(x)


def _dot(a, b):
    return jnp.dot(a, b, preferred_element_type=F32)


def _dot_nt(a, b):
    return lax.dot_general(a, b, (((1,), (1,)), ((), ())), preferred_element_type=F32)


def _dot_tn(a, b):
    return lax.dot_general(a, b, (((0,), (0,)), ((), ())), preferred_element_type=F32)


def _modnorm(x, norm_w, scale, shift):
    y = x * lax.rsqrt(jnp.mean(x * x, axis=-1, keepdims=True) + EPS) * norm_w
    return y * (1.0 + scale) + shift


def _ada_kernel(c_ref, w_ref, b_ref, o_ref):
    a = _silu(c_ref[...]).astype(BF16)
    o_ref[...] = _dot(a, w_ref[...].astype(BF16)) + b_ref[...]


def _ada(cond, w_mod, b_mod):
    rows, d = cond.shape
    n = w_mod.shape[1]
    tn = 1024
    return pl.pallas_call(
        _ada_kernel,
        grid=(n // tn,),
        in_specs=[pl.BlockSpec((rows, d), lambda j: (0, 0)),
                  pl.BlockSpec((d, tn), lambda j: (0, j)),
                  pl.BlockSpec((1, tn), lambda j: (0, j))],
        out_specs=pl.BlockSpec((rows, tn), lambda j: (0, j)),
        out_shape=jax.ShapeDtypeStruct((rows, n), F32),
        compiler_params=_params(("parallel",), 40),
        name="ada_mod",
    )(cond, w_mod, b_mod.reshape(1, n))


def _norm_next_slab(xn_ref, mn_ref, nw_ref, h_scr, nxt, step, sub, nslab):
    slab = xn_ref.shape[0] // nslab
    r0 = pl.multiple_of(jnp.minimum(step, nslab - 1) * slab, slab)
    m = mn_ref[0]
    h = _modnorm(xn_ref[pl.ds(r0, slab), :], nw_ref[...], m[3 * sub + 1:3 * sub + 2], m[3 * sub:3 * sub + 1])
    h_scr[nxt, pl.ds(r0, slab), :] = h.astype(BF16)


def _ffn_kernel(x_ref, xn_ref, m_ref, mn_ref, nw_ref, wg_ref, wu_ref, wd_ref, *rest, sub, final, nslab):
    if final:
        fin_ref, o_ref, h_scr, acc_scr = rest
    else:
        o_ref, h_scr, acc_scr = rest
    i = pl.program_id(0)
    f = pl.program_id(1)
    cur = lax.rem(i, 2)

    @pl.when((i == 0) & (f == 0))
    def _():
        m = m_ref[0]
        h = _modnorm(x_ref[...], nw_ref[...], m[3 * sub + 1:3 * sub + 2], m[3 * sub:3 * sub + 1])
        h_scr[0] = h.astype(BF16)

    @pl.when(f == 0)
    def _():
        acc_scr[...] = jnp.zeros_like(acc_scr)

    _norm_next_slab(xn_ref, mn_ref, nw_ref, h_scr, 1 - cur, f, sub, nslab)
    h = h_scr[cur]
    g = _dot(h, wg_ref[...])
    u = _dot(h, wu_ref[...])
    acc_scr[...] += _dot((_silu(g) * u).astype(BF16), wd_ref[...])

    @pl.when(f == pl.num_programs(1) - 1)
    def _():
        m = m_ref[0]
        y = x_ref[...] + 0.5 * m[3 * sub + 2:3 * sub + 3] * acc_scr[...]
        if final:
            y = y * lax.rsqrt(jnp.mean(y * y, axis=-1, keepdims=True) + EPS) * fin_ref[...]
        o_ref[...] = y


def _ffn(x2, mods, norm_w, w_up, w_down, sub, tokens_per_mod, final_w=None):
    t, d = x2.shape
    f = w_down.shape[0]
    tm, tf = 512, 512
    nf = f // tf
    nt = t // tm
    tiles_per_mod = tokens_per_mod // tm
    final = final_w is not None

    def nxt(i):
        return jnp.minimum(i + 1, nt - 1)

    in_specs = [pl.BlockSpec((tm, d), lambda i, j: (i, 0)),
                pl.BlockSpec((tm, d), lambda i, j: (nxt(i), 0)),
                pl.BlockSpec((1, N_MOD, d), lambda i, j: (i // tiles_per_mod, 0, 0)),
                pl.BlockSpec((1, N_MOD, d), lambda i, j: (nxt(i) // tiles_per_mod, 0, 0)),
                pl.BlockSpec((1, d), lambda i, j: (0, 0)),
                pl.BlockSpec((d, tf), lambda i, j: (0, j)),
                pl.BlockSpec((d, tf), lambda i, j: (0, nf + j)),
                pl.BlockSpec((tf, d), lambda i, j: (j, 0))]
    args = [x2, x2, mods, mods, norm_w.reshape(1, d), w_up, w_up, w_down]
    if final:
        in_specs.append(pl.BlockSpec((1, d), lambda i, j: (0, 0)))
        args.append(final_w.reshape(1, d))
    return pl.pallas_call(
        functools.partial(_ffn_kernel, sub=sub, final=final, nslab=8),
        grid=(nt, nf),
        in_specs=in_specs,
        out_specs=pl.BlockSpec((tm, d), lambda i, j: (i, 0)),
        out_shape=jax.ShapeDtypeStruct((t, d), F32),
        scratch_shapes=[pltpu.VMEM((2, tm, d), BF16), pltpu.VMEM((tm, d), F32)],
        compiler_params=_params(("arbitrary", "arbitrary"), 56),
        name="ffn_final" if final else "ffn",
    )(*args)


def _proj_kernel(x_ref, xn_ref, m_ref, mn_ref, nw_ref, w_ref, ws5_ref, *rest, sub, packed, nslab):
    if packed:
        o_ref, u0_ref, u1_ref, h_scr, a_scr = rest
    else:
        o_ref, s5_ref, h_scr = rest
    i = pl.program_id(0)
    j = pl.program_id(1)
    cur = lax.rem(i, 2)

    @pl.when((i == 0) & (j == 0))
    def _():
        m = m_ref[0]
        h = _modnorm(x_ref[...], nw_ref[...], m[3 * sub + 1:3 * sub + 2], m[3 * sub:3 * sub + 1])
        h_scr[0] = h.astype(BF16)

    @pl.when(j == 0)
    def _():
        s5 = _dot(h_scr[cur], ws5_ref[...])
        if packed:
            _s5_pack(s5, a_scr, (u0_ref, u1_ref))
        else:
            s5_ref[...] = s5

    _norm_next_slab(xn_ref, mn_ref, nw_ref, h_scr, 1 - cur, j, sub, nslab)
    o_ref[...] = _dot(h_scr[cur], w_ref[...])


def _s5_pack(a, a_scr, u_refs):
    tm, width = a.shape
    half = GRID_W // 2
    n_rl = tm // GRID_W
    for k in range(width // LANES):
        a_scr[k] = a[:, k * LANES:(k + 1) * LANES]
    for cc in range(2):
        acc = jnp.concatenate([a_scr[k, pl.ds(cc, tm // 2, stride=2), :] for k in range(width // LANES)], axis=1)
        for g in range(width // S5_GROUP):
            piece = jnp.concatenate([acc[rl * half:(rl + 1) * half, g * S5_GROUP:(g + 1) * S5_GROUP]
                                     for rl in range(n_rl)], axis=1)
            u_refs[cc][g] = piece.astype(BF16)


def _proj(x2, mods, norm_w, w, n_main, w_s5, sub, tokens_per_mod, packed):
    t, d = x2.shape
    tm, tn = 512, 896
    assert n_main % tn == 0
    nt = t // tm
    tiles_per_mod = tokens_per_mod // tm
    s5w = w_s5.shape[1]

    def nxt(i):
        return jnp.minimum(i + 1, nt - 1)

    in_specs = [pl.BlockSpec((tm, d), lambda i, j: (i, 0)),
                pl.BlockSpec((tm, d), lambda i, j: (nxt(i), 0)),
                pl.BlockSpec((1, N_MOD, d), lambda i, j: (i // tiles_per_mod, 0, 0)),
                pl.BlockSpec((1, N_MOD, d), lambda i, j: (nxt(i) // tiles_per_mod, 0, 0)),
                pl.BlockSpec((1, d), lambda i, j: (0, 0)),
                pl.BlockSpec((d, tn), lambda i, j: (0, j)),
                pl.BlockSpec((d, s5w), lambda i, j: (0, 0))]
    args = [x2, x2, mods, mods, norm_w.reshape(1, d), w, w_s5]
    out_specs = [pl.BlockSpec((tm, tn), lambda i, j: (i, j))]
    out_shape = [jax.ShapeDtypeStruct((t, n_main), F32)]
    scratch = [pltpu.VMEM((2, tm, d), BF16)]
    if packed:
        groups = s5w // S5_GROUP
        half = GRID_W // 2
        lanes_per_tile = (tm // GRID_W) * S5_GROUP
        rows_total = tokens_per_mod // GRID_W
        n_b = t // tokens_per_mod
        u_spec = pl.BlockSpec((groups, half, lanes_per_tile), lambda i, j: (0, i // tiles_per_mod, i % tiles_per_mod))
        u_shape = jax.ShapeDtypeStruct((groups, n_b * half, rows_total * S5_GROUP), BF16)
        out_specs += [u_spec, u_spec]
        out_shape += [u_shape, u_shape]
        scratch.append(pltpu.VMEM((s5w // LANES, tm, LANES), F32))
    else:
        out_specs.append(pl.BlockSpec((tm, s5w), lambda i, j: (i, 0)))
        out_shape.append(jax.ShapeDtypeStruct((t, s5w), F32))
    return pl.pallas_call(
        functools.partial(_proj_kernel, sub=sub, packed=packed, nslab=4),
        grid=(nt, n_main // tn),
        in_specs=in_specs,
        out_specs=out_specs,
        out_shape=out_shape,
        scratch_shapes=scratch,
        compiler_params=_params(("arbitrary", "arbitrary"), 48),
        name="in_proj_s5" if packed else "in_proj",
    )(*args)


def _conv_kernel(p_ref, w_ref, o_ref, xc_scr, xl_scr, xr_scr, *, rows, width, strip, n_q_tiles, n_qk_tiles):
    length = rows * width
    pad = (xc_scr.shape[0] - length) // 2
    tc = p_ref.shape[2]
    j = pl.program_id(1)
    x = p_ref[0]
    col = lax.broadcasted_iota(jnp.int32, (length, 1), 0) & (width - 1)
    zeros = jnp.zeros((pad, tc), F32)
    for scr, val in ((xc_scr, x), (xl_scr, jnp.where(col != width - 1, x, 0.0)), (xr_scr, jnp.where(col != 0, x, 0.0))):
        scr[0:pad, :] = zeros
        scr[pad + length:pad + length + pad, :] = zeros
        scr[pad:pad + length, :] = val
    w = w_ref[...]
    q_scale = jnp.where(j < n_q_tiles, HEAD_DIM ** -0.5, 1.0).astype(F32)
    row_taps = (-1, 0, 1) if rows > 1 else (0,)
    srcs = {-1: xl_scr, 0: xc_scr, 1: xr_scr}

    def strips(normalize):
        for s in range(length // strip):
            t0 = s * strip
            acc = jnp.zeros((strip, tc), F32)
            for dr in row_taps:
                for dc in (-1, 0, 1):
                    off = pad + t0 + dr * width + dc
                    tap = (dr + 1) * 3 + (dc + 1)
                    acc = acc + srcs[dc][off:off + strip, :] * w[tap:tap + 1, :]
            y = _silu(acc)
            if not normalize:
                o_ref[0, t0:t0 + strip, :] = y
                continue
            for hh in range(tc // HEAD_DIM):
                yh = y[:, hh * HEAD_DIM:(hh + 1) * HEAD_DIM]
                inv = lax.rsqrt(jnp.sum(yh * yh, axis=-1, keepdims=True) + EPS) * q_scale
                o_ref[0, t0:t0 + strip, hh * HEAD_DIM:(hh + 1) * HEAD_DIM] = yh * inv

    @pl.when(j < n_qk_tiles)
    def _():
        strips(True)

    @pl.when(j >= n_qk_tiles)
    def _():
        strips(False)


def _conv(p3, conv_w9, rows, width, n_conv, n_qk):
    b, length, _ = p3.shape
    tc = 512
    pad = width + 8 if rows > 1 else 8
    strip = min(256, length)
    kern = functools.partial(_conv_kernel, rows=rows, width=width, strip=strip,
                             n_q_tiles=(n_qk // 2) // tc, n_qk_tiles=n_qk // tc)
    return pl.pallas_call(
        kern,
        grid=(b, n_conv // tc),
        in_specs=[pl.BlockSpec((1, length, tc), lambda i, j: (i, 0, j)),
                  pl.BlockSpec((9, tc), lambda i, j: (0, j))],
        out_specs=pl.BlockSpec((1, length, tc), lambda i, j: (i, 0, j)),
        out_shape=jax.ShapeDtypeStruct((b, length, n_conv), F32),
        scratch_shapes=[pltpu.VMEM((length + 2 * pad, tc), F32)] * 3,
        compiler_params=_params(("parallel", "parallel"), 48),
        name="grid_conv",
    )(p3, conv_w9)


def _unit_tri_inverse(a_list, strict_list, row, col):
    n = 1
    e = None
    while n < CHUNK:
        k = n.bit_length() - 1
        pair = ((row >> (k + 1)) == (col >> (k + 1))) & ((row >> k) != (col >> k))
        l_n = [jnp.where(pair & st, a, 0.0) for a, st in zip(a_list, strict_list)]
        if e is None:
            e = [-l for l in l_n]
        else:
            e16 = [x.astype(BF16) for x in e]
            y = [l + _dot(l.astype(BF16), x16) for l, x16 in zip(l_n, e16)]
            e = [x - yy - _dot(x16, yy.astype(BF16)) for x, x16, yy in zip(e, e16, y)]
        n *= 2
    return e


def _delta_kernel(*refs, n_heads, with_init, with_output):
    chunk_refs = (refs[0:4], refs[4:8])
    alog_ref, dtb_ref = refs[8:10]
    rest = refs[10:]
    if with_init:
        s0_ref, rest = rest[0], rest[1:]
    if with_output:
        o_refs, s_scr = rest[0:2], rest[2]
    else:
        sfin_ref, s_scr = rest
    c = pl.program_id(1)

    @pl.when(c == 0)
    def _():
        if with_init:
            s_scr[...] = s0_ref[0]
        else:
            s_scr[...] = jnp.zeros_like(s_scr)

    row = lax.broadcasted_iota(jnp.int32, (CHUNK, CHUNK), 0)
    col = lax.broadcasted_iota(jnp.int32, (CHUNK, CHUNK), 1)
    incl_d = (row >= col, row <= col)
    strict_d = (row > col, row < col)

    chains = [(d, h) for d in range(2) for h in range(n_heads)]
    gc, gt, bt, decay, kh, vh, qh, strict = [], [], [], [], [], [], [], []
    for d in range(2):
        q_ref, k_ref, v_ref, gb_ref = chunk_refs[d]
        gb = gb_ref[0]
        x = gb + dtb_ref[...]
        softplus = jnp.maximum(x, 0.0) + jnp.log1p(jnp.exp(-jnp.abs(x)))
        g_all = -jnp.exp(alog_ref[...]) * softplus
        beta_all = _sigmoid(gb)
        gcum = jnp.dot(incl_d[d].astype(F32), g_all, precision=lax.Precision.HIGHEST,
                       preferred_element_type=F32)
        gtot = jnp.sum(g_all, axis=0, keepdims=True)
        gcum_t = jnp.concatenate([gcum, jnp.zeros_like(gcum)], axis=0).T
        for h in range(n_heads):
            sl = slice(h * HEAD_DIM, (h + 1) * HEAD_DIM)
            lane = d * n_heads + h
            gc.append(gcum[:, lane:lane + 1])
            gt.append(gtot[:, lane:lane + 1])
            bt.append(beta_all[:, 2 * n_heads + lane:2 * n_heads + lane + 1])
            decay.append(jnp.where(incl_d[d], jnp.exp(jnp.where(incl_d[d], gc[-1] - gcum_t[lane:lane + 1, 0:CHUNK], 0.0)), 0.0))
            kh.append(k_ref[0, :, sl])
            vh.append(v_ref[0, :, sl])
            if with_output:
                qh.append(q_ref[0, :, sl])
            strict.append(strict_d[d])
    n = range(len(chains))
    k16 = [x.astype(BF16) for x in kh]
    if with_output:
        kq = [_dot_nt(jnp.concatenate([k16[i], qh[i].astype(BF16)], axis=0), k16[i]) for i in n]
        kk = [x[:CHUNK] for x in kq]
        qk = [x[CHUNK:] for x in kq]
    else:
        kk = [_dot_nt(x, x) for x in k16]
    a_mat = [jnp.where(strict[i], bt[i] * decay[i] * kk[i], 0.0) for i in n]
    e = _unit_tri_inverse(a_mat, strict, row, col)
    eg = [jnp.exp(x) for x in gc]
    rhs = [jnp.concatenate([(bt[i] * eg[i]) * kh[i], bt[i] * vh[i]], axis=1) for i in n]
    sol = [rhs[i] + _dot(e[i].astype(BF16), rhs[i].astype(BF16)) for i in n]
    s_h = [s_scr[d, h] for d, h in chains]
    s16 = [x.astype(BF16) for x in s_h]
    if with_output:
        wq = [_dot(jnp.concatenate([sol[i][:, :HEAD_DIM].astype(BF16), (qh[i] * eg[i]).astype(BF16)], axis=0), s16[i])
              for i in n]
        ws = [x[:CHUNK] for x in wq]
        qs = [x[CHUNK:] for x in wq]
    else:
        ws = [_dot(sol[i][:, :HEAD_DIM].astype(BF16), s16[i]) for i in n]
    u16 = [(sol[i][:, HEAD_DIM:] - ws[i]).astype(BF16) for i in n]
    if with_output:
        for i, (d, h) in enumerate(chains):
            o_refs[d][0, :, h * HEAD_DIM:(h + 1) * HEAD_DIM] = qs[i] + _dot((qk[i] * decay[i]).astype(BF16), u16[i])
    for i, (d, h) in enumerate(chains):
        k_dec = kh[i] * jnp.exp(gt[i] - gc[i])
        s_scr[d, h] = jnp.exp(gt[i]) * s_h[i] + _dot_tn(k_dec.astype(BF16), u16[i])

    if not with_output:
        @pl.when(c == pl.num_programs(1) - 1)
        def _():
            sfin_ref[0] = s_scr[...]


def _delta(qkv, p3, gate_block, alog_r, dtb_r, n_heads, s0=None):
    b, length, _ = qkv.shape
    nch = length // CHUNK
    hd = n_heads * HEAD_DIM
    with_init = s0 is not None
    with_output = with_init

    in_specs, args = [], []
    for d in range(2):
        pos = (lambda c: c) if d == 0 else (lambda c: nch - 1 - c)
        in_specs += [pl.BlockSpec((1, CHUNK, hd), lambda i, c, pos=pos: (i, pos(c), 0)),
                     pl.BlockSpec((1, CHUNK, hd), lambda i, c, pos=pos: (i, pos(c), 1)),
                     pl.BlockSpec((1, CHUNK, hd), lambda i, c, pos=pos: (i, pos(c), 2)),
                     pl.BlockSpec((1, CHUNK, LANES), lambda i, c, pos=pos: (i, pos(c), gate_block))]
        args += [qkv, qkv, qkv, p3]
    in_specs += [pl.BlockSpec((1, LANES), lambda i, c: (0, 0))] * 2
    args += [alog_r, dtb_r]
    state_spec = pl.BlockSpec((1, 2, n_heads, HEAD_DIM, HEAD_DIM), lambda i, c: (i, 0, 0, 0, 0))
    if with_init:
        in_specs.append(state_spec)
        args.append(s0)
    if with_output:
        out_specs = [pl.BlockSpec((1, CHUNK, hd), lambda i, c: (i, c, 0)),
                     pl.BlockSpec((1, CHUNK, hd), lambda i, c: (i, nch - 1 - c, 0))]
        out_shape = [jax.ShapeDtypeStruct((b, length, hd), F32)] * 2
    else:
        out_specs = state_spec
        out_shape = jax.ShapeDtypeStruct((b, 2, n_heads, HEAD_DIM, HEAD_DIM), F32)
    return pl.pallas_call(
        functools.partial(_delta_kernel, n_heads=n_heads, with_init=with_init, with_output=with_output),
        grid=(b, nch),
        in_specs=in_specs,
        out_specs=out_specs,
        out_shape=out_shape,
        scratch_shapes=[pltpu.VMEM((2, n_heads, HEAD_DIM, HEAD_DIM), F32)],
        compiler_params=_params(("parallel", "arbitrary"), 40),
        name="delta_lat" if with_output else "delta_ctx",
    )(*args)


def _cmul(ar, ai, br, bi):
    return ar * br - ai * bi, ar * bi + ai * br


def _s5_tables(a_re, a_im, log_dt, b_re, b_im, c_re, c_im):
    g, p, s = b_re.shape
    dt = jnp.exp(log_dt.astype(F32))[..., None]
    lr, li = a_re.astype(F32) * dt, a_im.astype(F32) * dt

    def powers(d, n):
        n = n.astype(F32)
        mag = jnp.exp(lr[d][..., None] * n)
        ang = li[d][..., None] * n
        return mag * jnp.cos(ang), mag * jnp.sin(ang)

    abr = jnp.exp(lr) * jnp.cos(li)
    abi = jnp.exp(lr) * jnp.sin(li)
    nr, ni = abr - 1.0, abi
    cr, ci = a_re.astype(F32), a_im.astype(F32)
    den = cr * cr + ci * ci
    fr, fi = (nr * cr + ni * ci) / den, (ni * cr - nr * ci) / den
    bbr, bbi = _cmul(fr[..., None], fi[..., None], b_re.astype(F32)[None], b_im.astype(F32)[None])
    ccr = jnp.swapaxes(c_re.astype(F32), 1, 2)
    cci = jnp.swapaxes(c_im.astype(F32), 1, 2)
    sub = jnp.arange(S5_SUB)

    def x_table(d, n):
        er, ei = powers(d, n)
        xr, xi = _cmul(bbr[d][:, :, None, :], bbi[d][:, :, None, :], er[..., None], ei[..., None])
        xr = jnp.transpose(xr, (0, 2, 3, 1)).reshape(g, S5_SUB * s, p)
        xi = jnp.transpose(xi, (0, 2, 3, 1)).reshape(g, S5_SUB * s, p)
        return jnp.concatenate([xr, xi], axis=-1)

    def y_table(d, n):
        er, ei = powers(d, n)
        yr, yi = _cmul(ccr[:, :, None, :], cci[:, :, None, :], er[..., None], ei[..., None])
        return jnp.concatenate([yr.reshape(g, p, S5_SUB * s), -yi.reshape(g, p, S5_SUB * s)], axis=1)

    xt = jnp.stack([x_table(0, S5_SUB - 1 - sub), x_table(1, sub)], axis=1)
    yt = jnp.stack([y_table(0, sub + 1), y_table(0, sub - (S5_SUB - 1)), y_table(1, -sub), y_table(1, S5_SUB - sub)], axis=1)
    rows = []
    for d in range(2):
        er, ei = powers(d, S5_SUB * jnp.arange(CHUNK // S5_SUB + 1))
        for k in range(CHUNK // S5_SUB + 1):
            rows += [jnp.concatenate([er[..., k], er[..., k]], -1), jnp.concatenate([-ei[..., k], ei[..., k]], -1)]
    pw = jnp.stack(rows, axis=1)
    return xt, yt, pw


def _s5_kernel(uc_ref, u0_ref, u1_ref, xt_ref, yt_ref, pw_ref, dsk_ref, y_ref, bst_scr, m_scr, h_scr, y_scr,
               *, nb, grid_rows):
    nsub = CHUNK // S5_SUB
    n_pw = 2 * (nsub + 1)
    n_lat = u0_ref.shape[1] // nb
    n_ctx = uc_ref.shape[1] // nb
    row = lax.broadcasted_iota(jnp.int32, (LANES, LANES), 0)
    col = lax.broadcasted_iota(jnp.int32, (LANES, LANES), 1)
    jj, ii = row >> 4, col >> 4

    def hdot(a, b):
        return jnp.dot(a, b, precision=lax.Precision.HIGHEST, preferred_element_type=F32)

    def one_group(gi, carry):
        pw = pw_ref[gi]

        def crot(x, d, k, pw=pw):
            r0 = d * n_pw + 2 * k
            return pw[r0:r0 + 1] * x + pw[r0 + 1:r0 + 2] * pltpu.roll(x, LANES // 2, axis=1)

        xf, xb = xt_ref[gi, 0], xt_ref[gi, 1]
        xfs = [xf] + [crot(xf, 0, k) for k in range(1, nsub)]
        xbs = [xb] + [crot(xb, 1, k) for k in range(1, nsub)]
        yf1, yf0, yb0, yb1 = yt_ref[gi, 0], yt_ref[gi, 1], yt_ref[gi, 2], yt_ref[gi, 3]

        for j in range(nsub):
            bst_scr[j * LANES:(j + 1) * LANES, 0:LANES] = xfs[nsub - 1 - j].astype(BF16)
            bst_scr[j * LANES:(j + 1) * LANES, LANES:2 * LANES] = xbs[j].astype(BF16)
        bst = bst_scr[...]
        ul = jnp.concatenate([u0_ref[gi], u1_ref[gi]], axis=1)
        hc = _dot(uc_ref[gi], bst)
        hl = _dot(ul, bst)
        h_scr[0] = hl[:, 0:LANES]
        h_scr[1] = hl[:, LANES:2 * LANES]

        hf = jnp.zeros((nb, LANES), F32)
        hb = jnp.zeros((nb, LANES), F32)
        for c in range(n_ctx):
            hf = crot(hf, 0, nsub) + hc[c * nb:(c + 1) * nb, 0:LANES]
        for c in reversed(range(n_ctx)):
            hb = crot(hb, 1, nsub) + hc[c * nb:(c + 1) * nb, LANES:2 * LANES]
        for c in range(n_lat):
            rows_c = pl.ds(c, nb, stride=n_lat)
            inp = h_scr[0, rows_c, :]
            h_scr[0, rows_c, :] = hf
            hf = crot(hf, 0, nsub) + inp
        for c in reversed(range(n_lat)):
            rows_c = pl.ds(c, nb, stride=n_lat)
            inp = h_scr[1, rows_c, :]
            h_scr[1, rows_c, :] = hb
            hb = crot(hb, 1, nsub) + inp

        diag = (jnp.where(ii >= jj, hdot(xf, yf0), 0.0) + jnp.where(jj >= ii, hdot(xb, yb0), 0.0)
                + jnp.where(row == col, dsk_ref[gi], 0.0)).astype(BF16)
        taps_f = [None] + [hdot(xfs[dl - 1], yf1).astype(BF16) for dl in range(1, nsub)]
        taps_b = [None] + [hdot(xbs[dl], yb0).astype(BF16) for dl in range(1, nsub)]
        for bj in range(nsub):
            for bi in range(nsub):
                blk = diag if bi == bj else (taps_f[bi - bj] if bi > bj else taps_b[bj - bi])
                m_scr[bj * LANES:(bj + 1) * LANES, bi * LANES:(bi + 1) * LANES] = blk
        y = _dot(ul, m_scr[...])
        hin_f, hin_b = h_scr[0], h_scr[1]
        yf16, yb16 = yf1.astype(BF16), yb1.astype(BF16)
        for bi in range(nsub):
            y_scr[gi, :, bi * LANES:(bi + 1) * LANES] = (
                y[:, bi * LANES:(bi + 1) * LANES]
                + _dot(crot(hin_f, 0, bi).astype(BF16), yf16)
                + _dot(crot(hin_b, 1, nsub - 1 - bi).astype(BF16), yb16))
        return carry

    lax.fori_loop(0, u0_ref.shape[0], one_group, 0)

    half = GRID_W // 2
    tokens = grid_rows * GRID_W
    for cc in range(2):
        for r in range(grid_rows):
            lane0 = (cc * grid_rows + r) * S5_GROUP
            piece = jnp.concatenate([y_scr[gi, :, lane0:lane0 + S5_GROUP] for gi in range(u0_ref.shape[0])], axis=1)
            for b in range(nb):
                y_ref[pl.ds(b * tokens + r * GRID_W + cc, half, stride=2), :] = piece[b * half:(b + 1) * half]


def _s5(uc, u0, u1, tables, d_skip, nb, grid_rows):
    xt, yt, pw = tables
    g = u0.shape[0]
    rows_l = u0.shape[1]
    kdim = 2 * u0.shape[2]
    tokens = grid_rows * GRID_W
    gstep = LANES // S5_GROUP
    dsk = jnp.tile(d_skip.astype(F32).reshape(g, 1, S5_GROUP), (1, 1, S5_SUB))

    def spec(a):
        nd = a.ndim
        return pl.BlockSpec((gstep,) + tuple(a.shape[1:]), lambda i: (i,) + (0,) * (nd - 1))

    arrays = [uc, u0, u1, xt, yt, pw, dsk]
    return pl.pallas_call(
        functools.partial(_s5_kernel, nb=nb, grid_rows=grid_rows),
        grid=(g // gstep,),
        in_specs=[spec(a) for a in arrays],
        out_specs=pl.BlockSpec((nb * tokens, LANES), lambda i: (0, i)),
        out_shape=jax.ShapeDtypeStruct((nb * tokens, g * S5_GROUP), F32),
        scratch_shapes=[pltpu.VMEM((kdim, 2 * LANES), BF16), pltpu.VMEM((kdim, kdim), BF16),
                        pltpu.VMEM((2, rows_l, LANES), F32), pltpu.VMEM((gstep, rows_l, kdim), F32)],
        compiler_params=_params(("parallel",), 48),
        name="s5_scan",
    )(*arrays)


def _gelu_tanh(x):
    return 0.5 * x * (1.0 + jnp.tanh(0.7978845608028654 * (x + 0.044715 * (x * x * x))))


def _out_kernel(of_ref, ob_ref, z_ref, y_ref, x_ref, m_ref, dnw_ref, wglu_ref, wdn_ref, ws5_ref, out_ref, *, n_heads):
    o = of_ref[...] + ob_ref[...]
    z = z_ref[...]
    dnw = dnw_ref[...]
    parts = []
    for h in range(n_heads):
        sl = slice(h * HEAD_DIM, (h + 1) * HEAD_DIM)
        oh = o[:, sl]
        yh = oh * lax.rsqrt(jnp.mean(oh * oh, axis=-1, keepdims=True) + EPS) * dnw
        parts.append((yh * _silu(z[:, sl])).astype(BF16))
    dn = jnp.concatenate(parts, axis=1)
    t = _dot(_gelu_tanh(y_ref[...]).astype(BF16), wglu_ref[...])
    half = t.shape[1] // 2
    s5 = (t[:, :half] * _sigmoid(t[:, half:])).astype(BF16)
    acc = _dot(dn, wdn_ref[...]) + _dot(s5, ws5_ref[...])
    out_ref[...] = x_ref[...] + m_ref[0][5:6] * acc


def _out_proj(o_fwd, o_bwd, p2, z_block, y2, x2, mods, dn_norm, w_glu, w_out, n_heads, tokens_per_mod):
    t, d = x2.shape
    hd = n_heads * HEAD_DIM
    s5w = y2.shape[1]
    tm = 256
    tiles_per_mod = tokens_per_mod // tm
    return pl.pallas_call(
        functools.partial(_out_kernel, n_heads=n_heads),
        grid=(t // tm,),
        in_specs=[pl.BlockSpec((tm, hd), lambda i: (i, 0)),
                  pl.BlockSpec((tm, hd), lambda i: (i, 0)),
                  pl.BlockSpec((tm, hd), lambda i: (i, z_block)),
                  pl.BlockSpec((tm, s5w), lambda i: (i, 0)),
                  pl.BlockSpec((tm, d), lambda i: (i, 0)),
                  pl.BlockSpec((1, N_MOD, d), lambda i: (i // tiles_per_mod, 0, 0)),
                  pl.BlockSpec((1, HEAD_DIM), lambda i: (0, 0)),
                  pl.BlockSpec(w_glu.shape, lambda i: (0, 0)),
                  pl.BlockSpec((hd, d), lambda i: (0, 0)),
                  pl.BlockSpec((s5w, d), lambda i: (hd // s5w, 0))],
        out_specs=pl.BlockSpec((tm, d), lambda i: (i, 0)),
        out_shape=jax.ShapeDtypeStruct((t, d), F32),
        compiler_params=_params(("parallel",), 48),
        name="out_proj",
    )(o_fwd, o_bwd, p2, y2, x2, mods, dn_norm.reshape(1, HEAD_DIM), w_glu, w_out, w_out)


def _layer(x, ctx, m_lat, m_ctx, norm_ffn1, ffn1_up, ffn1_down, norm_mix, w_in, dn_conv, dn_a_log,
           dn_dt_bias, dn_norm, s5_a_re, s5_a_im, s5_log_dt, s5_b_re, s5_b_im, s5_c_re, s5_c_im, s5_d,
           s5_glu, w_out, norm_ffn2, ffn2_up, ffn2_down, final_norm):
    b, length, d = x.shape
    lc = ctx.shape[1]
    n_heads = dn_a_log.shape[1]
    hd = n_heads * HEAD_DIM
    n_conv = dn_conv.shape[-1]
    n_qk = n_conv - hd
    s5w = s5_d.shape[0]
    groups = s5w // S5_GROUP
    rows = length // GRID_W

    gate0 = n_conv + hd
    w_in16 = w_in.astype(BF16)
    w_s5 = w_in16[:, gate0 + 4 * n_heads:]
    n_main = gate0 + LANES
    z_block = n_conv // hd
    gate_block = gate0 // LANES
    lane_pad = jnp.zeros((LANES - 2 * n_heads,), F32)
    alog_r = jnp.concatenate([dn_a_log.astype(F32).reshape(-1), lane_pad]).reshape(1, LANES)
    dtb_r = jnp.concatenate([dn_dt_bias.astype(F32).reshape(-1), lane_pad]).reshape(1, LANES)
    conv_w9 = dn_conv.reshape(9, n_conv)

    x2 = x.reshape(b * length, d)
    c2 = ctx.reshape(b * lc, d)

    up1, down1 = ffn1_up.astype(BF16), ffn1_down.astype(BF16)
    x2 = _ffn(x2, m_lat, norm_ffn1, up1, down1, 0, length)
    c2 = _ffn(c2, m_ctx, norm_ffn1, up1, down1, 0, b * lc)

    p_lat, u_lat0, u_lat1 = _proj(x2, m_lat, norm_mix, w_in16, n_main, w_s5, 1, length, packed=True)
    p_ctx, u_ctx = _proj(c2, m_ctx, norm_mix, w_in16, n_main, w_s5, 1, b * lc, packed=False)
    p_lat3 = p_lat.reshape(b, length, n_main)
    p_ctx3 = p_ctx.reshape(b, lc, n_main)

    qkv_lat = _conv(p_lat3, conv_w9, rows, GRID_W, n_conv, n_qk)
    qkv_ctx = _conv(p_ctx3, conv_w9, 1, lc, n_conv, n_qk)
    s_ctx = _delta(qkv_ctx, p_ctx3, gate_block, alog_r, dtb_r, n_heads)
    o_fwd, o_bwd = _delta(qkv_lat, p_lat3, gate_block, alog_r, dtb_r, n_heads, s0=s_ctx)

    tables = _s5_tables(s5_a_re, s5_a_im, s5_log_dt, s5_b_re, s5_b_im, s5_c_re, s5_c_im)
    u_ctx = u_ctx.astype(BF16).reshape(b, lc // CHUNK, CHUNK, groups, S5_GROUP)
    u_ctx = jnp.transpose(u_ctx, (3, 1, 0, 2, 4)).reshape(groups, (lc // CHUNK) * b, CHUNK * S5_GROUP)
    y2 = _s5(u_ctx, u_lat0, u_lat1, tables, s5_d, b, rows)

    x2 = _out_proj(o_fwd.reshape(b * length, hd), o_bwd.reshape(b * length, hd), p_lat, z_block, y2, x2, m_lat, dn_norm,
                   s5_glu.astype(BF16), w_out.astype(BF16), n_heads, length)

    x2 = _ffn(x2, m_lat, norm_ffn2, ffn2_up.astype(BF16), ffn2_down.astype(BF16), 2, length,
              final_w=final_norm)
    return x2.reshape(b, length, d)


def kernel(x, c, ctx, c_ctx, w_mod, b_mod, norm_ffn1, ffn1_up, ffn1_down, norm_mix, w_in, dn_conv, dn_a_log, dn_dt_bias, dn_norm, s5_a_re, s5_a_im, s5_log_dt, s5_b_re, s5_b_im, s5_c_re, s5_c_im, s5_d, s5_glu, w_out, norm_ffn2, ffn2_up, ffn2_down, final_norm):
    depth = w_mod.shape[0]
    assert depth == 1, "the context stream update of deeper stacks is not implemented"
    b, _, d = x.shape
    cond = jnp.concatenate([c, c_ctx[None], jnp.zeros((16 - b - 1, d), c.dtype)], axis=0)
    m = _ada(cond, w_mod[0], b_mod[0]).reshape(16, N_MOD, d)
    return _layer(x, ctx, m[:b], m[b:b + 1], norm_ffn1[0], ffn1_up[0], ffn1_down[0], norm_mix[0], w_in[0],
                  dn_conv[0], dn_a_log[0], dn_dt_bias[0], dn_norm[0], s5_a_re[0], s5_a_im[0], s5_log_dt[0],
                  s5_b_re[0], s5_b_im[0], s5_c_re[0], s5_c_im[0], s5_d[0], s5_glu[0], w_out[0],
                  norm_ffn2[0], ffn2_up[0], ffn2_down[0], final_norm)
```

```python
import functools

import jax
import jax.numpy as jnp
from jax import lax
from jax.experimental import pallas as pl
from jax.experimental.pallas import tpu as pltpu

F32 = jnp.float32
BF16 = jnp.bfloat16
EPS = 1e-6
N_MOD = 9
GRID_W = 64
CHUNK = 64
HEAD_DIM = 128
S5_GROUP = 16
S5_SUB = 8
LANES = 128
MIB = 1024 * 1024


def _params(semantics, vmem_mib):
    return pltpu.CompilerParams(dimension_semantics=semantics, vmem_limit_bytes=vmem_mib * MIB)


def _sigmoid(x):
    return 1.0 / (1.0 + jnp.exp(-x))


def _silu(x):
    return x * _sigmoid(x)


def _dot(a, b):
    return jnp.dot(a, b, preferred_element_type=F32)


def _dot_nt(a, b):
    return lax.dot_general(a, b, (((1,), (1,)), ((), ())), preferred_element_type=F32)


def _dot_tn(a, b):
    return lax.dot_general(a, b, (((0,), (0,)), ((), ())), preferred_element_type=F32)


def _modnorm(x, norm_w, scale, shift):
    y = x * lax.rsqrt(jnp.mean(x * x, axis=-1, keepdims=True) + EPS) * norm_w
    return y * (1.0 + scale) + shift


def _ada_kernel(c_ref, w_ref, b_ref, o_ref):
    a = _silu(c_ref[...]).astype(BF16)
    o_ref[...] = _dot(a, w_ref[...].astype(BF16)) + b_ref[...]


def _ada(cond, w_mod, b_mod):
    rows, d = cond.shape
    n = w_mod.shape[1]
    tn = 1024
    return pl.pallas_call(
        _ada_kernel,
        grid=(n // tn,),
        in_specs=[pl.BlockSpec((rows, d), lambda j: (0, 0)),
                  pl.BlockSpec((d, tn), lambda j: (0, j)),
                  pl.BlockSpec((1, tn), lambda j: (0, j))],
        out_specs=pl.BlockSpec((rows, tn), lambda j: (0, j)),
        out_shape=jax.ShapeDtypeStruct((rows, n), F32),
        compiler_params=_params(("parallel",), 40),
        name="ada_mod",
    )(cond, w_mod, b_mod.reshape(1, n))


def _ffn_kernel(x_ref, m_ref, nw_ref, wg_ref, wu_ref, wd_ref, *rest, sub, final):
    if final:
        fin_ref, o_ref, h_scr, acc_scr = rest
    else:
        o_ref, h_scr, acc_scr = rest
    f = pl.program_id(1)

    @pl.when(f == 0)
    def _():
        m = m_ref[0]
        h = _modnorm(x_ref[...], nw_ref[...], m[3 * sub + 1:3 * sub + 2], m[3 * sub:3 * sub + 1])
        h_scr[...] = h.astype(BF16)
        acc_scr[...] = jnp.zeros_like(acc_scr)

    h = h_scr[...]
    g = _dot(h, wg_ref[...])
    u = _dot(h, wu_ref[...])
    acc_scr[...] += _dot((_silu(g) * u).astype(BF16), wd_ref[...])

    @pl.when(f == pl.num_programs(1) - 1)
    def _():
        m = m_ref[0]
        y = x_ref[...] + 0.5 * m[3 * sub + 2:3 * sub + 3] * acc_scr[...]
        if final:
            y = y * lax.rsqrt(jnp.mean(y * y, axis=-1, keepdims=True) + EPS) * fin_ref[...]
        o_ref[...] = y


def _ffn(x2, mods, norm_w, w_up, w_down, sub, tokens_per_mod, final_w=None):
    t, d = x2.shape
    f = w_down.shape[0]
    tm, tf = 512, 512
    nf = f // tf
    tiles_per_mod = tokens_per_mod // tm
    final = final_w is not None
    in_specs = [pl.BlockSpec((tm, d), lambda i, j: (i, 0)),
                pl.BlockSpec((1, N_MOD, d), lambda i, j: (i // tiles_per_mod, 0, 0)),
                pl.BlockSpec((1, d), lambda i, j: (0, 0)),
                pl.BlockSpec((d, tf), lambda i, j: (0, j)),
                pl.BlockSpec((d, tf), lambda i, j: (0, nf + j)),
                pl.BlockSpec((tf, d), lambda i, j: (j, 0))]
    args = [x2, mods, norm_w.reshape(1, d), w_up, w_up, w_down]
    if final:
        in_specs.append(pl.BlockSpec((1, d), lambda i, j: (0, 0)))
        args.append(final_w.reshape(1, d))
    return pl.pallas_call(
        functools.partial(_ffn_kernel, sub=sub, final=final),
        grid=(t // tm, nf),
        in_specs=in_specs,
        out_specs=pl.BlockSpec((tm, d), lambda i, j: (i, 0)),
        out_shape=jax.ShapeDtypeStruct((t, d), F32),
        scratch_shapes=[pltpu.VMEM((tm, d), BF16), pltpu.VMEM((tm, d), F32)],
        compiler_params=_params(("parallel", "arbitrary"), 48),
        name="ffn_final" if final else "ffn",
    )(*args)


def _proj_kernel(x_ref, m_ref, nw_ref, w_ref, ws5_ref, *rest, sub, packed):
    if packed:
        o_ref, u0_ref, u1_ref, h_scr, a_scr = rest
    else:
        o_ref, s5_ref, h_scr = rest

    @pl.when(pl.program_id(1) == 0)
    def _():
        m = m_ref[0]
        h = _modnorm(x_ref[...], nw_ref[...], m[3 * sub + 1:3 * sub + 2], m[3 * sub:3 * sub + 1])
        h_scr[...] = h.astype(BF16)
        s5 = _dot(h_scr[...], ws5_ref[...])
        if packed:
            _s5_pack(s5, a_scr, (u0_ref, u1_ref))
        else:
            s5_ref[...] = s5

    o_ref[...] = _dot(h_scr[...], w_ref[...])


def _s5_pack(a, a_scr, u_refs):
    tm, width = a.shape
    half = GRID_W // 2
    n_rl = tm // GRID_W
    for k in range(width // LANES):
        a_scr[k] = a[:, k * LANES:(k + 1) * LANES]
    for cc in range(2):
        acc = jnp.concatenate([a_scr[k, pl.ds(cc, tm // 2, stride=2), :] for k in range(width // LANES)], axis=1)
        for g in range(width // S5_GROUP):
            piece = jnp.concatenate([acc[rl * half:(rl + 1) * half, g * S5_GROUP:(g + 1) * S5_GROUP]
                                     for rl in range(n_rl)], axis=1)
            u_refs[cc][g] = piece.astype(BF16)


def _proj(x2, mods, norm_w, w, n_main, w_s5, sub, tokens_per_mod, packed):
    t, d = x2.shape
    tm, tn = 1024, 896
    assert n_main % tn == 0
    tiles_per_mod = tokens_per_mod // tm
    s5w = w_s5.shape[1]
    in_specs = [pl.BlockSpec((tm, d), lambda i, j: (i, 0)),
                pl.BlockSpec((1, N_MOD, d), lambda i, j: (i // tiles_per_mod, 0, 0)),
                pl.BlockSpec((1, d), lambda i, j: (0, 0)),
                pl.BlockSpec((d, tn), lambda i, j: (0, j)),
                pl.BlockSpec((d, s5w), lambda i, j: (0, 0))]
    args = [x2, mods, norm_w.reshape(1, d), w, w_s5]
    out_specs = [pl.BlockSpec((tm, tn), lambda i, j: (i, j))]
    out_shape = [jax.ShapeDtypeStruct((t, n_main), F32)]
    scratch = [pltpu.VMEM((tm, d), BF16)]
    if packed:
        groups = s5w // S5_GROUP
        half = GRID_W // 2
        lanes_per_tile = (tm // GRID_W) * S5_GROUP
        rows_total = tokens_per_mod // GRID_W
        n_b = t // tokens_per_mod
        u_spec = pl.BlockSpec((groups, half, lanes_per_tile), lambda i, j: (0, i // tiles_per_mod, i % tiles_per_mod))
        u_shape = jax.ShapeDtypeStruct((groups, n_b * half, rows_total * S5_GROUP), BF16)
        out_specs += [u_spec, u_spec]
        out_shape += [u_shape, u_shape]
        scratch.append(pltpu.VMEM((s5w // LANES, tm, LANES), F32))
    else:
        out_specs.append(pl.BlockSpec((tm, s5w), lambda i, j: (i, 0)))
        out_shape.append(jax.ShapeDtypeStruct((t, s5w), F32))
    return pl.pallas_call(
        functools.partial(_proj_kernel, sub=sub, packed=packed),
        grid=(t // tm, n_main // tn),
        in_specs=in_specs,
        out_specs=out_specs,
        out_shape=out_shape,
        scratch_shapes=scratch,
        compiler_params=_params(("parallel", "arbitrary"), 52),
        name="in_proj_s5" if packed else "in_proj",
    )(*args)


def _conv_kernel(p_ref, w_ref, o_ref, xp_scr, *, rows, width, strip, n_q_tiles, n_qk_tiles):
    length = rows * width
    pad = (xp_scr.shape[0] - length) // 2
    tc = p_ref.shape[2]
    j = pl.program_id(1)
    xp_scr[0:pad, :] = jnp.zeros((pad, tc), F32)
    xp_scr[pad + length:pad + length + pad, :] = jnp.zeros((pad, tc), F32)
    xp_scr[pad:pad + length, :] = p_ref[0]
    w = w_ref[...]
    q_scale = jnp.where(j < n_q_tiles, HEAD_DIM ** -0.5, 1.0).astype(F32)
    is_qk = j < n_qk_tiles
    row_taps = (-1, 0, 1) if rows > 1 else (0,)
    for s in range(length // strip):
        t0 = s * strip
        col = (lax.broadcasted_iota(jnp.int32, (strip, 1), 0) + t0) & (width - 1)
        acc = jnp.zeros((strip, tc), F32)
        for dr in row_taps:
            for dc in (-1, 0, 1):
                off = pad + t0 + dr * width + dc
                xs = xp_scr[off:off + strip, :]
                if dc == -1:
                    xs = jnp.where(col >= 1, xs, 0.0)
                elif dc == 1:
                    xs = jnp.where(col <= width - 2, xs, 0.0)
                tap = (dr + 1) * 3 + (dc + 1)
                acc = acc + xs * w[tap:tap + 1, :]
        y = _silu(acc)
        for hh in range(tc // HEAD_DIM):
            yh = y[:, hh * HEAD_DIM:(hh + 1) * HEAD_DIM]
            inv = lax.rsqrt(jnp.sum(yh * yh, axis=-1, keepdims=True) + EPS) * q_scale
            o_ref[0, t0:t0 + strip, hh * HEAD_DIM:(hh + 1) * HEAD_DIM] = yh * jnp.where(is_qk, inv, 1.0)


def _conv(p3, conv_w9, rows, width, n_conv, n_qk):
    b, length, _ = p3.shape
    tc = 512
    pad = width + 8 if rows > 1 else 8
    strip = min(256, length)
    kern = functools.partial(_conv_kernel, rows=rows, width=width, strip=strip,
                             n_q_tiles=(n_qk // 2) // tc, n_qk_tiles=n_qk // tc)
    return pl.pallas_call(
        kern,
        grid=(b, n_conv // tc),
        in_specs=[pl.BlockSpec((1, length, tc), lambda i, j: (i, 0, j)),
                  pl.BlockSpec((9, tc), lambda i, j: (0, j))],
        out_specs=pl.BlockSpec((1, length, tc), lambda i, j: (i, 0, j)),
        out_shape=jax.ShapeDtypeStruct((b, length, n_conv), F32),
        scratch_shapes=[pltpu.VMEM((length + 2 * pad, tc), F32)],
        compiler_params=_params(("parallel", "parallel"), 40),
        name="grid_conv",
    )(p3, conv_w9)


def _unit_tri_inverse(a_list, strict_list, row, col):
    n = 1
    e = None
    while n < CHUNK:
        k = n.bit_length() - 1
        pair = ((row >> (k + 1)) == (col >> (k + 1))) & ((row >> k) != (col >> k))
        l_n = [jnp.where(pair & st, a, 0.0) for a, st in zip(a_list, strict_list)]
        if e is None:
            e = [-l for l in l_n]
        else:
            e16 = [x.astype(BF16) for x in e]
            y = [l + _dot(l.astype(BF16), x16) for l, x16 in zip(l_n, e16)]
            e = [x - yy - _dot(x16, yy.astype(BF16)) for x, x16, yy in zip(e, e16, y)]
        n *= 2
    return e


def _delta_kernel(*refs, n_heads, with_init, with_output):
    chunk_refs = (refs[0:4], refs[4:8])
    alog_ref, dtb_ref = refs[8:10]
    rest = refs[10:]
    if with_init:
        s0_ref, rest = rest[0], rest[1:]
    if with_output:
        o_refs, s_scr = rest[0:2], rest[2]
    else:
        sfin_ref, s_scr = rest
    c = pl.program_id(1)

    @pl.when(c == 0)
    def _():
        if with_init:
            s_scr[...] = s0_ref[0]
        else:
            s_scr[...] = jnp.zeros_like(s_scr)

    row = lax.broadcasted_iota(jnp.int32, (CHUNK, CHUNK), 0)
    col = lax.broadcasted_iota(jnp.int32, (CHUNK, CHUNK), 1)
    incl_d = (row >= col, row <= col)
    strict_d = (row > col, row < col)

    chains = [(d, h) for d in range(2) for h in range(n_heads)]
    gc, gt, bt, decay, kh, vh, qh, strict = [], [], [], [], [], [], [], []
    for d in range(2):
        q_ref, k_ref, v_ref, gb_ref = chunk_refs[d]
        gb = gb_ref[0]
        x = gb + dtb_ref[...]
        softplus = jnp.maximum(x, 0.0) + jnp.log1p(jnp.exp(-jnp.abs(x)))
        g_all = -jnp.exp(alog_ref[...]) * softplus
        beta_all = _sigmoid(gb)
        gcum = jnp.dot(incl_d[d].astype(F32), g_all, precision=lax.Precision.HIGHEST,
                       preferred_element_type=F32)
        gtot = jnp.sum(g_all, axis=0, keepdims=True)
        gcum_t = jnp.concatenate([gcum, jnp.zeros_like(gcum)], axis=0).T
        for h in range(n_heads):
            sl = slice(h * HEAD_DIM, (h + 1) * HEAD_DIM)
            lane = d * n_heads + h
            gc.append(gcum[:, lane:lane + 1])
            gt.append(gtot[:, lane:lane + 1])
            bt.append(beta_all[:, 2 * n_heads + lane:2 * n_heads + lane + 1])
            decay.append(jnp.where(incl_d[d], jnp.exp(jnp.where(incl_d[d], gc[-1] - gcum_t[lane:lane + 1, 0:CHUNK], 0.0)), 0.0))
            kh.append(k_ref[0, :, sl])
            vh.append(v_ref[0, :, sl])
            if with_output:
                qh.append(q_ref[0, :, sl])
            strict.append(strict_d[d])
    n = range(len(chains))
    k16 = [x.astype(BF16) for x in kh]
    if with_output:
        kq = [_dot_nt(jnp.concatenate([k16[i], qh[i].astype(BF16)], axis=0), k16[i]) for i in n]
        kk = [x[:CHUNK] for x in kq]
        qk = [x[CHUNK:] for x in kq]
    else:
        kk = [_dot_nt(x, x) for x in k16]
    a_mat = [jnp.where(strict[i], bt[i] * decay[i] * kk[i], 0.0) for i in n]
    e = _unit_tri_inverse(a_mat, strict, row, col)
    eg = [jnp.exp(x) for x in gc]
    rhs = [jnp.concatenate([(bt[i] * eg[i]) * kh[i], bt[i] * vh[i]], axis=1) for i in n]
    sol = [rhs[i] + _dot(e[i].astype(BF16), rhs[i].astype(BF16)) for i in n]
    s_h = [s_scr[d, h] for d, h in chains]
    s16 = [x.astype(BF16) for x in s_h]
    if with_output:
        wq = [_dot(jnp.concatenate([sol[i][:, :HEAD_DIM].astype(BF16), (qh[i] * eg[i]).astype(BF16)], axis=0), s16[i])
              for i in n]
        ws = [x[:CHUNK] for x in wq]
        qs = [x[CHUNK:] for x in wq]
    else:
        ws = [_dot(sol[i][:, :HEAD_DIM].astype(BF16), s16[i]) for i in n]
    u16 = [(sol[i][:, HEAD_DIM:] - ws[i]).astype(BF16) for i in n]
    if with_output:
        for i, (d, h) in enumerate(chains):
            o_refs[d][0, :, h * HEAD_DIM:(h + 1) * HEAD_DIM] = qs[i] + _dot((qk[i] * decay[i]).astype(BF16), u16[i])
    for i, (d, h) in enumerate(chains):
        k_dec = kh[i] * jnp.exp(gt[i] - gc[i])
        s_scr[d, h] = jnp.exp(gt[i]) * s_h[i] + _dot_tn(k_dec.astype(BF16), u16[i])

    if not with_output:
        @pl.when(c == pl.num_programs(1) - 1)
        def _():
            sfin_ref[0] = s_scr[...]


def _delta(qkv, p3, gate_block, alog_r, dtb_r, n_heads, s0=None):
    b, length, _ = qkv.shape
    nch = length // CHUNK
    hd = n_heads * HEAD_DIM
    with_init = s0 is not None
    with_output = with_init

    in_specs, args = [], []
    for d in range(2):
        pos = (lambda c: c) if d == 0 else (lambda c: nch - 1 - c)
        in_specs += [pl.BlockSpec((1, CHUNK, hd), lambda i, c, pos=pos: (i, pos(c), 0)),
                     pl.BlockSpec((1, CHUNK, hd), lambda i, c, pos=pos: (i, pos(c), 1)),
                     pl.BlockSpec((1, CHUNK, hd), lambda i, c, pos=pos: (i, pos(c), 2)),
                     pl.BlockSpec((1, CHUNK, LANES), lambda i, c, pos=pos: (i, pos(c), gate_block))]
        args += [qkv, qkv, qkv, p3]
    in_specs += [pl.BlockSpec((1, LANES), lambda i, c: (0, 0))] * 2
    args += [alog_r, dtb_r]
    state_spec = pl.BlockSpec((1, 2, n_heads, HEAD_DIM, HEAD_DIM), lambda i, c: (i, 0, 0, 0, 0))
    if with_init:
        in_specs.append(state_spec)
        args.append(s0)
    if with_output:
        out_specs = [pl.BlockSpec((1, CHUNK, hd), lambda i, c: (i, c, 0)),
                     pl.BlockSpec((1, CHUNK, hd), lambda i, c: (i, nch - 1 - c, 0))]
        out_shape = [jax.ShapeDtypeStruct((b, length, hd), F32)] * 2
    else:
        out_specs = state_spec
        out_shape = jax.ShapeDtypeStruct((b, 2, n_heads, HEAD_DIM, HEAD_DIM), F32)
    return pl.pallas_call(
        functools.partial(_delta_kernel, n_heads=n_heads, with_init=with_init, with_output=with_output),
        grid=(b, nch),
        in_specs=in_specs,
        out_specs=out_specs,
        out_shape=out_shape,
        scratch_shapes=[pltpu.VMEM((2, n_heads, HEAD_DIM, HEAD_DIM), F32)],
        compiler_params=_params(("parallel", "arbitrary"), 40),
        name="delta_lat" if with_output else "delta_ctx",
    )(*args)


def _cmul(ar, ai, br, bi):
    return ar * br - ai * bi, ar * bi + ai * br


def _s5_tables(a_re, a_im, log_dt, b_re, b_im, c_re, c_im):
    g, p, s = b_re.shape
    dt = jnp.exp(log_dt.astype(F32))[..., None]
    lr, li = a_re.astype(F32) * dt, a_im.astype(F32) * dt

    def powers(d, n):
        n = n.astype(F32)
        mag = jnp.exp(lr[d][..., None] * n)
        ang = li[d][..., None] * n
        return mag * jnp.cos(ang), mag * jnp.sin(ang)

    abr = jnp.exp(lr) * jnp.cos(li)
    abi = jnp.exp(lr) * jnp.sin(li)
    nr, ni = abr - 1.0, abi
    cr, ci = a_re.astype(F32), a_im.astype(F32)
    den = cr * cr + ci * ci
    fr, fi = (nr * cr + ni * ci) / den, (ni * cr - nr * ci) / den
    bbr, bbi = _cmul(fr[..., None], fi[..., None], b_re.astype(F32)[None], b_im.astype(F32)[None])
    ccr = jnp.swapaxes(c_re.astype(F32), 1, 2)
    cci = jnp.swapaxes(c_im.astype(F32), 1, 2)
    sub = jnp.arange(S5_SUB)

    def x_table(d, n):
        er, ei = powers(d, n)
        xr, xi = _cmul(bbr[d][:, :, None, :], bbi[d][:, :, None, :], er[..., None], ei[..., None])
        xr = jnp.transpose(xr, (0, 2, 3, 1)).reshape(g, S5_SUB * s, p)
        xi = jnp.transpose(xi, (0, 2, 3, 1)).reshape(g, S5_SUB * s, p)
        return jnp.concatenate([xr, xi], axis=-1)

    def y_table(d, n):
        er, ei = powers(d, n)
        yr, yi = _cmul(ccr[:, :, None, :], cci[:, :, None, :], er[..., None], ei[..., None])
        return jnp.concatenate([yr.reshape(g, p, S5_SUB * s), -yi.reshape(g, p, S5_SUB * s)], axis=1)

    xt = jnp.stack([x_table(0, S5_SUB - 1 - sub), x_table(1, sub)], axis=1)
    yt = jnp.stack([y_table(0, sub + 1), y_table(0, sub - (S5_SUB - 1)), y_table(1, -sub), y_table(1, S5_SUB - sub)], axis=1)
    rows = []
    for d in range(2):
        er, ei = powers(d, S5_SUB * jnp.arange(CHUNK // S5_SUB + 1))
        for k in range(CHUNK // S5_SUB + 1):
            rows += [jnp.concatenate([er[..., k], er[..., k]], -1), jnp.concatenate([-ei[..., k], ei[..., k]], -1)]
    pw = jnp.stack(rows, axis=1)
    return xt, yt, pw


def _s5_kernel(uc_ref, u0_ref, u1_ref, xt_ref, yt_ref, pw_ref, dsk_ref, y_ref, bst_scr, m_scr, h_scr, y_scr,
               *, nb, grid_rows):
    nsub = CHUNK // S5_SUB
    n_pw = 2 * (nsub + 1)
    n_lat = u0_ref.shape[1] // nb
    n_ctx = uc_ref.shape[1] // nb
    row = lax.broadcasted_iota(jnp.int32, (LANES, LANES), 0)
    col = lax.broadcasted_iota(jnp.int32, (LANES, LANES), 1)
    jj, ii = row >> 4, col >> 4

    def hdot(a, b):
        return jnp.dot(a, b, precision=lax.Precision.HIGHEST, preferred_element_type=F32)

    def one_group(gi, carry):
        pw = pw_ref[gi]

        def crot(x, d, k, pw=pw):
            r0 = d * n_pw + 2 * k
            return pw[r0:r0 + 1] * x + pw[r0 + 1:r0 + 2] * pltpu.roll(x, LANES // 2, axis=1)

        xf, xb = xt_ref[gi, 0], xt_ref[gi, 1]
        xfs = [xf] + [crot(xf, 0, k) for k in range(1, nsub)]
        xbs = [xb] + [crot(xb, 1, k) for k in range(1, nsub)]
        yf1, yf0, yb0, yb1 = yt_ref[gi, 0], yt_ref[gi, 1], yt_ref[gi, 2], yt_ref[gi, 3]

        for j in range(nsub):
            bst_scr[j * LANES:(j + 1) * LANES, 0:LANES] = xfs[nsub - 1 - j].astype(BF16)
            bst_scr[j * LANES:(j + 1) * LANES, LANES:2 * LANES] = xbs[j].astype(BF16)
        bst = bst_scr[...]
        ul = jnp.concatenate([u0_ref[gi], u1_ref[gi]], axis=1)
        hc = _dot(uc_ref[gi], bst)
        hl = _dot(ul, bst)
        h_scr[0] = hl[:, 0:LANES]
        h_scr[1] = hl[:, LANES:2 * LANES]

        hf = jnp.zeros((nb, LANES), F32)
        hb = jnp.zeros((nb, LANES), F32)
        for c in range(n_ctx):
            hf = crot(hf, 0, nsub) + hc[c * nb:(c + 1) * nb, 0:LANES]
        for c in reversed(range(n_ctx)):
            hb = crot(hb, 1, nsub) + hc[c * nb:(c + 1) * nb, LANES:2 * LANES]
        for c in range(n_lat):
            rows_c = pl.ds(c, nb, stride=n_lat)
            inp = h_scr[0, rows_c, :]
            h_scr[0, rows_c, :] = hf
            hf = crot(hf, 0, nsub) + inp
        for c in reversed(range(n_lat)):
            rows_c = pl.ds(c, nb, stride=n_lat)
            inp = h_scr[1, rows_c, :]
            h_scr[1, rows_c, :] = hb
            hb = crot(hb, 1, nsub) + inp

        diag = (jnp.where(ii >= jj, hdot(xf, yf0), 0.0) + jnp.where(jj >= ii, hdot(xb, yb0), 0.0)
                + jnp.where(row == col, dsk_ref[gi], 0.0)).astype(BF16)
        taps_f = [None] + [hdot(xfs[dl - 1], yf1).astype(BF16) for dl in range(1, nsub)]
        taps_b = [None] + [hdot(xbs[dl], yb0).astype(BF16) for dl in range(1, nsub)]
        for bj in range(nsub):
            for bi in range(nsub):
                blk = diag if bi == bj else (taps_f[bi - bj] if bi > bj else taps_b[bj - bi])
                m_scr[bj * LANES:(bj + 1) * LANES, bi * LANES:(bi + 1) * LANES] = blk
        y = _dot(ul, m_scr[...])
        hin_f, hin_b = h_scr[0], h_scr[1]
        yf16, yb16 = yf1.astype(BF16), yb1.astype(BF16)
        for bi in range(nsub):
            y_scr[gi, :, bi * LANES:(bi + 1) * LANES] = (
                y[:, bi * LANES:(bi + 1) * LANES]
                + _dot(crot(hin_f, 0, bi).astype(BF16), yf16)
                + _dot(crot(hin_b, 1, nsub - 1 - bi).astype(BF16), yb16))
        return carry

    lax.fori_loop(0, u0_ref.shape[0], one_group, 0)

    half = GRID_W // 2
    tokens = grid_rows * GRID_W
    for cc in range(2):
        for r in range(grid_rows):
            lane0 = (cc * grid_rows + r) * S5_GROUP
            piece = jnp.concatenate([y_scr[gi, :, lane0:lane0 + S5_GROUP] for gi in range(u0_ref.shape[0])], axis=1)
            for b in range(nb):
                y_ref[pl.ds(b * tokens + r * GRID_W + cc, half, stride=2), :] = piece[b * half:(b + 1) * half]


def _s5(uc, u0, u1, tables, d_skip, nb, grid_rows):
    xt, yt, pw = tables
    g = u0.shape[0]
    rows_l = u0.shape[1]
    kdim = 2 * u0.shape[2]
    tokens = grid_rows * GRID_W
    gstep = LANES // S5_GROUP
    dsk = jnp.tile(d_skip.astype(F32).reshape(g, 1, S5_GROUP), (1, 1, S5_SUB))

    def spec(a):
        nd = a.ndim
        return pl.BlockSpec((gstep,) + tuple(a.shape[1:]), lambda i: (i,) + (0,) * (nd - 1))

    arrays = [uc, u0, u1, xt, yt, pw, dsk]
    return pl.pallas_call(
        functools.partial(_s5_kernel, nb=nb, grid_rows=grid_rows),
        grid=(g // gstep,),
        in_specs=[spec(a) for a in arrays],
        out_specs=pl.BlockSpec((nb * tokens, LANES), lambda i: (0, i)),
        out_shape=jax.ShapeDtypeStruct((nb * tokens, g * S5_GROUP), F32),
        scratch_shapes=[pltpu.VMEM((kdim, 2 * LANES), BF16), pltpu.VMEM((kdim, kdim), BF16),
                        pltpu.VMEM((2, rows_l, LANES), F32), pltpu.VMEM((gstep, rows_l, kdim), F32)],
        compiler_params=_params(("parallel",), 48),
        name="s5_scan",
    )(*arrays)


def _gelu_tanh(x):
    return 0.5 * x * (1.0 + jnp.tanh(0.7978845608028654 * (x + 0.044715 * (x * x * x))))


def _out_kernel(of_ref, ob_ref, z_ref, y_ref, x_ref, m_ref, dnw_ref, wglu_ref, wdn_ref, ws5_ref, out_ref, *, n_heads):
    o = of_ref[...] + ob_ref[...]
    z = z_ref[...]
    dnw = dnw_ref[...]
    parts = []
    for h in range(n_heads):
        sl = slice(h * HEAD_DIM, (h + 1) * HEAD_DIM)
        oh = o[:, sl]
        yh = oh * lax.rsqrt(jnp.mean(oh * oh, axis=-1, keepdims=True) + EPS) * dnw
        parts.append((yh * _silu(z[:, sl])).astype(BF16))
    dn = jnp.concatenate(parts, axis=1)
    t = _dot(_gelu_tanh(y_ref[...]).astype(BF16), wglu_ref[...])
    half = t.shape[1] // 2
    s5 = (t[:, :half] * _sigmoid(t[:, half:])).astype(BF16)
    acc = _dot(dn, wdn_ref[...]) + _dot(s5, ws5_ref[...])
    out_ref[...] = x_ref[...] + m_ref[0][5:6] * acc


def _out_proj(o_fwd, o_bwd, p2, z_block, y2, x2, mods, dn_norm, w_glu, w_out, n_heads, tokens_per_mod):
    t, d = x2.shape
    hd = n_heads * HEAD_DIM
    s5w = y2.shape[1]
    tm = 256
    tiles_per_mod = tokens_per_mod // tm
    return pl.pallas_call(
        functools.partial(_out_kernel, n_heads=n_heads),
        grid=(t // tm,),
        in_specs=[pl.BlockSpec((tm, hd), lambda i: (i, 0)),
                  pl.BlockSpec((tm, hd), lambda i: (i, 0)),
                  pl.BlockSpec((tm, hd), lambda i: (i, z_block)),
                  pl.BlockSpec((tm, s5w), lambda i: (i, 0)),
                  pl.BlockSpec((tm, d), lambda i: (i, 0)),
                  pl.BlockSpec((1, N_MOD, d), lambda i: (i // tiles_per_mod, 0, 0)),
                  pl.BlockSpec((1, HEAD_DIM), lambda i: (0, 0)),
                  pl.BlockSpec(w_glu.shape, lambda i: (0, 0)),
                  pl.BlockSpec((hd, d), lambda i: (0, 0)),
                  pl.BlockSpec((s5w, d), lambda i: (hd // s5w, 0))],
        out_specs=pl.BlockSpec((tm, d), lambda i: (i, 0)),
        out_shape=jax.ShapeDtypeStruct((t, d), F32),
        compiler_params=_params(("parallel",), 48),
        name="out_proj",
    )(o_fwd, o_bwd, p2, y2, x2, mods, dn_norm.reshape(1, HEAD_DIM), w_glu, w_out, w_out)


def _layer(x, ctx, m_lat, m_ctx, norm_ffn1, ffn1_up, ffn1_down, norm_mix, w_in, dn_conv, dn_a_log,
           dn_dt_bias, dn_norm, s5_a_re, s5_a_im, s5_log_dt, s5_b_re, s5_b_im, s5_c_re, s5_c_im, s5_d,
           s5_glu, w_out, norm_ffn2, ffn2_up, ffn2_down, final_norm):
    b, length, d = x.shape
    lc = ctx.shape[1]
    n_heads = dn_a_log.shape[1]
    hd = n_heads * HEAD_DIM
    n_conv = dn_conv.shape[-1]
    n_qk = n_conv - hd
    s5w = s5_d.shape[0]
    groups = s5w // S5_GROUP
    rows = length // GRID_W

    gate0 = n_conv + hd
    w_in16 = w_in.astype(BF16)
    w_s5 = w_in16[:, gate0 + 4 * n_heads:]
    n_main = gate0 + LANES
    z_block = n_conv // hd
    gate_block = gate0 // LANES
    lane_pad = jnp.zeros((LANES - 2 * n_heads,), F32)
    alog_r = jnp.concatenate([dn_a_log.astype(F32).reshape(-1), lane_pad]).reshape(1, LANES)
    dtb_r = jnp.concatenate([dn_dt_bias.astype(F32).reshape(-1), lane_pad]).reshape(1, LANES)
    conv_w9 = dn_conv.reshape(9, n_conv)

    x2 = x.reshape(b * length, d)
    c2 = ctx.reshape(b * lc, d)

    up1, down1 = ffn1_up.astype(BF16), ffn1_down.astype(BF16)
    x2 = _ffn(x2, m_lat, norm_ffn1, up1, down1, 0, length)
    c2 = _ffn(c2, m_ctx, norm_ffn1, up1, down1, 0, b * lc)

    p_lat, u_lat0, u_lat1 = _proj(x2, m_lat, norm_mix, w_in16, n_main, w_s5, 1, length, packed=True)
    p_ctx, u_ctx = _proj(c2, m_ctx, norm_mix, w_in16, n_main, w_s5, 1, b * lc, packed=False)
    p_lat3 = p_lat.reshape(b, length, n_main)
    p_ctx3 = p_ctx.reshape(b, lc, n_main)

    qkv_lat = _conv(p_lat3, conv_w9, rows, GRID_W, n_conv, n_qk)
    qkv_ctx = _conv(p_ctx3, conv_w9, 1, lc, n_conv, n_qk)
    s_ctx = _delta(qkv_ctx, p_ctx3, gate_block, alog_r, dtb_r, n_heads)
    o_fwd, o_bwd = _delta(qkv_lat, p_lat3, gate_block, alog_r, dtb_r, n_heads, s0=s_ctx)

    tables = _s5_tables(s5_a_re, s5_a_im, s5_log_dt, s5_b_re, s5_b_im, s5_c_re, s5_c_im)
    u_ctx = u_ctx.astype(BF16).reshape(b, lc // CHUNK, CHUNK, groups, S5_GROUP)
    u_ctx = jnp.transpose(u_ctx, (3, 1, 0, 2, 4)).reshape(groups, (lc // CHUNK) * b, CHUNK * S5_GROUP)
    y2 = _s5(u_ctx, u_lat0, u_lat1, tables, s5_d, b, rows)

    x2 = _out_proj(o_fwd.reshape(b * length, hd), o_bwd.reshape(b * length, hd), p_lat, z_block, y2, x2, m_lat, dn_norm,
                   s5_glu.astype(BF16), w_out.astype(BF16), n_heads, length)

    x2 = _ffn(x2, m_lat, norm_ffn2, ffn2_up.astype(BF16), ffn2_down.astype(BF16), 2, length,
              final_w=final_norm)
    return x2.reshape(b, length, d)


def kernel(x, c, ctx, c_ctx, w_mod, b_mod, norm_ffn1, ffn1_up, ffn1_down, norm_mix, w_in, dn_conv, dn_a_log, dn_dt_bias, dn_norm, s5_a_re, s5_a_im, s5_log_dt, s5_b_re, s5_b_im, s5_c_re, s5_c_im, s5_d, s5_glu, w_out, norm_ffn2, ffn2_up, ffn2_down, final_norm):
    depth = w_mod.shape[0]
    assert depth == 1, "the context stream update of deeper stacks is not implemented"
    b, _, d = x.shape
    cond = jnp.concatenate([c, c_ctx[None], jnp.zeros((16 - b - 1, d), c.dtype)], axis=0)
    m = _ada(cond, w_mod[0], b_mod[0]).reshape(16, N_MOD, d)
    return _layer(x, ctx, m[:b], m[b:b + 1], norm_ffn1[0], ffn1_up[0], ffn1_down[0], norm_mix[0], w_in[0],
                  dn_conv[0], dn_a_log[0], dn_dt_bias[0], dn_norm[0], s5_a_re[0], s5_a_im[0], s5_log_dt[0],
                  s5_b_re[0], s5_b_im[0], s5_c_re[0], s5_c_im[0], s5_d[0], s5_glu[0], w_out[0],
                  norm_ffn2[0], ffn2_up[0], ffn2_down[0], final_norm)
```

```python
import functools

import jax
import jax.numpy as jnp
from jax import lax
from jax.experimental import pallas as pl
from jax.experimental.pallas import tpu as pltpu

F32 = jnp.float32
BF16 = jnp.bfloat16
EPS = 1e-6
N_MOD = 9
GRID_W = 64
CHUNK = 64
HEAD_DIM = 128
S5_GROUP = 16
S5_SUB = 8
LANES = 128
MIB = 1024 * 1024


def _params(semantics, vmem_mib):
    return pltpu.CompilerParams(dimension_semantics=semantics, vmem_limit_bytes=vmem_mib * MIB)


def _sigmoid(x):
    return 1.0 / (1.0 + jnp.exp(-x))


def _silu(x):
    return x * _sigmoid(x)


def _dot(a, b):
    return jnp.dot(a, b, preferred_element_type=F32)


def _dot_nt(a, b):
    return lax.dot_general(a, b, (((1,), (1,)), ((), ())), preferred_element_type=F32)


def _dot_tn(a, b):
    return lax.dot_general(a, b, (((0,), (0,)), ((), ())), preferred_element_type=F32)


def _modnorm(x, norm_w, scale, shift):
    y = x * lax.rsqrt(jnp.mean(x * x, axis=-1, keepdims=True) + EPS) * norm_w
    return y * (1.0 + scale) + shift


def _ada_kernel(c_ref, w_ref, b_ref, o_ref):
    a = _silu(c_ref[...]).astype(BF16)
    o_ref[...] = _dot(a, w_ref[...].astype(BF16)) + b_ref[...]


def _ada(cond, w_mod, b_mod):
    rows, d = cond.shape
    n = w_mod.shape[1]
    tn = 1024
    return pl.pallas_call(
        _ada_kernel,
        grid=(n // tn,),
        in_specs=[pl.BlockSpec((rows, d), lambda j: (0, 0)),
                  pl.BlockSpec((d, tn), lambda j: (0, j)),
                  pl.BlockSpec((1, tn), lambda j: (0, j))],
        out_specs=pl.BlockSpec((rows, tn), lambda j: (0, j)),
        out_shape=jax.ShapeDtypeStruct((rows, n), F32),
        compiler_params=_params(("parallel",), 40),
        name="ada_mod",
    )(cond, w_mod, b_mod.reshape(1, n))


def _ffn_kernel(x_ref, m_ref, nw_ref, wg_ref, wu_ref, wd_ref, *rest, sub, final):
    if final:
        fin_ref, o_ref, h_scr, acc_scr = rest
    else:
        o_ref, h_scr, acc_scr = rest
    f = pl.program_id(1)

    @pl.when(f == 0)
    def _():
        m = m_ref[0]
        h = _modnorm(x_ref[...], nw_ref[...], m[3 * sub + 1:3 * sub + 2], m[3 * sub:3 * sub + 1])
        h_scr[...] = h.astype(BF16)
        acc_scr[...] = jnp.zeros_like(acc_scr)

    h = h_scr[...]
    g = _dot(h, wg_ref[...])
    u = _dot(h, wu_ref[...])
    acc_scr[...] += _dot((_silu(g) * u).astype(BF16), wd_ref[...])

    @pl.when(f == pl.num_programs(1) - 1)
    def _():
        m = m_ref[0]
        y = x_ref[...] + 0.5 * m[3 * sub + 2:3 * sub + 3] * acc_scr[...]
        if final:
            y = y * lax.rsqrt(jnp.mean(y * y, axis=-1, keepdims=True) + EPS) * fin_ref[...]
        o_ref[...] = y


def _ffn(x2, mods, norm_w, w_up, w_down, sub, tokens_per_mod, final_w=None):
    t, d = x2.shape
    f = w_down.shape[0]
    tm, tf = 512, 512
    nf = f // tf
    tiles_per_mod = tokens_per_mod // tm
    final = final_w is not None
    in_specs = [pl.BlockSpec((tm, d), lambda i, j: (i, 0)),
                pl.BlockSpec((1, N_MOD, d), lambda i, j: (i // tiles_per_mod, 0, 0)),
                pl.BlockSpec((1, d), lambda i, j: (0, 0)),
                pl.BlockSpec((d, tf), lambda i, j: (0, j)),
                pl.BlockSpec((d, tf), lambda i, j: (0, nf + j)),
                pl.BlockSpec((tf, d), lambda i, j: (j, 0))]
    args = [x2, mods, norm_w.reshape(1, d), w_up, w_up, w_down]
    if final:
        in_specs.append(pl.BlockSpec((1, d), lambda i, j: (0, 0)))
        args.append(final_w.reshape(1, d))
    return pl.pallas_call(
        functools.partial(_ffn_kernel, sub=sub, final=final),
        grid=(t // tm, nf),
        in_specs=in_specs,
        out_specs=pl.BlockSpec((tm, d), lambda i, j: (i, 0)),
        out_shape=jax.ShapeDtypeStruct((t, d), F32),
        scratch_shapes=[pltpu.VMEM((tm, d), BF16), pltpu.VMEM((tm, d), F32)],
        compiler_params=_params(("parallel", "arbitrary"), 48),
        name="ffn_final" if final else "ffn",
    )(*args)


def _proj_kernel(x_ref, m_ref, nw_ref, w_ref, ws5_ref, *rest, sub, packed):
    if packed:
        o_ref, u0_ref, u1_ref, h_scr, a_scr = rest
    else:
        o_ref, s5_ref, h_scr = rest

    @pl.when(pl.program_id(1) == 0)
    def _():
        m = m_ref[0]
        h = _modnorm(x_ref[...], nw_ref[...], m[3 * sub + 1:3 * sub + 2], m[3 * sub:3 * sub + 1])
        h_scr[...] = h.astype(BF16)
        s5 = _dot(h_scr[...], ws5_ref[...])
        if packed:
            _s5_pack(s5, a_scr, (u0_ref, u1_ref))
        else:
            s5_ref[...] = s5

    o_ref[...] = _dot(h_scr[...], w_ref[...])


def _s5_pack(a, a_scr, u_refs):
    tm, width = a.shape
    half = GRID_W // 2
    n_rl = tm // GRID_W
    for k in range(width // LANES):
        a_scr[k] = a[:, k * LANES:(k + 1) * LANES]
    for cc in range(2):
        acc = jnp.concatenate([a_scr[k, pl.ds(cc, tm // 2, stride=2), :] for k in range(width // LANES)], axis=1)
        for g in range(width // S5_GROUP):
            piece = jnp.concatenate([acc[rl * half:(rl + 1) * half, g * S5_GROUP:(g + 1) * S5_GROUP]
                                     for rl in range(n_rl)], axis=1)
            u_refs[cc][g] = piece.astype(BF16)


def _proj(x2, mods, norm_w, w, n_main, w_s5, sub, tokens_per_mod, packed):
    t, d = x2.shape
    tm, tn = 1024, 896
    assert n_main % tn == 0
    tiles_per_mod = tokens_per_mod // tm
    s5w = w_s5.shape[1]
    in_specs = [pl.BlockSpec((tm, d), lambda i, j: (i, 0)),
                pl.BlockSpec((1, N_MOD, d), lambda i, j: (i // tiles_per_mod, 0, 0)),
                pl.BlockSpec((1, d), lambda i, j: (0, 0)),
                pl.BlockSpec((d, tn), lambda i, j: (0, j)),
                pl.BlockSpec((d, s5w), lambda i, j: (0, 0))]
    args = [x2, mods, norm_w.reshape(1, d), w, w_s5]
    out_specs = [pl.BlockSpec((tm, tn), lambda i, j: (i, j))]
    out_shape = [jax.ShapeDtypeStruct((t, n_main), F32)]
    scratch = [pltpu.VMEM((tm, d), BF16)]
    if packed:
        groups = s5w // S5_GROUP
        half = GRID_W // 2
        lanes_per_tile = (tm // GRID_W) * S5_GROUP
        rows_total = tokens_per_mod // GRID_W
        n_b = t // tokens_per_mod
        u_spec = pl.BlockSpec((groups, half, lanes_per_tile), lambda i, j: (0, i // tiles_per_mod, i % tiles_per_mod))
        u_shape = jax.ShapeDtypeStruct((groups, n_b * half, rows_total * S5_GROUP), BF16)
        out_specs += [u_spec, u_spec]
        out_shape += [u_shape, u_shape]
        scratch.append(pltpu.VMEM((s5w // LANES, tm, LANES), F32))
    else:
        out_specs.append(pl.BlockSpec((tm, s5w), lambda i, j: (i, 0)))
        out_shape.append(jax.ShapeDtypeStruct((t, s5w), F32))
    return pl.pallas_call(
        functools.partial(_proj_kernel, sub=sub, packed=packed),
        grid=(t // tm, n_main // tn),
        in_specs=in_specs,
        out_specs=out_specs,
        out_shape=out_shape,
        scratch_shapes=scratch,
        compiler_params=_params(("parallel", "arbitrary"), 52),
        name="in_proj_s5" if packed else "in_proj",
    )(*args)


def _conv_kernel(p_ref, w_ref, o_ref, xp_scr, *, rows, width, strip, n_q_tiles, n_qk_tiles):
    length = rows * width
    pad = (xp_scr.shape[0] - length) // 2
    tc = p_ref.shape[2]
    j = pl.program_id(1)
    xp_scr[0:pad, :] = jnp.zeros((pad, tc), F32)
    xp_scr[pad + length:pad + length + pad, :] = jnp.zeros((pad, tc), F32)
    xp_scr[pad:pad + length, :] = p_ref[0]
    w = w_ref[...]
    q_scale = jnp.where(j < n_q_tiles, HEAD_DIM ** -0.5, 1.0).astype(F32)
    is_qk = j < n_qk_tiles
    row_taps = (-1, 0, 1) if rows > 1 else (0,)
    halo = 8
    for s in range(length // strip):
        t0 = s * strip
        col = (lax.broadcasted_iota(jnp.int32, (strip, 1), 0) + t0) & (width - 1)
        base = pad + t0 - halo
        acc = jnp.zeros((strip, tc), F32)
        for dc in (-1, 0, 1):
            z = jnp.zeros((strip + 2 * halo, tc), F32)
            for dr in row_taps:
                tap = (dr + 1) * 3 + (dc + 1)
                z = z + xp_scr[base + dr * width:base + dr * width + strip + 2 * halo, :] * w[tap:tap + 1, :]
            z = z[halo + dc:halo + dc + strip, :]
            if dc == -1:
                z = jnp.where(col >= 1, z, 0.0)
            elif dc == 1:
                z = jnp.where(col <= width - 2, z, 0.0)
            acc = acc + z
        y = _silu(acc)
        for hh in range(tc // HEAD_DIM):
            yh = y[:, hh * HEAD_DIM:(hh + 1) * HEAD_DIM]
            inv = lax.rsqrt(jnp.sum(yh * yh, axis=-1, keepdims=True) + EPS) * q_scale
            o_ref[0, t0:t0 + strip, hh * HEAD_DIM:(hh + 1) * HEAD_DIM] = yh * jnp.where(is_qk, inv, 1.0)


def _conv(p3, conv_w9, rows, width, n_conv, n_qk):
    b, length, _ = p3.shape
    tc = 512
    pad = width + 8 if rows > 1 else 8
    strip = min(256, length)
    kern = functools.partial(_conv_kernel, rows=rows, width=width, strip=strip,
                             n_q_tiles=(n_qk // 2) // tc, n_qk_tiles=n_qk // tc)
    return pl.pallas_call(
        kern,
        grid=(b, n_conv // tc),
        in_specs=[pl.BlockSpec((1, length, tc), lambda i, j: (i, 0, j)),
                  pl.BlockSpec((9, tc), lambda i, j: (0, j))],
        out_specs=pl.BlockSpec((1, length, tc), lambda i, j: (i, 0, j)),
        out_shape=jax.ShapeDtypeStruct((b, length, n_conv), F32),
        scratch_shapes=[pltpu.VMEM((length + 2 * pad, tc), F32)],
        compiler_params=_params(("parallel", "parallel"), 40),
        name="grid_conv",
    )(p3, conv_w9)


def _unit_tri_inverse(a_list, strict_list, row, col):
    n = 1
    e = None
    while n < CHUNK:
        k = n.bit_length() - 1
        pair = ((row >> (k + 1)) == (col >> (k + 1))) & ((row >> k) != (col >> k))
        l_n = [jnp.where(pair & st, a, 0.0) for a, st in zip(a_list, strict_list)]
        if e is None:
            e = [-l for l in l_n]
        else:
            e16 = [x.astype(BF16) for x in e]
            y = [l + _dot(l.astype(BF16), x16) for l, x16 in zip(l_n, e16)]
            e = [x - yy - _dot(x16, yy.astype(BF16)) for x, x16, yy in zip(e, e16, y)]
        n *= 2
    return e


def _delta_kernel(*refs, n_heads, with_init, with_output):
    chunk_refs = (refs[0:4], refs[4:8])
    alog_ref, dtb_ref = refs[8:10]
    rest = refs[10:]
    if with_init:
        s0_ref, rest = rest[0], rest[1:]
    if with_output:
        o_refs, s_scr = rest[0:2], rest[2]
    else:
        sfin_ref, s_scr = rest
    c = pl.program_id(1)

    @pl.when(c == 0)
    def _():
        if with_init:
            s_scr[...] = s0_ref[0]
        else:
            s_scr[...] = jnp.zeros_like(s_scr)

    row = lax.broadcasted_iota(jnp.int32, (CHUNK, CHUNK), 0)
    col = lax.broadcasted_iota(jnp.int32, (CHUNK, CHUNK), 1)
    incl_d = (row >= col, row <= col)
    strict_d = (row > col, row < col)

    chains = [(d, h) for d in range(2) for h in range(n_heads)]
    gc, gt, bt, decay, kh, vh, qh, strict = [], [], [], [], [], [], [], []
    for d in range(2):
        q_ref, k_ref, v_ref, gb_ref = chunk_refs[d]
        gb = gb_ref[0]
        x = gb + dtb_ref[...]
        softplus = jnp.maximum(x, 0.0) + jnp.log1p(jnp.exp(-jnp.abs(x)))
        g_all = -jnp.exp(alog_ref[...]) * softplus
        beta_all = _sigmoid(gb)
        gcum = jnp.dot(incl_d[d].astype(F32), g_all, precision=lax.Precision.HIGHEST,
                       preferred_element_type=F32)
        gtot = jnp.sum(g_all, axis=0, keepdims=True)
        gcum_t = jnp.concatenate([gcum, jnp.zeros_like(gcum)], axis=0).T
        for h in range(n_heads):
            sl = slice(h * HEAD_DIM, (h + 1) * HEAD_DIM)
            lane = d * n_heads + h
            gc.append(gcum[:, lane:lane + 1])
            gt.append(gtot[:, lane:lane + 1])
            bt.append(beta_all[:, 2 * n_heads + lane:2 * n_heads + lane + 1])
            decay.append(jnp.where(incl_d[d], jnp.exp(jnp.where(incl_d[d], gc[-1] - gcum_t[lane:lane + 1, 0:CHUNK], 0.0)), 0.0))
            kh.append(k_ref[0, :, sl])
            vh.append(v_ref[0, :, sl])
            if with_output:
                qh.append(q_ref[0, :, sl])
            strict.append(strict_d[d])
    n = range(len(chains))
    k16 = [x.astype(BF16) for x in kh]
    if with_output:
        kq = [_dot_nt(jnp.concatenate([k16[i], qh[i].astype(BF16)], axis=0), k16[i]) for i in n]
        kk = [x[:CHUNK] for x in kq]
        qk = [x[CHUNK:] for x in kq]
    else:
        kk = [_dot_nt(x, x) for x in k16]
    a_mat = [jnp.where(strict[i], bt[i] * decay[i] * kk[i], 0.0) for i in n]
    e = _unit_tri_inverse(a_mat, strict, row, col)
    eg = [jnp.exp(x) for x in gc]
    rhs = [jnp.concatenate([(bt[i] * eg[i]) * kh[i], bt[i] * vh[i]], axis=1) for i in n]
    sol = [rhs[i] + _dot(e[i].astype(BF16), rhs[i].astype(BF16)) for i in n]
    s_h = [s_scr[d, h] for d, h in chains]
    s16 = [x.astype(BF16) for x in s_h]
    if with_output:
        wq = [_dot(jnp.concatenate([sol[i][:, :HEAD_DIM].astype(BF16), (qh[i] * eg[i]).astype(BF16)], axis=0), s16[i])
              for i in n]
        ws = [x[:CHUNK] for x in wq]
        qs = [x[CHUNK:] for x in wq]
    else:
        ws = [_dot(sol[i][:, :HEAD_DIM].astype(BF16), s16[i]) for i in n]
    u16 = [(sol[i][:, HEAD_DIM:] - ws[i]).astype(BF16) for i in n]
    if with_output:
        for i, (d, h) in enumerate(chains):
            o_refs[d][0, :, h * HEAD_DIM:(h + 1) * HEAD_DIM] = qs[i] + _dot((qk[i] * decay[i]).astype(BF16), u16[i])
    for i, (d, h) in enumerate(chains):
        k_dec = kh[i] * jnp.exp(gt[i] - gc[i])
        s_scr[d, h] = jnp.exp(gt[i]) * s_h[i] + _dot_tn(k_dec.astype(BF16), u16[i])

    if not with_output:
        @pl.when(c == pl.num_programs(1) - 1)
        def _():
            sfin_ref[0] = s_scr[...]


def _delta(qkv, p3, gate_block, alog_r, dtb_r, n_heads, s0=None):
    b, length, _ = qkv.shape
    nch = length // CHUNK
    hd = n_heads * HEAD_DIM
    with_init = s0 is not None
    with_output = with_init

    in_specs, args = [], []
    for d in range(2):
        pos = (lambda c: c) if d == 0 else (lambda c: nch - 1 - c)
        in_specs += [pl.BlockSpec((1, CHUNK, hd), lambda i, c, pos=pos: (i, pos(c), 0)),
                     pl.BlockSpec((1, CHUNK, hd), lambda i, c, pos=pos: (i, pos(c), 1)),
                     pl.BlockSpec((1, CHUNK, hd), lambda i, c, pos=pos: (i, pos(c), 2)),
                     pl.BlockSpec((1, CHUNK, LANES), lambda i, c, pos=pos: (i, pos(c), gate_block))]
        args += [qkv, qkv, qkv, p3]
    in_specs += [pl.BlockSpec((1, LANES), lambda i, c: (0, 0))] * 2
    args += [alog_r, dtb_r]
    state_spec = pl.BlockSpec((1, 2, n_heads, HEAD_DIM, HEAD_DIM), lambda i, c: (i, 0, 0, 0, 0))
    if with_init:
        in_specs.append(state_spec)
        args.append(s0)
    if with_output:
        out_specs = [pl.BlockSpec((1, CHUNK, hd), lambda i, c: (i, c, 0)),
                     pl.BlockSpec((1, CHUNK, hd), lambda i, c: (i, nch - 1 - c, 0))]
        out_shape = [jax.ShapeDtypeStruct((b, length, hd), F32)] * 2
    else:
        out_specs = state_spec
        out_shape = jax.ShapeDtypeStruct((b, 2, n_heads, HEAD_DIM, HEAD_DIM), F32)
    return pl.pallas_call(
        functools.partial(_delta_kernel, n_heads=n_heads, with_init=with_init, with_output=with_output),
        grid=(b, nch),
        in_specs=in_specs,
        out_specs=out_specs,
        out_shape=out_shape,
        scratch_shapes=[pltpu.VMEM((2, n_heads, HEAD_DIM, HEAD_DIM), F32)],
        compiler_params=_params(("parallel", "arbitrary"), 40),
        name="delta_lat" if with_output else "delta_ctx",
    )(*args)


def _cmul(ar, ai, br, bi):
    return ar * br - ai * bi, ar * bi + ai * br


def _s5_tables(a_re, a_im, log_dt, b_re, b_im, c_re, c_im):
    g, p, s = b_re.shape
    dt = jnp.exp(log_dt.astype(F32))[..., None]
    lr, li = a_re.astype(F32) * dt, a_im.astype(F32) * dt

    def powers(d, n):
        n = n.astype(F32)
        mag = jnp.exp(lr[d][..., None] * n)
        ang = li[d][..., None] * n
        return mag * jnp.cos(ang), mag * jnp.sin(ang)

    abr = jnp.exp(lr) * jnp.cos(li)
    abi = jnp.exp(lr) * jnp.sin(li)
    nr, ni = abr - 1.0, abi
    cr, ci = a_re.astype(F32), a_im.astype(F32)
    den = cr * cr + ci * ci
    fr, fi = (nr * cr + ni * ci) / den, (ni * cr - nr * ci) / den
    bbr, bbi = _cmul(fr[..., None], fi[..., None], b_re.astype(F32)[None], b_im.astype(F32)[None])
    ccr = jnp.swapaxes(c_re.astype(F32), 1, 2)
    cci = jnp.swapaxes(c_im.astype(F32), 1, 2)
    sub = jnp.arange(S5_SUB)

    def x_table(d, n):
        er, ei = powers(d, n)
        xr, xi = _cmul(bbr[d][:, :, None, :], bbi[d][:, :, None, :], er[..., None], ei[..., None])
        xr = jnp.transpose(xr, (0, 2, 3, 1)).reshape(g, S5_SUB * s, p)
        xi = jnp.transpose(xi, (0, 2, 3, 1)).reshape(g, S5_SUB * s, p)
        return jnp.concatenate([xr, xi], axis=-1)

    def y_table(d, n):
        er, ei = powers(d, n)
        yr, yi = _cmul(ccr[:, :, None, :], cci[:, :, None, :], er[..., None], ei[..., None])
        return jnp.concatenate([yr.reshape(g, p, S5_SUB * s), -yi.reshape(g, p, S5_SUB * s)], axis=1)

    xt = jnp.stack([x_table(0, S5_SUB - 1 - sub), x_table(1, sub)], axis=1)
    yt = jnp.stack([y_table(0, sub + 1), y_table(0, sub - (S5_SUB - 1)), y_table(1, -sub), y_table(1, S5_SUB - sub)], axis=1)
    rows = []
    for d in range(2):
        er, ei = powers(d, S5_SUB * jnp.arange(CHUNK // S5_SUB + 1))
        for k in range(CHUNK // S5_SUB + 1):
            rows += [jnp.concatenate([er[..., k], er[..., k]], -1), jnp.concatenate([-ei[..., k], ei[..., k]], -1)]
    pw = jnp.stack(rows, axis=1)
    return xt, yt, pw


def _s5_kernel(uc_ref, u0_ref, u1_ref, xt_ref, yt_ref, pw_ref, dsk_ref, y_ref, bst_scr, m_scr, h_scr, y_scr,
               *, nb, grid_rows):
    nsub = CHUNK // S5_SUB
    n_pw = 2 * (nsub + 1)
    n_lat = u0_ref.shape[1] // nb
    n_ctx = uc_ref.shape[1] // nb
    row = lax.broadcasted_iota(jnp.int32, (LANES, LANES), 0)
    col = lax.broadcasted_iota(jnp.int32, (LANES, LANES), 1)
    jj, ii = row >> 4, col >> 4

    def hdot(a, b):
        return jnp.dot(a, b, precision=lax.Precision.HIGHEST, preferred_element_type=F32)

    def one_group(gi, carry):
        pw = pw_ref[gi]

        def crot(x, d, k, pw=pw):
            r0 = d * n_pw + 2 * k
            return pw[r0:r0 + 1] * x + pw[r0 + 1:r0 + 2] * pltpu.roll(x, LANES // 2, axis=1)

        xf, xb = xt_ref[gi, 0], xt_ref[gi, 1]
        xfs = [xf] + [crot(xf, 0, k) for k in range(1, nsub)]
        xbs = [xb] + [crot(xb, 1, k) for k in range(1, nsub)]
        yf1, yf0, yb0, yb1 = yt_ref[gi, 0], yt_ref[gi, 1], yt_ref[gi, 2], yt_ref[gi, 3]

        for j in range(nsub):
            bst_scr[j * LANES:(j + 1) * LANES, 0:LANES] = xfs[nsub - 1 - j].astype(BF16)
            bst_scr[j * LANES:(j + 1) * LANES, LANES:2 * LANES] = xbs[j].astype(BF16)
        bst = bst_scr[...]
        ul = jnp.concatenate([u0_ref[gi], u1_ref[gi]], axis=1)
        hc = _dot(uc_ref[gi], bst)
        hl = _dot(ul, bst)
        h_scr[0] = hl[:, 0:LANES]
        h_scr[1] = hl[:, LANES:2 * LANES]

        hf = jnp.zeros((nb, LANES), F32)
        hb = jnp.zeros((nb, LANES), F32)
        for c in range(n_ctx):
            hf = crot(hf, 0, nsub) + hc[c * nb:(c + 1) * nb, 0:LANES]
        for c in reversed(range(n_ctx)):
            hb = crot(hb, 1, nsub) + hc[c * nb:(c + 1) * nb, LANES:2 * LANES]
        for c in range(n_lat):
            rows_c = pl.ds(c, nb, stride=n_lat)
            inp = h_scr[0, rows_c, :]
            h_scr[0, rows_c, :] = hf
            hf = crot(hf, 0, nsub) + inp
        for c in reversed(range(n_lat)):
            rows_c = pl.ds(c, nb, stride=n_lat)
            inp = h_scr[1, rows_c, :]
            h_scr[1, rows_c, :] = hb
            hb = crot(hb, 1, nsub) + inp

        diag = (jnp.where(ii >= jj, hdot(xf, yf0), 0.0) + jnp.where(jj >= ii, hdot(xb, yb0), 0.0)
                + jnp.where(row == col, dsk_ref[gi], 0.0)).astype(BF16)
        taps_f = [None] + [hdot(xfs[dl - 1], yf1).astype(BF16) for dl in range(1, nsub)]
        taps_b = [None] + [hdot(xbs[dl], yb0).astype(BF16) for dl in range(1, nsub)]
        for bj in range(nsub):
            for bi in range(nsub):
                blk = diag if bi == bj else (taps_f[bi - bj] if bi > bj else taps_b[bj - bi])
                m_scr[bj * LANES:(bj + 1) * LANES, bi * LANES:(bi + 1) * LANES] = blk
        y = _dot(ul, m_scr[...])
        hin_f, hin_b = h_scr[0], h_scr[1]
        yf16, yb16 = yf1.astype(BF16), yb1.astype(BF16)
        for bi in range(nsub):
            y_scr[gi, :, bi * LANES:(bi + 1) * LANES] = (
                y[:, bi * LANES:(bi + 1) * LANES]
                + _dot(crot(hin_f, 0, bi).astype(BF16), yf16)
                + _dot(crot(hin_b, 1, nsub - 1 - bi).astype(BF16), yb16))
        return carry

    lax.fori_loop(0, u0_ref.shape[0], one_group, 0)

    half = GRID_W // 2
    tokens = grid_rows * GRID_W
    for cc in range(2):
        for r in range(grid_rows):
            lane0 = (cc * grid_rows + r) * S5_GROUP
            piece = jnp.concatenate([y_scr[gi, :, lane0:lane0 + S5_GROUP] for gi in range(u0_ref.shape[0])], axis=1)
            for b in range(nb):
                y_ref[pl.ds(b * tokens + r * GRID_W + cc, half, stride=2), :] = piece[b * half:(b + 1) * half]


def _s5(uc, u0, u1, tables, d_skip, nb, grid_rows):
    xt, yt, pw = tables
    g = u0.shape[0]
    rows_l = u0.shape[1]
    kdim = 2 * u0.shape[2]
    tokens = grid_rows * GRID_W
    gstep = LANES // S5_GROUP
    dsk = jnp.tile(d_skip.astype(F32).reshape(g, 1, S5_GROUP), (1, 1, S5_SUB))

    def spec(a):
        nd = a.ndim
        return pl.BlockSpec((gstep,) + tuple(a.shape[1:]), lambda i: (i,) + (0,) * (nd - 1))

    arrays = [uc, u0, u1, xt, yt, pw, dsk]
    return pl.pallas_call(
        functools.partial(_s5_kernel, nb=nb, grid_rows=grid_rows),
        grid=(g // gstep,),
        in_specs=[spec(a) for a in arrays],
        out_specs=pl.BlockSpec((nb * tokens, LANES), lambda i: (0, i)),
        out_shape=jax.ShapeDtypeStruct((nb * tokens, g * S5_GROUP), F32),
        scratch_shapes=[pltpu.VMEM((kdim, 2 * LANES), BF16), pltpu.VMEM((kdim, kdim), BF16),
                        pltpu.VMEM((2, rows_l, LANES), F32), pltpu.VMEM((gstep, rows_l, kdim), F32)],
        compiler_params=_params(("parallel",), 48),
        name="s5_scan",
    )(*arrays)


def _gelu_tanh(x):
    return 0.5 * x * (1.0 + jnp.tanh(0.7978845608028654 * (x + 0.044715 * (x * x * x))))


def _out_kernel(of_ref, ob_ref, z_ref, y_ref, x_ref, m_ref, dnw_ref, wglu_ref, wdn_ref, ws5_ref, out_ref, *, n_heads):
    o = of_ref[...] + ob_ref[...]
    z = z_ref[...]
    dnw = dnw_ref[...]
    parts = []
    for h in range(n_heads):
        sl = slice(h * HEAD_DIM, (h + 1) * HEAD_DIM)
        oh = o[:, sl]
        yh = oh * lax.rsqrt(jnp.mean(oh * oh, axis=-1, keepdims=True) + EPS) * dnw
        parts.append((yh * _silu(z[:, sl])).astype(BF16))
    dn = jnp.concatenate(parts, axis=1)
    t = _dot(_gelu_tanh(y_ref[...]).astype(BF16), wglu_ref[...])
    half = t.shape[1] // 2
    s5 = (t[:, :half] * _sigmoid(t[:, half:])).astype(BF16)
    acc = _dot(dn, wdn_ref[...]) + _dot(s5, ws5_ref[...])
    out_ref[...] = x_ref[...] + m_ref[0][5:6] * acc


def _out_proj(o_fwd, o_bwd, p2, z_block, y2, x2, mods, dn_norm, w_glu, w_out, n_heads, tokens_per_mod):
    t, d = x2.shape
    hd = n_heads * HEAD_DIM
    s5w = y2.shape[1]
    tm = 256
    tiles_per_mod = tokens_per_mod // tm
    return pl.pallas_call(
        functools.partial(_out_kernel, n_heads=n_heads),
        grid=(t // tm,),
        in_specs=[pl.BlockSpec((tm, hd), lambda i: (i, 0)),
                  pl.BlockSpec((tm, hd), lambda i: (i, 0)),
                  pl.BlockSpec((tm, hd), lambda i: (i, z_block)),
                  pl.BlockSpec((tm, s5w), lambda i: (i, 0)),
                  pl.BlockSpec((tm, d), lambda i: (i, 0)),
                  pl.BlockSpec((1, N_MOD, d), lambda i: (i // tiles_per_mod, 0, 0)),
                  pl.BlockSpec((1, HEAD_DIM), lambda i: (0, 0)),
                  pl.BlockSpec(w_glu.shape, lambda i: (0, 0)),
                  pl.BlockSpec((hd, d), lambda i: (0, 0)),
                  pl.BlockSpec((s5w, d), lambda i: (hd // s5w, 0))],
        out_specs=pl.BlockSpec((tm, d), lambda i: (i, 0)),
        out_shape=jax.ShapeDtypeStruct((t, d), F32),
        compiler_params=_params(("parallel",), 48),
        name="out_proj",
    )(o_fwd, o_bwd, p2, y2, x2, mods, dn_norm.reshape(1, HEAD_DIM), w_glu, w_out, w_out)


def _layer(x, ctx, m_lat, m_ctx, norm_ffn1, ffn1_up, ffn1_down, norm_mix, w_in, dn_conv, dn_a_log,
           dn_dt_bias, dn_norm, s5_a_re, s5_a_im, s5_log_dt, s5_b_re, s5_b_im, s5_c_re, s5_c_im, s5_d,
           s5_glu, w_out, norm_ffn2, ffn2_up, ffn2_down, final_norm):
    b, length, d = x.shape
    lc = ctx.shape[1]
    n_heads = dn_a_log.shape[1]
    hd = n_heads * HEAD_DIM
    n_conv = dn_conv.shape[-1]
    n_qk = n_conv - hd
    s5w = s5_d.shape[0]
    groups = s5w // S5_GROUP
    rows = length // GRID_W

    gate0 = n_conv + hd
    w_in16 = w_in.astype(BF16)
    w_s5 = w_in16[:, gate0 + 4 * n_heads:]
    n_main = gate0 + LANES
    z_block = n_conv // hd
    gate_block = gate0 // LANES
    lane_pad = jnp.zeros((LANES - 2 * n_heads,), F32)
    alog_r = jnp.concatenate([dn_a_log.astype(F32).reshape(-1), lane_pad]).reshape(1, LANES)
    dtb_r = jnp.concatenate([dn_dt_bias.astype(F32).reshape(-1), lane_pad]).reshape(1, LANES)
    conv_w9 = dn_conv.reshape(9, n_conv)

    x2 = x.reshape(b * length, d)
    c2 = ctx.reshape(b * lc, d)

    up1, down1 = ffn1_up.astype(BF16), ffn1_down.astype(BF16)
    x2 = _ffn(x2, m_lat, norm_ffn1, up1, down1, 0, length)
    c2 = _ffn(c2, m_ctx, norm_ffn1, up1, down1, 0, b * lc)

    p_lat, u_lat0, u_lat1 = _proj(x2, m_lat, norm_mix, w_in16, n_main, w_s5, 1, length, packed=True)
    p_ctx, u_ctx = _proj(c2, m_ctx, norm_mix, w_in16, n_main, w_s5, 1, b * lc, packed=False)
    p_lat3 = p_lat.reshape(b, length, n_main)
    p_ctx3 = p_ctx.reshape(b, lc, n_main)

    qkv_lat = _conv(p_lat3, conv_w9, rows, GRID_W, n_conv, n_qk)
    qkv_ctx = _conv(p_ctx3, conv_w9, 1, lc, n_conv, n_qk)
    s_ctx = _delta(qkv_ctx, p_ctx3, gate_block, alog_r, dtb_r, n_heads)
    o_fwd, o_bwd = _delta(qkv_lat, p_lat3, gate_block, alog_r, dtb_r, n_heads, s0=s_ctx)

    tables = _s5_tables(s5_a_re, s5_a_im, s5_log_dt, s5_b_re, s5_b_im, s5_c_re, s5_c_im)
    u_ctx = u_ctx.astype(BF16).reshape(b, lc // CHUNK, CHUNK, groups, S5_GROUP)
    u_ctx = jnp.transpose(u_ctx, (3, 1, 0, 2, 4)).reshape(groups, (lc // CHUNK) * b, CHUNK * S5_GROUP)
    y2 = _s5(u_ctx, u_lat0, u_lat1, tables, s5_d, b, rows)

    x2 = _out_proj(o_fwd.reshape(b * length, hd), o_bwd.reshape(b * length, hd), p_lat, z_block, y2, x2, m_lat, dn_norm,
                   s5_glu.astype(BF16), w_out.astype(BF16), n_heads, length)

    x2 = _ffn(x2, m_lat, norm_ffn2, ffn2_up.astype(BF16), ffn2_down.astype(BF16), 2, length,
              final_w=final_norm)
    return x2.reshape(b, length, d)


def kernel(x, c, ctx, c_ctx, w_mod, b_mod, norm_ffn1, ffn1_up, ffn1_down, norm_mix, w_in, dn_conv, dn_a_log, dn_dt_bias, dn_norm, s5_a_re, s5_a_im, s5_log_dt, s5_b_re, s5_b_im, s5_c_re, s5_c_im, s5_d, s5_glu, w_out, norm_ffn2, ffn2_up, ffn2_down, final_norm):
    depth = w_mod.shape[0]
    assert depth == 1, "the context stream update of deeper stacks is not implemented"
    b, _, d = x.shape
    cond = jnp.concatenate([c, c_ctx[None], jnp.zeros((16 - b - 1, d), c.dtype)], axis=0)
    m = _ada(cond, w_mod[0], b_mod[0]).reshape(16, N_MOD, d)
    return _layer(x, ctx, m[:b], m[b:b + 1], norm_ffn1[0], ffn1_up[0], ffn1_down[0], norm_mix[0], w_in[0],
                  dn_conv[0], dn_a_log[0], dn_dt_bias[0], dn_norm[0], s5_a_re[0], s5_a_im[0], s5_log_dt[0],
                  s5_b_re[0], s5_b_im[0], s5_c_re[0], s5_c_im[0], s5_d[0], s5_glu[0], w_out[0],
                  norm_ffn2[0], ffn2_up[0], ffn2_down[0], final_norm)
```

```python
import functools

import jax
import jax.numpy as jnp
from jax import lax
from jax.experimental import pallas as pl
from jax.experimental.pallas import tpu as pltpu

F32 = jnp.float32
BF16 = jnp.bfloat16
EPS = 1e-6
N_MOD = 9
GRID_W = 64
CHUNK = 64
HEAD_DIM = 128
S5_GROUP = 16
S5_SUB = 8
LANES = 128
MIB = 1024 * 1024


def _params(semantics, vmem_mib):
    return pltpu.CompilerParams(dimension_semantics=semantics, vmem_limit_bytes=vmem_mib * MIB)


def _sigmoid(x):
    return 1.0 / (1.0 + jnp.exp(-x))


def _silu(x):
    return x * _sigmoid(x)


def _dot(a, b):
    return jnp.dot(a, b, preferred_element_type=F32)


def _dot_nt(a, b):
    return lax.dot_general(a, b, (((1,), (1,)), ((), ())), preferred_element_type=F32)


def _dot_tn(a, b):
    return lax.dot_general(a, b, (((0,), (0,)), ((), ())), preferred_element_type=F32)


def _modnorm(x, norm_w, scale, shift):
    y = x * lax.rsqrt(jnp.mean(x * x, axis=-1, keepdims=True) + EPS) * norm_w
    return y * (1.0 + scale) + shift


def _ada_kernel(c_ref, w_ref, b_ref, o_ref):
    a = _silu(c_ref[...]).astype(BF16)
    o_ref[...] = _dot(a, w_ref[...].astype(BF16)) + b_ref[...]


def _ada(cond, w_mod, b_mod):
    rows, d = cond.shape
    n = w_mod.shape[1]
    tn = 1024
    return pl.pallas_call(
        _ada_kernel,
        grid=(n // tn,),
        in_specs=[pl.BlockSpec((rows, d), lambda j: (0, 0)),
                  pl.BlockSpec((d, tn), lambda j: (0, j)),
                  pl.BlockSpec((1, tn), lambda j: (0, j))],
        out_specs=pl.BlockSpec((rows, tn), lambda j: (0, j)),
        out_shape=jax.ShapeDtypeStruct((rows, n), F32),
        compiler_params=_params(("parallel",), 40),
        name="ada_mod",
    )(cond, w_mod, b_mod.reshape(1, n))


def _ffn_kernel(x_ref, m_ref, nw_ref, wg_ref, wu_ref, wd_ref, *rest, sub, final):
    if final:
        fin_ref, o_ref, h_scr, acc_scr = rest
    else:
        o_ref, h_scr, acc_scr = rest
    f = pl.program_id(1)

    @pl.when(f == 0)
    def _():
        m = m_ref[0]
        h = _modnorm(x_ref[...], nw_ref[...], m[3 * sub + 1:3 * sub + 2], m[3 * sub:3 * sub + 1])
        h_scr[...] = h.astype(BF16)
        acc_scr[...] = jnp.zeros_like(acc_scr)

    h = h_scr[...]
    g = _dot(h, wg_ref[...])
    u = _dot(h, wu_ref[...])
    acc_scr[...] += _dot((_silu(g) * u).astype(BF16), wd_ref[...])

    @pl.when(f == pl.num_programs(1) - 1)
    def _():
        m = m_ref[0]
        y = x_ref[...] + 0.5 * m[3 * sub + 2:3 * sub + 3] * acc_scr[...]
        if final:
            y = y * lax.rsqrt(jnp.mean(y * y, axis=-1, keepdims=True) + EPS) * fin_ref[...]
        o_ref[...] = y


def _ffn(x2, mods, norm_w, w_up, w_down, sub, tokens_per_mod, final_w=None):
    t, d = x2.shape
    f = w_down.shape[0]
    tm, tf = 512, 512
    nf = f // tf
    tiles_per_mod = tokens_per_mod // tm
    final = final_w is not None
    in_specs = [pl.BlockSpec((tm, d), lambda i, j: (i, 0)),
                pl.BlockSpec((1, N_MOD, d), lambda i, j: (i // tiles_per_mod, 0, 0)),
                pl.BlockSpec((1, d), lambda i, j: (0, 0)),
                pl.BlockSpec((d, tf), lambda i, j: (0, j)),
                pl.BlockSpec((d, tf), lambda i, j: (0, nf + j)),
                pl.BlockSpec((tf, d), lambda i, j: (j, 0))]
    args = [x2, mods, norm_w.reshape(1, d), w_up, w_up, w_down]
    if final:
        in_specs.append(pl.BlockSpec((1, d), lambda i, j: (0, 0)))
        args.append(final_w.reshape(1, d))
    return pl.pallas_call(
        functools.partial(_ffn_kernel, sub=sub, final=final),
        grid=(t // tm, nf),
        in_specs=in_specs,
        out_specs=pl.BlockSpec((tm, d), lambda i, j: (i, 0)),
        out_shape=jax.ShapeDtypeStruct((t, d), F32),
        scratch_shapes=[pltpu.VMEM((tm, d), BF16), pltpu.VMEM((tm, d), F32)],
        compiler_params=_params(("parallel", "arbitrary"), 48),
        name="ffn_final" if final else "ffn",
    )(*args)


def _proj_kernel(x_ref, m_ref, nw_ref, w_ref, ws5_ref, *rest, sub, packed):
    if packed:
        o_ref, u0_ref, u1_ref, h_scr, a_scr = rest
    else:
        o_ref, s5_ref, h_scr = rest

    @pl.when(pl.program_id(1) == 0)
    def _():
        m = m_ref[0]
        h = _modnorm(x_ref[...], nw_ref[...], m[3 * sub + 1:3 * sub + 2], m[3 * sub:3 * sub + 1])
        h_scr[...] = h.astype(BF16)
        s5 = _dot(h_scr[...], ws5_ref[...])
        if packed:
            _s5_pack(s5, a_scr, (u0_ref, u1_ref))
        else:
            s5_ref[...] = s5

    o_ref[...] = _dot(h_scr[...], w_ref[...])


def _s5_pack(a, a_scr, u_refs):
    tm, width = a.shape
    half = GRID_W // 2
    n_rl = tm // GRID_W
    for k in range(width // LANES):
        a_scr[k] = a[:, k * LANES:(k + 1) * LANES]
    for cc in range(2):
        acc = jnp.concatenate([a_scr[k, pl.ds(cc, tm // 2, stride=2), :] for k in range(width // LANES)], axis=1)
        for g in range(width // S5_GROUP):
            piece = jnp.concatenate([acc[rl * half:(rl + 1) * half, g * S5_GROUP:(g + 1) * S5_GROUP]
                                     for rl in range(n_rl)], axis=1)
            u_refs[cc][g] = piece.astype(BF16)


def _proj(x2, mods, norm_w, w, n_main, w_s5, sub, tokens_per_mod, packed):
    t, d = x2.shape
    tm, tn = 1024, 896
    assert n_main % tn == 0
    tiles_per_mod = tokens_per_mod // tm
    s5w = w_s5.shape[1]
    in_specs = [pl.BlockSpec((tm, d), lambda i, j: (i, 0)),
                pl.BlockSpec((1, N_MOD, d), lambda i, j: (i // tiles_per_mod, 0, 0)),
                pl.BlockSpec((1, d), lambda i, j: (0, 0)),
                pl.BlockSpec((d, tn), lambda i, j: (0, j)),
                pl.BlockSpec((d, s5w), lambda i, j: (0, 0))]
    args = [x2, mods, norm_w.reshape(1, d), w, w_s5]
    out_specs = [pl.BlockSpec((tm, tn), lambda i, j: (i, j))]
    out_shape = [jax.ShapeDtypeStruct((t, n_main), F32)]
    scratch = [pltpu.VMEM((tm, d), BF16)]
    if packed:
        groups = s5w // S5_GROUP
        half = GRID_W // 2
        lanes_per_tile = (tm // GRID_W) * S5_GROUP
        rows_total = tokens_per_mod // GRID_W
        n_b = t // tokens_per_mod
        u_spec = pl.BlockSpec((groups, half, lanes_per_tile), lambda i, j: (0, i // tiles_per_mod, i % tiles_per_mod))
        u_shape = jax.ShapeDtypeStruct((groups, n_b * half, rows_total * S5_GROUP), BF16)
        out_specs += [u_spec, u_spec]
        out_shape += [u_shape, u_shape]
        scratch.append(pltpu.VMEM((s5w // LANES, tm, LANES), F32))
    else:
        out_specs.append(pl.BlockSpec((tm, s5w), lambda i, j: (i, 0)))
        out_shape.append(jax.ShapeDtypeStruct((t, s5w), F32))
    return pl.pallas_call(
        functools.partial(_proj_kernel, sub=sub, packed=packed),
        grid=(t // tm, n_main // tn),
        in_specs=in_specs,
        out_specs=out_specs,
        out_shape=out_shape,
        scratch_shapes=scratch,
        compiler_params=_params(("parallel", "arbitrary"), 52),
        name="in_proj_s5" if packed else "in_proj",
    )(*args)


def _conv_kernel(p_ref, w_ref, o_ref, xp_scr, *, rows, width, strip, n_q_tiles, n_qk_tiles):
    length = rows * width
    pad = (xp_scr.shape[0] - length) // 2
    tc = p_ref.shape[2]
    j = pl.program_id(1)
    xp_scr[0:pad, :] = jnp.zeros((pad, tc), F32)
    xp_scr[pad + length:pad + length + pad, :] = jnp.zeros((pad, tc), F32)
    xp_scr[pad:pad + length, :] = p_ref[0]
    w = w_ref[...]
    q_scale = jnp.where(j < n_q_tiles, HEAD_DIM ** -0.5, 1.0).astype(F32)
    is_qk = j < n_qk_tiles
    row_taps = (-1, 0, 1) if rows > 1 else (0,)
    halo = 8
    for s in range(length // strip):
        t0 = s * strip
        col = (lax.broadcasted_iota(jnp.int32, (strip, 1), 0) + t0) & (width - 1)
        base = pad + t0 - halo
        acc = jnp.zeros((strip, tc), F32)
        for dc in (-1, 0, 1):
            z = jnp.zeros((strip + 2 * halo, tc), F32)
            for dr in row_taps:
                tap = (dr + 1) * 3 + (dc + 1)
                z = z + xp_scr[base + dr * width:base + dr * width + strip + 2 * halo, :] * w[tap:tap + 1, :]
            z = z[halo + dc:halo + dc + strip, :]
            if dc == -1:
                z = jnp.where(col >= 1, z, 0.0)
            elif dc == 1:
                z = jnp.where(col <= width - 2, z, 0.0)
            acc = acc + z
        y = _silu(acc)
        for hh in range(tc // HEAD_DIM):
            yh = y[:, hh * HEAD_DIM:(hh + 1) * HEAD_DIM]
            inv = lax.rsqrt(jnp.sum(yh * yh, axis=-1, keepdims=True) + EPS) * q_scale
            o_ref[0, t0:t0 + strip, hh * HEAD_DIM:(hh + 1) * HEAD_DIM] = yh * jnp.where(is_qk, inv, 1.0)


def _conv(p3, conv_w9, rows, width, n_conv, n_qk):
    b, length, _ = p3.shape
    tc = 512
    pad = width + 8 if rows > 1 else 8
    strip = min(256, length)
    kern = functools.partial(_conv_kernel, rows=rows, width=width, strip=strip,
                             n_q_tiles=(n_qk // 2) // tc, n_qk_tiles=n_qk // tc)
    return pl.pallas_call(
        kern,
        grid=(b, n_conv // tc),
        in_specs=[pl.BlockSpec((1, length, tc), lambda i, j: (i, 0, j)),
                  pl.BlockSpec((9, tc), lambda i, j: (0, j))],
        out_specs=pl.BlockSpec((1, length, tc), lambda i, j: (i, 0, j)),
        out_shape=jax.ShapeDtypeStruct((b, length, n_conv), F32),
        scratch_shapes=[pltpu.VMEM((length + 2 * pad, tc), F32)],
        compiler_params=_params(("parallel", "parallel"), 40),
        name="grid_conv",
    )(p3, conv_w9)


def _unit_tri_inverse(a_list, strict_list, row, col):
    n = 1
    e = None
    while n < CHUNK:
        k = n.bit_length() - 1
        pair = ((row >> (k + 1)) == (col >> (k + 1))) & ((row >> k) != (col >> k))
        l_n = [jnp.where(pair & st, a, 0.0) for a, st in zip(a_list, strict_list)]
        if e is None:
            e = [-l for l in l_n]
        else:
            e16 = [x.astype(BF16) for x in e]
            y = [l + _dot(l.astype(BF16), x16) for l, x16 in zip(l_n, e16)]
            e = [x - yy - _dot(x16, yy.astype(BF16)) for x, x16, yy in zip(e, e16, y)]
        n *= 2
    return e


def _delta_kernel(*refs, n_heads, with_init, with_output):
    chunk_refs = (refs[0:4], refs[4:8])
    alog_ref, dtb_ref = refs[8:10]
    rest = refs[10:]
    if with_init:
        s0_ref, rest = rest[0], rest[1:]
    if with_output:
        o_refs, s_scr = rest[0:2], rest[2]
    else:
        sfin_ref, s_scr = rest
    c = pl.program_id(1)

    @pl.when(c == 0)
    def _():
        if with_init:
            s_scr[...] = s0_ref[0]
        else:
            s_scr[...] = jnp.zeros_like(s_scr)

    row = lax.broadcasted_iota(jnp.int32, (CHUNK, CHUNK), 0)
    col = lax.broadcasted_iota(jnp.int32, (CHUNK, CHUNK), 1)
    incl_d = (row >= col, row <= col)
    strict_d = (row > col, row < col)

    chains = [(d, h) for d in range(2) for h in range(n_heads)]
    gc, gt, bt, decay, kh, vh, qh, strict = [], [], [], [], [], [], [], []
    for d in range(2):
        q_ref, k_ref, v_ref, gb_ref = chunk_refs[d]
        gb = gb_ref[0]
        x = gb + dtb_ref[...]
        softplus = jnp.maximum(x, 0.0) + jnp.log1p(jnp.exp(-jnp.abs(x)))
        g_all = -jnp.exp(alog_ref[...]) * softplus
        beta_all = _sigmoid(gb)
        gcum = jnp.dot(incl_d[d].astype(F32), g_all, precision=lax.Precision.HIGHEST,
                       preferred_element_type=F32)
        gtot = jnp.sum(g_all, axis=0, keepdims=True)
        gcum_t = jnp.concatenate([gcum, jnp.zeros_like(gcum)], axis=0).T
        for h in range(n_heads):
            sl = slice(h * HEAD_DIM, (h + 1) * HEAD_DIM)
            lane = d * n_heads + h
            gc.append(gcum[:, lane:lane + 1])
            gt.append(gtot[:, lane:lane + 1])
            bt.append(beta_all[:, 2 * n_heads + lane:2 * n_heads + lane + 1])
            decay.append(jnp.where(incl_d[d], jnp.exp(jnp.where(incl_d[d], gc[-1] - gcum_t[lane:lane + 1, 0:CHUNK], 0.0)), 0.0))
            kh.append(k_ref[0, :, sl])
            vh.append(v_ref[0, :, sl])
            if with_output:
                qh.append(q_ref[0, :, sl])
            strict.append(strict_d[d])
    n = range(len(chains))
    k16 = [x.astype(BF16) for x in kh]
    if with_output:
        kq = [_dot_nt(jnp.concatenate([k16[i], qh[i].astype(BF16)], axis=0), k16[i]) for i in n]
        kk = [x[:CHUNK] for x in kq]
        qk = [x[CHUNK:] for x in kq]
    else:
        kk = [_dot_nt(x, x) for x in k16]
    a_mat = [jnp.where(strict[i], bt[i] * decay[i] * kk[i], 0.0) for i in n]
    e = _unit_tri_inverse(a_mat, strict, row, col)
    eg = [jnp.exp(x) for x in gc]
    rhs = [jnp.concatenate([(bt[i] * eg[i]) * kh[i], bt[i] * vh[i]], axis=1) for i in n]
    sol = [rhs[i] + _dot(e[i].astype(BF16), rhs[i].astype(BF16)) for i in n]
    s_h = [s_scr[d, h] for d, h in chains]
    s16 = [x.astype(BF16) for x in s_h]
    if with_output:
        wq = [_dot(jnp.concatenate([sol[i][:, :HEAD_DIM].astype(BF16), (qh[i] * eg[i]).astype(BF16)], axis=0), s16[i])
              for i in n]
        ws = [x[:CHUNK] for x in wq]
        qs = [x[CHUNK:] for x in wq]
    else:
        ws = [_dot(sol[i][:, :HEAD_DIM].astype(BF16), s16[i]) for i in n]
    u16 = [(sol[i][:, HEAD_DIM:] - ws[i]).astype(BF16) for i in n]
    if with_output:
        for i, (d, h) in enumerate(chains):
            o_refs[d][0, :, h * HEAD_DIM:(h + 1) * HEAD_DIM] = qs[i] + _dot((qk[i] * decay[i]).astype(BF16), u16[i])
    for i, (d, h) in enumerate(chains):
        k_dec = kh[i] * jnp.exp(gt[i] - gc[i])
        s_scr[d, h] = jnp.exp(gt[i]) * s_h[i] + _dot_tn(k_dec.astype(BF16), u16[i])

    if not with_output:
        @pl.when(c == pl.num_programs(1) - 1)
        def _():
            sfin_ref[0] = s_scr[...]


def _delta(qkv, p3, gate_block, alog_r, dtb_r, n_heads, s0=None):
    b, length, _ = qkv.shape
    nch = length // CHUNK
    hd = n_heads * HEAD_DIM
    with_init = s0 is not None
    with_output = with_init

    in_specs, args = [], []
    for d in range(2):
        pos = (lambda c: c) if d == 0 else (lambda c: nch - 1 - c)
        in_specs += [pl.BlockSpec((1, CHUNK, hd), lambda i, c, pos=pos: (i, pos(c), 0)),
                     pl.BlockSpec((1, CHUNK, hd), lambda i, c, pos=pos: (i, pos(c), 1)),
                     pl.BlockSpec((1, CHUNK, hd), lambda i, c, pos=pos: (i, pos(c), 2)),
                     pl.BlockSpec((1, CHUNK, LANES), lambda i, c, pos=pos: (i, pos(c), gate_block))]
        args += [qkv, qkv, qkv, p3]
    in_specs += [pl.BlockSpec((1, LANES), lambda i, c: (0, 0))] * 2
    args += [alog_r, dtb_r]
    state_spec = pl.BlockSpec((1, 2, n_heads, HEAD_DIM, HEAD_DIM), lambda i, c: (i, 0, 0, 0, 0))
    if with_init:
        in_specs.append(state_spec)
        args.append(s0)
    if with_output:
        out_specs = [pl.BlockSpec((1, CHUNK, hd), lambda i, c: (i, c, 0)),
                     pl.BlockSpec((1, CHUNK, hd), lambda i, c: (i, nch - 1 - c, 0))]
        out_shape = [jax.ShapeDtypeStruct((b, length, hd), F32)] * 2
    else:
        out_specs = state_spec
        out_shape = jax.ShapeDtypeStruct((b, 2, n_heads, HEAD_DIM, HEAD_DIM), F32)
    return pl.pallas_call(
        functools.partial(_delta_kernel, n_heads=n_heads, with_init=with_init, with_output=with_output),
        grid=(b, nch),
        in_specs=in_specs,
        out_specs=out_specs,
        out_shape=out_shape,
        scratch_shapes=[pltpu.VMEM((2, n_heads, HEAD_DIM, HEAD_DIM), F32)],
        compiler_params=_params(("parallel", "arbitrary"), 40),
        name="delta_lat" if with_output else "delta_ctx",
    )(*args)


def _cmul(ar, ai, br, bi):
    return ar * br - ai * bi, ar * bi + ai * br


def _s5_tables(a_re, a_im, log_dt, b_re, b_im, c_re, c_im, d_skip):
    g, p, s = b_re.shape
    dt = jnp.exp(log_dt.astype(F32))[..., None]
    lr, li = a_re.astype(F32) * dt, a_im.astype(F32) * dt

    def powers(d, n):
        n = n.astype(F32)
        mag = jnp.exp(lr[d][..., None] * n)
        ang = li[d][..., None] * n
        return mag * jnp.cos(ang), mag * jnp.sin(ang)

    abr = jnp.exp(lr) * jnp.cos(li)
    abi = jnp.exp(lr) * jnp.sin(li)
    nr, ni = abr - 1.0, abi
    cr, ci = a_re.astype(F32), a_im.astype(F32)
    den = cr * cr + ci * ci
    fr, fi = (nr * cr + ni * ci) / den, (ni * cr - nr * ci) / den
    bbr, bbi = _cmul(fr[..., None], fi[..., None], b_re.astype(F32)[None], b_im.astype(F32)[None])
    ccr = jnp.swapaxes(c_re.astype(F32), 1, 2)
    cci = jnp.swapaxes(c_im.astype(F32), 1, 2)
    sub = jnp.arange(S5_SUB)

    def x_table(d, n):
        er, ei = powers(d, n)
        xr, xi = _cmul(bbr[d][:, :, None, :], bbi[d][:, :, None, :], er[..., None], ei[..., None])
        xr = jnp.transpose(xr, (0, 2, 3, 1)).reshape(g, S5_SUB * s, p)
        xi = jnp.transpose(xi, (0, 2, 3, 1)).reshape(g, S5_SUB * s, p)
        return jnp.concatenate([xr, xi], axis=-1)

    def y_table(d, n):
        er, ei = powers(d, n)
        yr, yi = _cmul(ccr[:, :, None, :], cci[:, :, None, :], er[..., None], ei[..., None])
        return jnp.concatenate([yr.reshape(g, p, S5_SUB * s), -yi.reshape(g, p, S5_SUB * s)], axis=1)

    xt = jnp.stack([x_table(0, S5_SUB - 1 - sub), x_table(1, sub)], axis=1)
    yt = jnp.stack([y_table(0, sub + 1), y_table(1, S5_SUB - sub)], axis=1)

    def diag_taps(d, flip):
        er, ei = powers(d, sub)
        xr, xi = _cmul(bbr[d][:, :, None, :], bbi[d][:, :, None, :], er[..., None], ei[..., None])
        taps = (jnp.einsum('gpks,gpq->gkqs', xr, ccr, precision=lax.Precision.HIGHEST)
                - jnp.einsum('gpks,gpq->gkqs', xi, cci, precision=lax.Precision.HIGHEST))
        lag = sub[None, :] - sub[:, None]
        lag = -lag if flip else lag
        blk = jnp.where((lag >= 0)[None, :, :, None, None], taps[:, jnp.clip(lag, 0, S5_SUB - 1)], 0.0)
        return jnp.transpose(blk, (0, 1, 4, 2, 3)).reshape(g, S5_SUB * s, S5_SUB * s)

    skip = jnp.tile(d_skip.astype(F32).reshape(g, 1, s), (1, 1, S5_SUB))
    diag = diag_taps(0, False) + diag_taps(1, True) + jnp.eye(S5_SUB * s, dtype=F32)[None] * skip
    rows = []
    for d in range(2):
        er, ei = powers(d, S5_SUB * jnp.arange(CHUNK // S5_SUB + 1))
        for k in range(CHUNK // S5_SUB + 1):
            rows += [jnp.concatenate([er[..., k], er[..., k]], -1), jnp.concatenate([-ei[..., k], ei[..., k]], -1)]
    pw = jnp.stack(rows, axis=1)
    return xt, yt, diag, pw


def _s5_kernel(uc_ref, u0_ref, u1_ref, xt_ref, yt_ref, diag_ref, pw_ref, y_ref, bst_scr, m_scr, h_scr, y_scr,
               *, nb, grid_rows):
    nsub = CHUNK // S5_SUB
    n_pw = 2 * (nsub + 1)
    n_lat = u0_ref.shape[1] // nb
    n_ctx = uc_ref.shape[1] // nb

    def hdot(a, b):
        return jnp.dot(a, b, precision=lax.Precision.HIGHEST, preferred_element_type=F32)

    def one_group(gi, carry):
        pw = pw_ref[gi]

        def crot(x, d, k, pw=pw):
            r0 = d * n_pw + 2 * k
            return pw[r0:r0 + 1] * x + pw[r0 + 1:r0 + 2] * pltpu.roll(x, LANES // 2, axis=1)

        xf, xb = xt_ref[gi, 0], xt_ref[gi, 1]
        xfs = [xf] + [crot(xf, 0, k) for k in range(1, nsub)]
        xbs = [xb] + [crot(xb, 1, k) for k in range(1, nsub)]
        yf1, yb1 = yt_ref[gi, 0], yt_ref[gi, 1]

        for j in range(nsub):
            bst_scr[j * LANES:(j + 1) * LANES, 0:LANES] = xfs[nsub - 1 - j].astype(BF16)
            bst_scr[j * LANES:(j + 1) * LANES, LANES:2 * LANES] = xbs[j].astype(BF16)
        bst = bst_scr[...]
        ul = jnp.concatenate([u0_ref[gi], u1_ref[gi]], axis=1)
        hc = _dot(uc_ref[gi], bst)
        hl = _dot(ul, bst)
        h_scr[0] = hl[:, 0:LANES]
        h_scr[1] = hl[:, LANES:2 * LANES]

        hf = jnp.zeros((nb, LANES), F32)
        hb = jnp.zeros((nb, LANES), F32)
        for c in range(n_ctx):
            hf = crot(hf, 0, nsub) + hc[c * nb:(c + 1) * nb, 0:LANES]
        for c in reversed(range(n_ctx)):
            hb = crot(hb, 1, nsub) + hc[c * nb:(c + 1) * nb, LANES:2 * LANES]
        for c in range(n_lat):
            rows_c = pl.ds(c, nb, stride=n_lat)
            inp = h_scr[0, rows_c, :]
            h_scr[0, rows_c, :] = hf
            hf = crot(hf, 0, nsub) + inp
        for c in reversed(range(n_lat)):
            rows_c = pl.ds(c, nb, stride=n_lat)
            inp = h_scr[1, rows_c, :]
            h_scr[1, rows_c, :] = hb
            hb = crot(hb, 1, nsub) + inp

        diag = diag_ref[gi].astype(BF16)
        taps_f = [None] + [hdot(xfs[dl - 1], yf1).astype(BF16) for dl in range(1, nsub)]
        taps_b = [None] + [hdot(xbs[dl - 1], yb1).astype(BF16) for dl in range(1, nsub)]
        for bj in range(nsub):
            for bi in range(nsub):
                blk = diag if bi == bj else (taps_f[bi - bj] if bi > bj else taps_b[bj - bi])
                m_scr[bj * LANES:(bj + 1) * LANES, bi * LANES:(bi + 1) * LANES] = blk
        y = _dot(ul, m_scr[...])
        hin_f, hin_b = h_scr[0], h_scr[1]
        yf16, yb16 = yf1.astype(BF16), yb1.astype(BF16)
        for bi in range(nsub):
            y_scr[gi, :, bi * LANES:(bi + 1) * LANES] = (
                y[:, bi * LANES:(bi + 1) * LANES]
                + _dot(crot(hin_f, 0, bi).astype(BF16), yf16)
                + _dot(crot(hin_b, 1, nsub - 1 - bi).astype(BF16), yb16))
        return carry

    lax.fori_loop(0, u0_ref.shape[0], one_group, 0)

    half = GRID_W // 2
    tokens = grid_rows * GRID_W
    for cc in range(2):
        for r in range(grid_rows):
            lane0 = (cc * grid_rows + r) * S5_GROUP
            piece = jnp.concatenate([y_scr[gi, :, lane0:lane0 + S5_GROUP] for gi in range(u0_ref.shape[0])], axis=1)
            for b in range(nb):
                y_ref[pl.ds(b * tokens + r * GRID_W + cc, half, stride=2), :] = piece[b * half:(b + 1) * half]


def _s5(uc, u0, u1, tables, nb, grid_rows):
    xt, yt, diag, pw = tables
    g = u0.shape[0]
    rows_l = u0.shape[1]
    kdim = 2 * u0.shape[2]
    tokens = grid_rows * GRID_W
    gstep = LANES // S5_GROUP

    def spec(a):
        nd = a.ndim
        return pl.BlockSpec((gstep,) + tuple(a.shape[1:]), lambda i: (i,) + (0,) * (nd - 1))

    arrays = [uc, u0, u1, xt, yt, diag, pw]
    return pl.pallas_call(
        functools.partial(_s5_kernel, nb=nb, grid_rows=grid_rows),
        grid=(g // gstep,),
        in_specs=[spec(a) for a in arrays],
        out_specs=pl.BlockSpec((nb * tokens, LANES), lambda i: (0, i)),
        out_shape=jax.ShapeDtypeStruct((nb * tokens, g * S5_GROUP), F32),
        scratch_shapes=[pltpu.VMEM((kdim, 2 * LANES), BF16), pltpu.VMEM((kdim, kdim), BF16),
                        pltpu.VMEM((2, rows_l, LANES), F32), pltpu.VMEM((gstep, rows_l, kdim), F32)],
        compiler_params=_params(("parallel",), 48),
        name="s5_scan",
    )(*arrays)


def _gelu_tanh(x):
    return 0.5 * x * (1.0 + jnp.tanh(0.7978845608028654 * (x + 0.044715 * (x * x * x))))


def _out_kernel(of_ref, ob_ref, z_ref, y_ref, x_ref, m_ref, dnw_ref, wglu_ref, wdn_ref, ws5_ref, out_ref, *, n_heads):
    o = of_ref[...] + ob_ref[...]
    z = z_ref[...]
    dnw = dnw_ref[...]
    parts = []
    for h in range(n_heads):
        sl = slice(h * HEAD_DIM, (h + 1) * HEAD_DIM)
        oh = o[:, sl]
        yh = oh * lax.rsqrt(jnp.mean(oh * oh, axis=-1, keepdims=True) + EPS) * dnw
        parts.append((yh * _silu(z[:, sl])).astype(BF16))
    dn = jnp.concatenate(parts, axis=1)
    t = _dot(_gelu_tanh(y_ref[...]).astype(BF16), wglu_ref[...])
    half = t.shape[1] // 2
    s5 = (t[:, :half] * _sigmoid(t[:, half:])).astype(BF16)
    acc = _dot(dn, wdn_ref[...]) + _dot(s5, ws5_ref[...])
    out_ref[...] = x_ref[...] + m_ref[0][5:6] * acc


def _out_proj(o_fwd, o_bwd, p2, z_block, y2, x2, mods, dn_norm, w_glu, w_out, n_heads, tokens_per_mod):
    t, d = x2.shape
    hd = n_heads * HEAD_DIM
    s5w = y2.shape[1]
    tm = 256
    tiles_per_mod = tokens_per_mod // tm
    return pl.pallas_call(
        functools.partial(_out_kernel, n_heads=n_heads),
        grid=(t // tm,),
        in_specs=[pl.BlockSpec((tm, hd), lambda i: (i, 0)),
                  pl.BlockSpec((tm, hd), lambda i: (i, 0)),
                  pl.BlockSpec((tm, hd), lambda i: (i, z_block)),
                  pl.BlockSpec((tm, s5w), lambda i: (i, 0)),
                  pl.BlockSpec((tm, d), lambda i: (i, 0)),
                  pl.BlockSpec((1, N_MOD, d), lambda i: (i // tiles_per_mod, 0, 0)),
                  pl.BlockSpec((1, HEAD_DIM), lambda i: (0, 0)),
                  pl.BlockSpec(w_glu.shape, lambda i: (0, 0)),
                  pl.BlockSpec((hd, d), lambda i: (0, 0)),
                  pl.BlockSpec((s5w, d), lambda i: (hd // s5w, 0))],
        out_specs=pl.BlockSpec((tm, d), lambda i: (i, 0)),
        out_shape=jax.ShapeDtypeStruct((t, d), F32),
        compiler_params=_params(("parallel",), 48),
        name="out_proj",
    )(o_fwd, o_bwd, p2, y2, x2, mods, dn_norm.reshape(1, HEAD_DIM), w_glu, w_out, w_out)


def _layer(x, ctx, m_lat, m_ctx, norm_ffn1, ffn1_up, ffn1_down, norm_mix, w_in, dn_conv, dn_a_log,
           dn_dt_bias, dn_norm, s5_a_re, s5_a_im, s5_log_dt, s5_b_re, s5_b_im, s5_c_re, s5_c_im, s5_d,
           s5_glu, w_out, norm_ffn2, ffn2_up, ffn2_down, final_norm):
    b, length, d = x.shape
    lc = ctx.shape[1]
    n_heads = dn_a_log.shape[1]
    hd = n_heads * HEAD_DIM
    n_conv = dn_conv.shape[-1]
    n_qk = n_conv - hd
    s5w = s5_d.shape[0]
    groups = s5w // S5_GROUP
    rows = length // GRID_W

    gate0 = n_conv + hd
    w_in16 = w_in.astype(BF16)
    w_s5 = w_in16[:, gate0 + 4 * n_heads:]
    n_main = gate0 + LANES
    z_block = n_conv // hd
    gate_block = gate0 // LANES
    lane_pad = jnp.zeros((LANES - 2 * n_heads,), F32)
    alog_r = jnp.concatenate([dn_a_log.astype(F32).reshape(-1), lane_pad]).reshape(1, LANES)
    dtb_r = jnp.concatenate([dn_dt_bias.astype(F32).reshape(-1), lane_pad]).reshape(1, LANES)
    conv_w9 = dn_conv.reshape(9, n_conv)

    x2 = x.reshape(b * length, d)
    c2 = ctx.reshape(b * lc, d)

    up1, down1 = ffn1_up.astype(BF16), ffn1_down.astype(BF16)
    x2 = _ffn(x2, m_lat, norm_ffn1, up1, down1, 0, length)
    c2 = _ffn(c2, m_ctx, norm_ffn1, up1, down1, 0, b * lc)

    p_lat, u_lat0, u_lat1 = _proj(x2, m_lat, norm_mix, w_in16, n_main, w_s5, 1, length, packed=True)
    p_ctx, u_ctx = _proj(c2, m_ctx, norm_mix, w_in16, n_main, w_s5, 1, b * lc, packed=False)
    p_lat3 = p_lat.reshape(b, length, n_main)
    p_ctx3 = p_ctx.reshape(b, lc, n_main)

    qkv_lat = _conv(p_lat3, conv_w9, rows, GRID_W, n_conv, n_qk)
    qkv_ctx = _conv(p_ctx3, conv_w9, 1, lc, n_conv, n_qk)
    s_ctx = _delta(qkv_ctx, p_ctx3, gate_block, alog_r, dtb_r, n_heads)
    o_fwd, o_bwd = _delta(qkv_lat, p_lat3, gate_block, alog_r, dtb_r, n_heads, s0=s_ctx)

    tables = _s5_tables(s5_a_re, s5_a_im, s5_log_dt, s5_b_re, s5_b_im, s5_c_re, s5_c_im, s5_d)
    u_ctx = u_ctx.astype(BF16).reshape(b, lc // CHUNK, CHUNK, groups, S5_GROUP)
    u_ctx = jnp.transpose(u_ctx, (3, 1, 0, 2, 4)).reshape(groups, (lc // CHUNK) * b, CHUNK * S5_GROUP)
    y2 = _s5(u_ctx, u_lat0, u_lat1, tables, b, rows)

    x2 = _out_proj(o_fwd.reshape(b * length, hd), o_bwd.reshape(b * length, hd), p_lat, z_block, y2, x2, m_lat, dn_norm,
                   s5_glu.astype(BF16), w_out.astype(BF16), n_heads, length)

    x2 = _ffn(x2, m_lat, norm_ffn2, ffn2_up.astype(BF16), ffn2_down.astype(BF16), 2, length,
              final_w=final_norm)
    return x2.reshape(b, length, d)


def kernel(x, c, ctx, c_ctx, w_mod, b_mod, norm_ffn1, ffn1_up, ffn1_down, norm_mix, w_in, dn_conv, dn_a_log, dn_dt_bias, dn_norm, s5_a_re, s5_a_im, s5_log_dt, s5_b_re, s5_b_im, s5_c_re, s5_c_im, s5_d, s5_glu, w_out, norm_ffn2, ffn2_up, ffn2_down, final_norm):
    depth = w_mod.shape[0]
    assert depth == 1, "the context stream update of deeper stacks is not implemented"
    b, _, d = x.shape
    cond = jnp.concatenate([c, c_ctx[None], jnp.zeros((16 - b - 1, d), c.dtype)], axis=0)
    m = _ada(cond, w_mod[0], b_mod[0]).reshape(16, N_MOD, d)
    return _layer(x, ctx, m[:b], m[b:b + 1], norm_ffn1[0], ffn1_up[0], ffn1_down[0], norm_mix[0], w_in[0],
                  dn_conv[0], dn_a_log[0], dn_dt_bias[0], dn_norm[0], s5_a_re[0], s5_a_im[0], s5_log_dt[0],
                  s5_b_re[0], s5_b_im[0], s5_c_re[0], s5_c_im[0], s5_d[0], s5_glu[0], w_out[0],
                  norm_ffn2[0], ffn2_up[0], ffn2_down[0], final_norm)
```

```python
import functools

import jax
import jax.numpy as jnp
from jax import lax
from jax.experimental import pallas as pl
from jax.experimental.pallas import tpu as pltpu

F32 = jnp.float32
BF16 = jnp.bfloat16
EPS = 1e-6
N_MOD = 9
GRID_W = 64
CHUNK = 64
HEAD_DIM = 128
S5_GROUP = 16
S5_SUB = 8
LANES = 128
MIB = 1024 * 1024


def _params(semantics, vmem_mib):
    return pltpu.CompilerParams(dimension_semantics=semantics, vmem_limit_bytes=vmem_mib * MIB)


def _sigmoid(x):
    return 1.0 / (1.0 + jnp.exp(-x))


def _silu(x):
    return x * _sigmoid(x)


def _dot(a, b):
    return jnp.dot(a, b, preferred_element_type=F32)


def _dot_nt(a, b):
    return lax.dot_general(a, b, (((1,), (1,)), ((), ())), preferred_element_type=F32)


def _dot_tn(a, b):
    return lax.dot_general(a, b, (((0,), (0,)), ((), ())), preferred_element_type=F32)


def _modnorm(x, norm_w, scale, shift):
    y = x * lax.rsqrt(jnp.mean(x * x, axis=-1, keepdims=True) + EPS) * norm_w
    return y * (1.0 + scale) + shift


def _ada_kernel(c_ref, w_ref, b_ref, o_ref):
    a = _silu(c_ref[...]).astype(BF16)
    o_ref[...] = _dot(a, w_ref[...].astype(BF16)) + b_ref[...]


def _ada(cond, w_mod, b_mod):
    rows, d = cond.shape
    n = w_mod.shape[1]
    tn = 1024
    return pl.pallas_call(
        _ada_kernel,
        grid=(n // tn,),
        in_specs=[pl.BlockSpec((rows, d), lambda j: (0, 0)),
                  pl.BlockSpec((d, tn), lambda j: (0, j)),
                  pl.BlockSpec((1, tn), lambda j: (0, j))],
        out_specs=pl.BlockSpec((rows, tn), lambda j: (0, j)),
        out_shape=jax.ShapeDtypeStruct((rows, n), F32),
        compiler_params=_params(("parallel",), 40),
        name="ada_mod",
    )(cond, w_mod, b_mod.reshape(1, n))


def _ffn_kernel(x_ref, m_ref, nw_ref, wg_ref, wu_ref, wd_ref, *rest, sub, final):
    if final:
        fin_ref, o_ref, h_scr, acc_scr = rest
    else:
        o_ref, h_scr, acc_scr = rest
    f = pl.program_id(1)

    @pl.when(f == 0)
    def _():
        m = m_ref[0]
        h = _modnorm(x_ref[...], nw_ref[...], m[3 * sub + 1:3 * sub + 2], m[3 * sub:3 * sub + 1])
        h_scr[...] = h.astype(BF16)
        acc_scr[...] = jnp.zeros_like(acc_scr)

    h = h_scr[...]
    g = _dot(h, wg_ref[...])
    u = _dot(h, wu_ref[...])
    acc_scr[...] += _dot((_silu(g) * u).astype(BF16), wd_ref[...])

    @pl.when(f == pl.num_programs(1) - 1)
    def _():
        m = m_ref[0]
        y = x_ref[...] + 0.5 * m[3 * sub + 2:3 * sub + 3] * acc_scr[...]
        if final:
            y = y * lax.rsqrt(jnp.mean(y * y, axis=-1, keepdims=True) + EPS) * fin_ref[...]
        o_ref[...] = y


def _ffn(x2, mods, norm_w, w_up, w_down, sub, tokens_per_mod, final_w=None):
    t, d = x2.shape
    f = w_down.shape[0]
    tm, tf = 512, 512
    nf = f // tf
    tiles_per_mod = tokens_per_mod // tm
    final = final_w is not None
    in_specs = [pl.BlockSpec((tm, d), lambda i, j: (i, 0)),
                pl.BlockSpec((1, N_MOD, d), lambda i, j: (i // tiles_per_mod, 0, 0)),
                pl.BlockSpec((1, d), lambda i, j: (0, 0)),
                pl.BlockSpec((d, tf), lambda i, j: (0, j)),
                pl.BlockSpec((d, tf), lambda i, j: (0, nf + j)),
                pl.BlockSpec((tf, d), lambda i, j: (j, 0))]
    args = [x2, mods, norm_w.reshape(1, d), w_up, w_up, w_down]
    if final:
        in_specs.append(pl.BlockSpec((1, d), lambda i, j: (0, 0)))
        args.append(final_w.reshape(1, d))
    return pl.pallas_call(
        functools.partial(_ffn_kernel, sub=sub, final=final),
        grid=(t // tm, nf),
        in_specs=in_specs,
        out_specs=pl.BlockSpec((tm, d), lambda i, j: (i, 0)),
        out_shape=jax.ShapeDtypeStruct((t, d), F32),
        scratch_shapes=[pltpu.VMEM((tm, d), BF16), pltpu.VMEM((tm, d), F32)],
        compiler_params=_params(("parallel", "arbitrary"), 48),
        name="ffn_final" if final else "ffn",
    )(*args)


def _proj_kernel(x_ref, m_ref, nw_ref, w_ref, ws5_ref, *rest, sub, packed):
    if packed:
        o_ref, u0_ref, u1_ref, h_scr, a_scr = rest
    else:
        o_ref, s5_ref, h_scr = rest

    @pl.when(pl.program_id(1) == 0)
    def _():
        m = m_ref[0]
        h = _modnorm(x_ref[...], nw_ref[...], m[3 * sub + 1:3 * sub + 2], m[3 * sub:3 * sub + 1])
        h_scr[...] = h.astype(BF16)
        s5 = _dot(h_scr[...], ws5_ref[...])
        if packed:
            _s5_pack(s5, a_scr, (u0_ref, u1_ref))
        else:
            s5_ref[...] = s5

    o_ref[...] = _dot(h_scr[...], w_ref[...])


def _s5_pack(a, a_scr, u_refs):
    tm, width = a.shape
    half = GRID_W // 2
    n_rl = tm // GRID_W
    for k in range(width // LANES):
        a_scr[k] = a[:, k * LANES:(k + 1) * LANES]
    for cc in range(2):
        acc = jnp.concatenate([a_scr[k, pl.ds(cc, tm // 2, stride=2), :] for k in range(width // LANES)], axis=1)
        for g in range(width // S5_GROUP):
            piece = jnp.concatenate([acc[rl * half:(rl + 1) * half, g * S5_GROUP:(g + 1) * S5_GROUP]
                                     for rl in range(n_rl)], axis=1)
            u_refs[cc][g] = piece.astype(BF16)


def _proj(x2, mods, norm_w, w, n_main, w_s5, sub, tokens_per_mod, packed):
    t, d = x2.shape
    tm, tn = 1024, 896
    assert n_main % tn == 0
    tiles_per_mod = tokens_per_mod // tm
    s5w = w_s5.shape[1]
    in_specs = [pl.BlockSpec((tm, d), lambda i, j: (i, 0)),
                pl.BlockSpec((1, N_MOD, d), lambda i, j: (i // tiles_per_mod, 0, 0)),
                pl.BlockSpec((1, d), lambda i, j: (0, 0)),
                pl.BlockSpec((d, tn), lambda i, j: (0, j)),
                pl.BlockSpec((d, s5w), lambda i, j: (0, 0))]
    args = [x2, mods, norm_w.reshape(1, d), w, w_s5]
    out_specs = [pl.BlockSpec((tm, tn), lambda i, j: (i, j))]
    out_shape = [jax.ShapeDtypeStruct((t, n_main), F32)]
    scratch = [pltpu.VMEM((tm, d), BF16)]
    if packed:
        groups = s5w // S5_GROUP
        half = GRID_W // 2
        lanes_per_tile = (tm // GRID_W) * S5_GROUP
        rows_total = tokens_per_mod // GRID_W
        n_b = t // tokens_per_mod
        u_spec = pl.BlockSpec((groups, half, lanes_per_tile), lambda i, j: (0, i // tiles_per_mod, i % tiles_per_mod))
        u_shape = jax.ShapeDtypeStruct((groups, n_b * half, rows_total * S5_GROUP), BF16)
        out_specs += [u_spec, u_spec]
        out_shape += [u_shape, u_shape]
        scratch.append(pltpu.VMEM((s5w // LANES, tm, LANES), F32))
    else:
        out_specs.append(pl.BlockSpec((tm, s5w), lambda i, j: (i, 0)))
        out_shape.append(jax.ShapeDtypeStruct((t, s5w), F32))
    return pl.pallas_call(
        functools.partial(_proj_kernel, sub=sub, packed=packed),
        grid=(t // tm, n_main // tn),
        in_specs=in_specs,
        out_specs=out_specs,
        out_shape=out_shape,
        scratch_shapes=scratch,
        compiler_params=_params(("parallel", "arbitrary"), 52),
        name="in_proj_s5" if packed else "in_proj",
    )(*args)


def _conv_kernel(p_ref, w_ref, o_ref, xp_scr, *, rows, width, strip, n_q_tiles, n_qk_tiles):
    length = rows * width
    pad = (xp_scr.shape[0] - length) // 2
    tc = p_ref.shape[2]
    j = pl.program_id(1)
    xp_scr[0:pad, :] = jnp.zeros((pad, tc), F32)
    xp_scr[pad + length:pad + length + pad, :] = jnp.zeros((pad, tc), F32)
    xp_scr[pad:pad + length, :] = p_ref[0]
    w = w_ref[...]
    q_scale = jnp.where(j < n_q_tiles, HEAD_DIM ** -0.5, 1.0).astype(F32)
    is_qk = j < n_qk_tiles
    row_taps = (-1, 0, 1) if rows > 1 else (0,)
    halo = 8
    for s in range(length // strip):
        t0 = s * strip
        col = (lax.broadcasted_iota(jnp.int32, (strip, 1), 0) + t0) & (width - 1)
        base = pad + t0 - halo
        acc = jnp.zeros((strip, tc), F32)
        for dc in (-1, 0, 1):
            z = jnp.zeros((strip + 2 * halo, tc), F32)
            for dr in row_taps:
                tap = (dr + 1) * 3 + (dc + 1)
                z = z + xp_scr[base + dr * width:base + dr * width + strip + 2 * halo, :] * w[tap:tap + 1, :]
            z = z[halo + dc:halo + dc + strip, :]
            if dc == -1:
                z = jnp.where(col >= 1, z, 0.0)
            elif dc == 1:
                z = jnp.where(col <= width - 2, z, 0.0)
            acc = acc + z
        y = _silu(acc)
        for hh in range(tc // HEAD_DIM):
            yh = y[:, hh * HEAD_DIM:(hh + 1) * HEAD_DIM]
            inv = lax.rsqrt(jnp.sum(yh * yh, axis=-1, keepdims=True) + EPS) * q_scale
            o_ref[0, t0:t0 + strip, hh * HEAD_DIM:(hh + 1) * HEAD_DIM] = yh * jnp.where(is_qk, inv, 1.0)


def _conv(p3, conv_w9, rows, width, n_conv, n_qk):
    b, length, _ = p3.shape
    tc = 512
    pad = width + 8 if rows > 1 else 8
    strip = min(256, length)
    kern = functools.partial(_conv_kernel, rows=rows, width=width, strip=strip,
                             n_q_tiles=(n_qk // 2) // tc, n_qk_tiles=n_qk // tc)
    return pl.pallas_call(
        kern,
        grid=(b, n_conv // tc),
        in_specs=[pl.BlockSpec((1, length, tc), lambda i, j: (i, 0, j)),
                  pl.BlockSpec((9, tc), lambda i, j: (0, j))],
        out_specs=pl.BlockSpec((1, length, tc), lambda i, j: (i, 0, j)),
        out_shape=jax.ShapeDtypeStruct((b, length, n_conv), F32),
        scratch_shapes=[pltpu.VMEM((length + 2 * pad, tc), F32)],
        compiler_params=_params(("parallel", "parallel"), 40),
        name="grid_conv",
    )(p3, conv_w9)


def _unit_tri_inverse(a_list, strict_list, row, col):
    n = 1
    e = None
    while n < CHUNK:
        k = n.bit_length() - 1
        pair = ((row >> (k + 1)) == (col >> (k + 1))) & ((row >> k) != (col >> k))
        l_n = [jnp.where(pair & st, a, 0.0) for a, st in zip(a_list, strict_list)]
        if e is None:
            e = [-l for l in l_n]
        else:
            e16 = [x.astype(BF16) for x in e]
            y = [l + _dot(l.astype(BF16), x16) for l, x16 in zip(l_n, e16)]
            e = [x - yy - _dot(x16, yy.astype(BF16)) for x, x16, yy in zip(e, e16, y)]
        n *= 2
    return e


def _delta_kernel(*refs, n_heads, with_init, with_output):
    chunk_refs = (refs[0:4], refs[4:8])
    alog_ref, dtb_ref = refs[8:10]
    rest = refs[10:]
    if with_init:
        s0_ref, rest = rest[0], rest[1:]
    if with_output:
        o_refs, s_scr = rest[0:2], rest[2]
    else:
        sfin_ref, s_scr = rest
    c = pl.program_id(1)

    @pl.when(c == 0)
    def _():
        if with_init:
            s_scr[...] = s0_ref[0]
        else:
            s_scr[...] = jnp.zeros_like(s_scr)

    row = lax.broadcasted_iota(jnp.int32, (CHUNK, CHUNK), 0)
    col = lax.broadcasted_iota(jnp.int32, (CHUNK, CHUNK), 1)
    incl_d = (row >= col, row <= col)
    strict_d = (row > col, row < col)

    chains = [(d, h) for d in range(2) for h in range(n_heads)]
    gc, gt, bt, decay, kh, vh, qh, strict = [], [], [], [], [], [], [], []
    for d in range(2):
        q_ref, k_ref, v_ref, gb_ref = chunk_refs[d]
        gb = gb_ref[0]
        x = gb + dtb_ref[...]
        softplus = jnp.maximum(x, 0.0) + jnp.log1p(jnp.exp(-jnp.abs(x)))
        g_all = -jnp.exp(alog_ref[...]) * softplus
        beta_all = _sigmoid(gb)
        gcum = jnp.dot(incl_d[d].astype(F32), g_all, precision=lax.Precision.HIGHEST,
                       preferred_element_type=F32)
        gtot = jnp.sum(g_all, axis=0, keepdims=True)
        gcum_t = jnp.concatenate([gcum, jnp.zeros_like(gcum)], axis=0).T
        for h in range(n_heads):
            sl = slice(h * HEAD_DIM, (h + 1) * HEAD_DIM)
            lane = d * n_heads + h
            gc.append(gcum[:, lane:lane + 1])
            gt.append(gtot[:, lane:lane + 1])
            bt.append(beta_all[:, 2 * n_heads + lane:2 * n_heads + lane + 1])
            decay.append(jnp.where(incl_d[d], jnp.exp(jnp.where(incl_d[d], gc[-1] - gcum_t[lane:lane + 1, 0:CHUNK], 0.0)), 0.0))
            kh.append(k_ref[0, :, sl])
            vh.append(v_ref[0, :, sl])
            if with_output:
                qh.append(q_ref[0, :, sl])
            strict.append(strict_d[d])
    n = range(len(chains))
    k16 = [x.astype(BF16) for x in kh]
    if with_output:
        kq = [_dot_nt(jnp.concatenate([k16[i], qh[i].astype(BF16)], axis=0), k16[i]) for i in n]
        kk = [x[:CHUNK] for x in kq]
        qk = [x[CHUNK:] for x in kq]
    else:
        kk = [_dot_nt(x, x) for x in k16]
    a_mat = [jnp.where(strict[i], bt[i] * decay[i] * kk[i], 0.0) for i in n]
    e = _unit_tri_inverse(a_mat, strict, row, col)
    eg = [jnp.exp(x) for x in gc]
    rhs = [jnp.concatenate([(bt[i] * eg[i]) * kh[i], bt[i] * vh[i]], axis=1) for i in n]
    sol = [rhs[i] + _dot(e[i].astype(BF16), rhs[i].astype(BF16)) for i in n]
    s_h = [s_scr[d, h] for d, h in chains]
    s16 = [x.astype(BF16) for x in s_h]
    if with_output:
        wq = [_dot(jnp.concatenate([sol[i][:, :HEAD_DIM].astype(BF16), (qh[i] * eg[i]).astype(BF16)], axis=0), s16[i])
              for i in n]
        ws = [x[:CHUNK] for x in wq]
        qs = [x[CHUNK:] for x in wq]
    else:
        ws = [_dot(sol[i][:, :HEAD_DIM].astype(BF16), s16[i]) for i in n]
    u16 = [(sol[i][:, HEAD_DIM:] - ws[i]).astype(BF16) for i in n]
    if with_output:
        for i, (d, h) in enumerate(chains):
            o_refs[d][0, :, h * HEAD_DIM:(h + 1) * HEAD_DIM] = qs[i] + _dot((qk[i] * decay[i]).astype(BF16), u16[i])
    for i, (d, h) in enumerate(chains):
        k_dec = kh[i] * jnp.exp(gt[i] - gc[i])
        s_scr[d, h] = jnp.exp(gt[i]) * s_h[i] + _dot_tn(k_dec.astype(BF16), u16[i])

    if not with_output:
        @pl.when(c == pl.num_programs(1) - 1)
        def _():
            sfin_ref[0] = s_scr[...]


def _delta(qkv, p3, gate_block, alog_r, dtb_r, n_heads, s0=None):
    b, length, _ = qkv.shape
    nch = length // CHUNK
    hd = n_heads * HEAD_DIM
    with_init = s0 is not None
    with_output = with_init

    in_specs, args = [], []
    for d in range(2):
        pos = (lambda c: c) if d == 0 else (lambda c: nch - 1 - c)
        in_specs += [pl.BlockSpec((1, CHUNK, hd), lambda i, c, pos=pos: (i, pos(c), 0)),
                     pl.BlockSpec((1, CHUNK, hd), lambda i, c, pos=pos: (i, pos(c), 1)),
                     pl.BlockSpec((1, CHUNK, hd), lambda i, c, pos=pos: (i, pos(c), 2)),
                     pl.BlockSpec((1, CHUNK, LANES), lambda i, c, pos=pos: (i, pos(c), gate_block))]
        args += [qkv, qkv, qkv, p3]
    in_specs += [pl.BlockSpec((1, LANES), lambda i, c: (0, 0))] * 2
    args += [alog_r, dtb_r]
    state_spec = pl.BlockSpec((1, 2, n_heads, HEAD_DIM, HEAD_DIM), lambda i, c: (i, 0, 0, 0, 0))
    if with_init:
        in_specs.append(state_spec)
        args.append(s0)
    if with_output:
        out_specs = [pl.BlockSpec((1, CHUNK, hd), lambda i, c: (i, c, 0)),
                     pl.BlockSpec((1, CHUNK, hd), lambda i, c: (i, nch - 1 - c, 0))]
        out_shape = [jax.ShapeDtypeStruct((b, length, hd), F32)] * 2
    else:
        out_specs = state_spec
        out_shape = jax.ShapeDtypeStruct((b, 2, n_heads, HEAD_DIM, HEAD_DIM), F32)
    return pl.pallas_call(
        functools.partial(_delta_kernel, n_heads=n_heads, with_init=with_init, with_output=with_output),
        grid=(b, nch),
        in_specs=in_specs,
        out_specs=out_specs,
        out_shape=out_shape,
        scratch_shapes=[pltpu.VMEM((2, n_heads, HEAD_DIM, HEAD_DIM), F32)],
        compiler_params=_params(("parallel", "arbitrary"), 40),
        name="delta_lat" if with_output else "delta_ctx",
    )(*args)


def _cmul(ar, ai, br, bi):
    return ar * br - ai * bi, ar * bi + ai * br


def _s5_tables(a_re, a_im, log_dt, b_re, b_im, c_re, c_im, d_skip):
    g, p, s = b_re.shape
    dt = jnp.exp(log_dt.astype(F32))[..., None]
    lr, li = a_re.astype(F32) * dt, a_im.astype(F32) * dt

    def powers(d, n):
        n = n.astype(F32)
        mag = jnp.exp(lr[d][..., None] * n)
        ang = li[d][..., None] * n
        return mag * jnp.cos(ang), mag * jnp.sin(ang)

    abr = jnp.exp(lr) * jnp.cos(li)
    abi = jnp.exp(lr) * jnp.sin(li)
    nr, ni = abr - 1.0, abi
    cr, ci = a_re.astype(F32), a_im.astype(F32)
    den = cr * cr + ci * ci
    fr, fi = (nr * cr + ni * ci) / den, (ni * cr - nr * ci) / den
    bbr, bbi = _cmul(fr[..., None], fi[..., None], b_re.astype(F32)[None], b_im.astype(F32)[None])
    ccr = jnp.swapaxes(c_re.astype(F32), 1, 2)
    cci = jnp.swapaxes(c_im.astype(F32), 1, 2)
    sub = jnp.arange(S5_SUB)

    def x_table(d, n):
        er, ei = powers(d, n)
        xr, xi = _cmul(bbr[d][:, :, None, :], bbi[d][:, :, None, :], er[..., None], ei[..., None])
        xr = jnp.transpose(xr, (0, 2, 3, 1)).reshape(g, S5_SUB * s, p)
        xi = jnp.transpose(xi, (0, 2, 3, 1)).reshape(g, S5_SUB * s, p)
        return jnp.concatenate([xr, xi], axis=-1)

    def y_table(d, n):
        er, ei = powers(d, n)
        yr, yi = _cmul(ccr[:, :, None, :], cci[:, :, None, :], er[..., None], ei[..., None])
        return jnp.concatenate([yr.reshape(g, p, S5_SUB * s), -yi.reshape(g, p, S5_SUB * s)], axis=1)

    xt = jnp.stack([x_table(0, S5_SUB - 1 - sub), x_table(1, sub)], axis=1)
    yt = jnp.stack([y_table(0, sub + 1), y_table(1, S5_SUB - sub)], axis=1)

    def near_taps(d, descending):
        er, ei = powers(d, sub)
        xr, xi = _cmul(bbr[d][:, :, None, :], bbi[d][:, :, None, :], er[..., None], ei[..., None])
        taps = (jnp.einsum('gpks,gpq->gskq', xr, ccr, precision=lax.Precision.HIGHEST)
                - jnp.einsum('gpks,gpq->gskq', xi, cci, precision=lax.Precision.HIGHEST))
        if descending:
            taps = taps[:, :, ::-1]
        return taps.reshape(g, s, S5_SUB * s)

    tt = jnp.stack([near_taps(0, False), near_taps(1, True),
                    jnp.tile(d_skip.astype(F32).reshape(g, 1, s), (1, s, S5_SUB))], axis=1)
    rows = []
    for d in range(2):
        er, ei = powers(d, S5_SUB * jnp.arange(CHUNK // S5_SUB + 1))
        for k in range(CHUNK // S5_SUB + 1):
            rows += [jnp.concatenate([er[..., k], er[..., k]], -1), jnp.concatenate([-ei[..., k], ei[..., k]], -1)]
    pw = jnp.stack(rows, axis=1)
    return xt, yt, tt, pw


def _s5_diag_block(tt):
    lane = lax.broadcasted_iota(jnp.int32, (S5_GROUP, LANES), 1)
    chan = lax.broadcasted_iota(jnp.int32, (S5_GROUP, LANES), 0)
    blocks = []
    for jj in range(S5_SUB):
        right = jj * S5_GROUP
        left = (S5_SUB - 1 - jj) * S5_GROUP
        fwd = tt[0] if right == 0 else jnp.where(lane >= right, pltpu.roll(tt[0], right, axis=1), 0.0)
        bwd = tt[1] if left == 0 else jnp.where(lane < LANES - left, pltpu.roll(tt[1], LANES - left, axis=1), 0.0)
        blocks.append(fwd + bwd + jnp.where(lane == right + chan, tt[2], 0.0))
    return jnp.concatenate(blocks, axis=0)


def _s5_kernel(uc_ref, u0_ref, u1_ref, xt_ref, yt_ref, tt_ref, pw_ref, y_ref, bst_scr, m_scr, h_scr, y_scr,
               *, nb, grid_rows):
    nsub = CHUNK // S5_SUB
    n_pw = 2 * (nsub + 1)
    n_lat = u0_ref.shape[1] // nb
    n_ctx = uc_ref.shape[1] // nb

    def hdot(a, b):
        return jnp.dot(a, b, precision=lax.Precision.HIGHEST, preferred_element_type=F32)

    def one_group(gi, carry):
        pw = pw_ref[gi]

        def crot(x, d, k, pw=pw):
            r0 = d * n_pw + 2 * k
            return pw[r0:r0 + 1] * x + pw[r0 + 1:r0 + 2] * pltpu.roll(x, LANES // 2, axis=1)

        xf, xb = xt_ref[gi, 0], xt_ref[gi, 1]
        xfs = [xf] + [crot(xf, 0, k) for k in range(1, nsub)]
        xbs = [xb] + [crot(xb, 1, k) for k in range(1, nsub)]
        yf1, yb1 = yt_ref[gi, 0], yt_ref[gi, 1]

        for j in range(nsub):
            bst_scr[j * LANES:(j + 1) * LANES, 0:LANES] = xfs[nsub - 1 - j].astype(BF16)
            bst_scr[j * LANES:(j + 1) * LANES, LANES:2 * LANES] = xbs[j].astype(BF16)
        bst = bst_scr[...]
        ul = jnp.concatenate([u0_ref[gi], u1_ref[gi]], axis=1)
        hc = _dot(uc_ref[gi], bst)
        hl = _dot(ul, bst)
        h_scr[0] = hl[:, 0:LANES]
        h_scr[1] = hl[:, LANES:2 * LANES]

        hf = jnp.zeros((nb, LANES), F32)
        hb = jnp.zeros((nb, LANES), F32)
        for c in range(n_ctx):
            hf = crot(hf, 0, nsub) + hc[c * nb:(c + 1) * nb, 0:LANES]
        for c in reversed(range(n_ctx)):
            hb = crot(hb, 1, nsub) + hc[c * nb:(c + 1) * nb, LANES:2 * LANES]
        for c in range(n_lat):
            rows_c = pl.ds(c, nb, stride=n_lat)
            inp = h_scr[0, rows_c, :]
            h_scr[0, rows_c, :] = hf
            hf = crot(hf, 0, nsub) + inp
        for c in reversed(range(n_lat)):
            rows_c = pl.ds(c, nb, stride=n_lat)
            inp = h_scr[1, rows_c, :]
            h_scr[1, rows_c, :] = hb
            hb = crot(hb, 1, nsub) + inp

        diag = _s5_diag_block(tt_ref[gi]).astype(BF16)
        taps_f = [None] + [hdot(xfs[dl - 1], yf1).astype(BF16) for dl in range(1, nsub)]
        taps_b = [None] + [hdot(xbs[dl - 1], yb1).astype(BF16) for dl in range(1, nsub)]
        for bj in range(nsub):
            for bi in range(nsub):
                blk = diag if bi == bj else (taps_f[bi - bj] if bi > bj else taps_b[bj - bi])
                m_scr[bj * LANES:(bj + 1) * LANES, bi * LANES:(bi + 1) * LANES] = blk
        y = _dot(ul, m_scr[...])
        hin_f, hin_b = h_scr[0], h_scr[1]
        yf16, yb16 = yf1.astype(BF16), yb1.astype(BF16)
        for bi in range(nsub):
            y_scr[gi, :, bi * LANES:(bi + 1) * LANES] = (
                y[:, bi * LANES:(bi + 1) * LANES]
                + _dot(crot(hin_f, 0, bi).astype(BF16), yf16)
                + _dot(crot(hin_b, 1, nsub - 1 - bi).astype(BF16), yb16))
        return carry

    lax.fori_loop(0, u0_ref.shape[0], one_group, 0)

    half = GRID_W // 2
    tokens = grid_rows * GRID_W
    for cc in range(2):
        for r in range(grid_rows):
            lane0 = (cc * grid_rows + r) * S5_GROUP
            piece = jnp.concatenate([y_scr[gi, :, lane0:lane0 + S5_GROUP] for gi in range(u0_ref.shape[0])], axis=1)
            for b in range(nb):
                y_ref[pl.ds(b * tokens + r * GRID_W + cc, half, stride=2), :] = piece[b * half:(b + 1) * half]


def _s5(uc, u0, u1, tables, nb, grid_rows):
    xt, yt, tt, pw = tables
    g = u0.shape[0]
    rows_l = u0.shape[1]
    kdim = 2 * u0.shape[2]
    tokens = grid_rows * GRID_W
    gstep = LANES // S5_GROUP

    def spec(a):
        nd = a.ndim
        return pl.BlockSpec((gstep,) + tuple(a.shape[1:]), lambda i: (i,) + (0,) * (nd - 1))

    arrays = [uc, u0, u1, xt, yt, tt, pw]
    return pl.pallas_call(
        functools.partial(_s5_kernel, nb=nb, grid_rows=grid_rows),
        grid=(g // gstep,),
        in_specs=[spec(a) for a in arrays],
        out_specs=pl.BlockSpec((nb * tokens, LANES), lambda i: (0, i)),
        out_shape=jax.ShapeDtypeStruct((nb * tokens, g * S5_GROUP), F32),
        scratch_shapes=[pltpu.VMEM((kdim, 2 * LANES), BF16), pltpu.VMEM((kdim, kdim), BF16),
                        pltpu.VMEM((2, rows_l, LANES), F32), pltpu.VMEM((gstep, rows_l, kdim), F32)],
        compiler_params=_params(("parallel",), 48),
        name="s5_scan",
    )(*arrays)


def _gelu_tanh(x):
    return 0.5 * x * (1.0 + jnp.tanh(0.7978845608028654 * (x + 0.044715 * (x * x * x))))


def _out_kernel(of_ref, ob_ref, z_ref, y_ref, x_ref, m_ref, dnw_ref, wglu_ref, wdn_ref, ws5_ref, out_ref, *, n_heads):
    o = of_ref[...] + ob_ref[...]
    z = z_ref[...]
    dnw = dnw_ref[...]
    parts = []
    for h in range(n_heads):
        sl = slice(h * HEAD_DIM, (h + 1) * HEAD_DIM)
        oh = o[:, sl]
        yh = oh * lax.rsqrt(jnp.mean(oh * oh, axis=-1, keepdims=True) + EPS) * dnw
        parts.append((yh * _silu(z[:, sl])).astype(BF16))
    dn = jnp.concatenate(parts, axis=1)
    t = _dot(_gelu_tanh(y_ref[...]).astype(BF16), wglu_ref[...])
    half = t.shape[1] // 2
    s5 = (t[:, :half] * _sigmoid(t[:, half:])).astype(BF16)
    acc = _dot(dn, wdn_ref[...]) + _dot(s5, ws5_ref[...])
    out_ref[...] = x_ref[...] + m_ref[0][5:6] * acc


def _out_proj(o_fwd, o_bwd, p2, z_block, y2, x2, mods, dn_norm, w_glu, w_out, n_heads, tokens_per_mod):
    t, d = x2.shape
    hd = n_heads * HEAD_DIM
    s5w = y2.shape[1]
    tm = 256
    tiles_per_mod = tokens_per_mod // tm
    return pl.pallas_call(
        functools.partial(_out_kernel, n_heads=n_heads),
        grid=(t // tm,),
        in_specs=[pl.BlockSpec((tm, hd), lambda i: (i, 0)),
                  pl.BlockSpec((tm, hd), lambda i: (i, 0)),
                  pl.BlockSpec((tm, hd), lambda i: (i, z_block)),
                  pl.BlockSpec((tm, s5w), lambda i: (i, 0)),
                  pl.BlockSpec((tm, d), lambda i: (i, 0)),
                  pl.BlockSpec((1, N_MOD, d), lambda i: (i // tiles_per_mod, 0, 0)),
                  pl.BlockSpec((1, HEAD_DIM), lambda i: (0, 0)),
                  pl.BlockSpec(w_glu.shape, lambda i: (0, 0)),
                  pl.BlockSpec((hd, d), lambda i: (0, 0)),
                  pl.BlockSpec((s5w, d), lambda i: (hd // s5w, 0))],
        out_specs=pl.BlockSpec((tm, d), lambda i: (i, 0)),
        out_shape=jax.ShapeDtypeStruct((t, d), F32),
        compiler_params=_params(("parallel",), 48),
        name="out_proj",
    )(o_fwd, o_bwd, p2, y2, x2, mods, dn_norm.reshape(1, HEAD_DIM), w_glu, w_out, w_out)


def _layer(x, ctx, m_lat, m_ctx, norm_ffn1, ffn1_up, ffn1_down, norm_mix, w_in, dn_conv, dn_a_log,
           dn_dt_bias, dn_norm, s5_a_re, s5_a_im, s5_log_dt, s5_b_re, s5_b_im, s5_c_re, s5_c_im, s5_d,
           s5_glu, w_out, norm_ffn2, ffn2_up, ffn2_down, final_norm):
    b, length, d = x.shape
    lc = ctx.shape[1]
    n_heads = dn_a_log.shape[1]
    hd = n_heads * HEAD_DIM
    n_conv = dn_conv.shape[-1]
    n_qk = n_conv - hd
    s5w = s5_d.shape[0]
    groups = s5w // S5_GROUP
    rows = length // GRID_W

    gate0 = n_conv + hd
    w_in16 = w_in.astype(BF16)
    w_s5 = w_in16[:, gate0 + 4 * n_heads:]
    n_main = gate0 + LANES
    z_block = n_conv // hd
    gate_block = gate0 // LANES
    lane_pad = jnp.zeros((LANES - 2 * n_heads,), F32)
    alog_r = jnp.concatenate([dn_a_log.astype(F32).reshape(-1), lane_pad]).reshape(1, LANES)
    dtb_r = jnp.concatenate([dn_dt_bias.astype(F32).reshape(-1), lane_pad]).reshape(1, LANES)
    conv_w9 = dn_conv.reshape(9, n_conv)

    x2 = x.reshape(b * length, d)
    c2 = ctx.reshape(b * lc, d)

    up1, down1 = ffn1_up.astype(BF16), ffn1_down.astype(BF16)
    x2 = _ffn(x2, m_lat, norm_ffn1, up1, down1, 0, length)
    c2 = _ffn(c2, m_ctx, norm_ffn1, up1, down1, 0, b * lc)

    p_lat, u_lat0, u_lat1 = _proj(x2, m_lat, norm_mix, w_in16, n_main, w_s5, 1, length, packed=True)
    p_ctx, u_ctx = _proj(c2, m_ctx, norm_mix, w_in16, n_main, w_s5, 1, b * lc, packed=False)
    p_lat3 = p_lat.reshape(b, length, n_main)
    p_ctx3 = p_ctx.reshape(b, lc, n_main)

    qkv_lat = _conv(p_lat3, conv_w9, rows, GRID_W, n_conv, n_qk)
    qkv_ctx = _conv(p_ctx3, conv_w9, 1, lc, n_conv, n_qk)
    s_ctx = _delta(qkv_ctx, p_ctx3, gate_block, alog_r, dtb_r, n_heads)
    o_fwd, o_bwd = _delta(qkv_lat, p_lat3, gate_block, alog_r, dtb_r, n_heads, s0=s_ctx)

    tables = _s5_tables(s5_a_re, s5_a_im, s5_log_dt, s5_b_re, s5_b_im, s5_c_re, s5_c_im, s5_d)
    u_ctx = u_ctx.astype(BF16).reshape(b, lc // CHUNK, CHUNK, groups, S5_GROUP)
    u_ctx = jnp.transpose(u_ctx, (3, 1, 0, 2, 4)).reshape(groups, (lc // CHUNK) * b, CHUNK * S5_GROUP)
    y2 = _s5(u_ctx, u_lat0, u_lat1, tables, b, rows)

    x2 = _out_proj(o_fwd.reshape(b * length, hd), o_bwd.reshape(b * length, hd), p_lat, z_block, y2, x2, m_lat, dn_norm,
                   s5_glu.astype(BF16), w_out.astype(BF16), n_heads, length)

    x2 = _ffn(x2, m_lat, norm_ffn2, ffn2_up.astype(BF16), ffn2_down.astype(BF16), 2, length,
              final_w=final_norm)
    return x2.reshape(b, length, d)


def kernel(x, c, ctx, c_ctx, w_mod, b_mod, norm_ffn1, ffn1_up, ffn1_down, norm_mix, w_in, dn_conv, dn_a_log, dn_dt_bias, dn_norm, s5_a_re, s5_a_im, s5_log_dt, s5_b_re, s5_b_im, s5_c_re, s5_c_im, s5_d, s5_glu, w_out, norm_ffn2, ffn2_up, ffn2_down, final_norm):
    depth = w_mod.shape[0]
    assert depth == 1, "the context stream update of deeper stacks is not implemented"
    b, _, d = x.shape
    cond = jnp.concatenate([c, c_ctx[None], jnp.zeros((16 - b - 1, d), c.dtype)], axis=0)
    m = _ada(cond, w_mod[0], b_mod[0]).reshape(16, N_MOD, d)
    return _layer(x, ctx, m[:b], m[b:b + 1], norm_ffn1[0], ffn1_up[0], ffn1_down[0], norm_mix[0], w_in[0],
                  dn_conv[0], dn_a_log[0], dn_dt_bias[0], dn_norm[0], s5_a_re[0], s5_a_im[0], s5_log_dt[0],
                  s5_b_re[0], s5_b_im[0], s5_c_re[0], s5_c_im[0], s5_d[0], s5_glu[0], w_out[0],
                  norm_ffn2[0], ffn2_up[0], ffn2_down[0], final_norm)
```

```python
import functools

import jax
import jax.numpy as jnp
from jax import lax
from jax.experimental import pallas as pl
from jax.experimental.pallas import tpu as pltpu

F32 = jnp.float32
BF16 = jnp.bfloat16
EPS = 1e-6
N_MOD = 9
GRID_W = 64
CHUNK = 64
HEAD_DIM = 128
S5_GROUP = 16
S5_SUB = 8
LANES = 128
ROW_CHUNK = 64
MIB = 1024 * 1024


def _params(semantics, vmem_mib):
    return pltpu.CompilerParams(dimension_semantics=semantics, vmem_limit_bytes=vmem_mib * MIB)


def _sigmoid(x):
    return 1.0 / (1.0 + jnp.exp(-x))


def _silu(x):
    return x * _sigmoid(x)


def _dot(a, b):
    return jnp.dot(a, b, preferred_element_type=F32)


def _dot_nt(a, b):
    return lax.dot_general(a, b, (((1,), (1,)), ((), ())), preferred_element_type=F32)


def _dot_tn(a, b):
    return lax.dot_general(a, b, (((0,), (0,)), ((), ())), preferred_element_type=F32)


def _modnorm(x, norm_w, scale, shift):
    y = x * lax.rsqrt(jnp.mean(x * x, axis=-1, keepdims=True) + EPS) * norm_w
    return y * (1.0 + scale) + shift


def _modnorm_rows(x_ref, h_ref, inv_scr, norm_w, scale, shift):
    tm, d = x_ref.shape
    gain = norm_w * (1.0 + scale)

    def stats(i, carry):
        rows = pl.ds(pl.multiple_of(i * ROW_CHUNK, ROW_CHUNK), ROW_CHUNK)
        x = x_ref[rows, :]
        inv = lax.rsqrt(jnp.mean(x * x, axis=-1, keepdims=True) + EPS)
        inv_scr[rows, :] = jnp.broadcast_to(inv, (ROW_CHUNK, LANES))
        return carry

    def scale_rows(i, carry):
        rows = pl.ds(pl.multiple_of(i * ROW_CHUNK, ROW_CHUNK), ROW_CHUNK)
        inv = inv_scr[rows, :]
        for k in range(d // LANES):
            cols = slice(k * LANES, (k + 1) * LANES)
            h_ref[rows, cols] = (x_ref[rows, cols] * inv * gain[:, cols] + shift[:, cols]).astype(BF16)
        return carry

    lax.fori_loop(0, tm // ROW_CHUNK, stats, 0, unroll=2)
    lax.fori_loop(0, tm // ROW_CHUNK, scale_rows, 0)


def _ada_kernel(c_ref, w_ref, b_ref, o_ref):
    a = _silu(c_ref[...]).astype(BF16)
    o_ref[...] = _dot(a, w_ref[...].astype(BF16)) + b_ref[...]


def _ada(cond, w_mod, b_mod):
    rows, d = cond.shape
    n = w_mod.shape[1]
    tn = 1024
    return pl.pallas_call(
        _ada_kernel,
        grid=(n // tn,),
        in_specs=[pl.BlockSpec((rows, d), lambda j: (0, 0)),
                  pl.BlockSpec((d, tn), lambda j: (0, j)),
                  pl.BlockSpec((1, tn), lambda j: (0, j))],
        out_specs=pl.BlockSpec((rows, tn), lambda j: (0, j)),
        out_shape=jax.ShapeDtypeStruct((rows, n), F32),
        compiler_params=_params(("parallel",), 40),
        name="ada_mod",
    )(cond, w_mod, b_mod.reshape(1, n))


def _ffn_kernel(x_ref, m_ref, nw_ref, wg_ref, wu_ref, wd_ref, *rest, sub, final):
    if final:
        fin_ref, o_ref, h_scr, acc_scr, inv_scr = rest
    else:
        o_ref, h_scr, acc_scr, inv_scr = rest
    f = pl.program_id(1)

    @pl.when(f == 0)
    def _():
        m = m_ref[0]
        _modnorm_rows(x_ref, h_scr, inv_scr, nw_ref[...], m[3 * sub + 1:3 * sub + 2], m[3 * sub:3 * sub + 1])
        acc_scr[...] = jnp.zeros_like(acc_scr)

    h = h_scr[...]
    g = _dot(h, wg_ref[...])
    u = _dot(h, wu_ref[...])
    acc_scr[...] += _dot((_silu(g) * u).astype(BF16), wd_ref[...])

    @pl.when(f == pl.num_programs(1) - 1)
    def _():
        m = m_ref[0]
        y = x_ref[...] + 0.5 * m[3 * sub + 2:3 * sub + 3] * acc_scr[...]
        if final:
            y = y * lax.rsqrt(jnp.mean(y * y, axis=-1, keepdims=True) + EPS) * fin_ref[...]
        o_ref[...] = y


def _ffn(x2, mods, norm_w, w_up, w_down, sub, tokens_per_mod, final_w=None):
    t, d = x2.shape
    f = w_down.shape[0]
    tm, tf = 512, 512
    nf = f // tf
    tiles_per_mod = tokens_per_mod // tm
    final = final_w is not None
    in_specs = [pl.BlockSpec((tm, d), lambda i, j: (i, 0)),
                pl.BlockSpec((1, N_MOD, d), lambda i, j: (i // tiles_per_mod, 0, 0)),
                pl.BlockSpec((1, d), lambda i, j: (0, 0)),
                pl.BlockSpec((d, tf), lambda i, j: (0, j)),
                pl.BlockSpec((d, tf), lambda i, j: (0, nf + j)),
                pl.BlockSpec((tf, d), lambda i, j: (j, 0))]
    args = [x2, mods, norm_w.reshape(1, d), w_up, w_up, w_down]
    if final:
        in_specs.append(pl.BlockSpec((1, d), lambda i, j: (0, 0)))
        args.append(final_w.reshape(1, d))
    return pl.pallas_call(
        functools.partial(_ffn_kernel, sub=sub, final=final),
        grid=(t // tm, nf),
        in_specs=in_specs,
        out_specs=pl.BlockSpec((tm, d), lambda i, j: (i, 0)),
        out_shape=jax.ShapeDtypeStruct((t, d), F32),
        scratch_shapes=[pltpu.VMEM((tm, d), BF16), pltpu.VMEM((tm, d), F32), pltpu.VMEM((tm, LANES), F32)],
        compiler_params=_params(("parallel", "arbitrary"), 48),
        name="ffn_final" if final else "ffn",
    )(*args)


def _proj_kernel(x_ref, m_ref, nw_ref, w_ref, ws5_ref, *rest, sub, packed):
    if packed:
        o_ref, u0_ref, u1_ref, h_scr, inv_scr, a_scr = rest
    else:
        o_ref, s5_ref, h_scr, inv_scr = rest

    @pl.when(pl.program_id(1) == 0)
    def _():
        m = m_ref[0]
        _modnorm_rows(x_ref, h_scr, inv_scr, nw_ref[...], m[3 * sub + 1:3 * sub + 2], m[3 * sub:3 * sub + 1])
        s5 = _dot(h_scr[...], ws5_ref[...])
        if packed:
            _s5_pack(s5, a_scr, (u0_ref, u1_ref))
        else:
            s5_ref[...] = s5

    o_ref[...] = _dot(h_scr[...], w_ref[...])


def _s5_pack(a, a_scr, u_refs):
    tm, width = a.shape
    half = GRID_W // 2
    n_rl = tm // GRID_W
    for k in range(width // LANES):
        a_scr[k] = a[:, k * LANES:(k + 1) * LANES]
    for cc in range(2):
        acc = jnp.concatenate([a_scr[k, pl.ds(cc, tm // 2, stride=2), :] for k in range(width // LANES)], axis=1)
        for g in range(width // S5_GROUP):
            piece = jnp.concatenate([acc[rl * half:(rl + 1) * half, g * S5_GROUP:(g + 1) * S5_GROUP]
                                     for rl in range(n_rl)], axis=1)
            u_refs[cc][g] = piece.astype(BF16)


def _proj(x2, mods, norm_w, w, n_main, w_s5, sub, tokens_per_mod, packed):
    t, d = x2.shape
    tm, tn = 1024, 896
    assert n_main % tn == 0
    tiles_per_mod = tokens_per_mod // tm
    s5w = w_s5.shape[1]
    in_specs = [pl.BlockSpec((tm, d), lambda i, j: (i, 0)),
                pl.BlockSpec((1, N_MOD, d), lambda i, j: (i // tiles_per_mod, 0, 0)),
                pl.BlockSpec((1, d), lambda i, j: (0, 0)),
                pl.BlockSpec((d, tn), lambda i, j: (0, j)),
                pl.BlockSpec((d, s5w), lambda i, j: (0, 0))]
    args = [x2, mods, norm_w.reshape(1, d), w, w_s5]
    out_specs = [pl.BlockSpec((tm, tn), lambda i, j: (i, j))]
    out_shape = [jax.ShapeDtypeStruct((t, n_main), F32)]
    scratch = [pltpu.VMEM((tm, d), BF16), pltpu.VMEM((tm, LANES), F32)]
    if packed:
        groups = s5w // S5_GROUP
        half = GRID_W // 2
        lanes_per_tile = (tm // GRID_W) * S5_GROUP
        rows_total = tokens_per_mod // GRID_W
        n_b = t // tokens_per_mod
        u_spec = pl.BlockSpec((groups, half, lanes_per_tile), lambda i, j: (0, i // tiles_per_mod, i % tiles_per_mod))
        u_shape = jax.ShapeDtypeStruct((groups, n_b * half, rows_total * S5_GROUP), BF16)
        out_specs += [u_spec, u_spec]
        out_shape += [u_shape, u_shape]
        scratch.append(pltpu.VMEM((s5w // LANES, tm, LANES), F32))
    else:
        out_specs.append(pl.BlockSpec((tm, s5w), lambda i, j: (i, 0)))
        out_shape.append(jax.ShapeDtypeStruct((t, s5w), F32))
    return pl.pallas_call(
        functools.partial(_proj_kernel, sub=sub, packed=packed),
        grid=(t // tm, n_main // tn),
        in_specs=in_specs,
        out_specs=out_specs,
        out_shape=out_shape,
        scratch_shapes=scratch,
        compiler_params=_params(("parallel", "arbitrary"), 52),
        name="in_proj_s5" if packed else "in_proj",
    )(*args)


def _conv_kernel(p_ref, w_ref, o_ref, xp_scr, *, rows, width, strip, n_q_tiles, n_qk_tiles):
    length = rows * width
    pad = (xp_scr.shape[0] - length) // 2
    tc = p_ref.shape[2]
    j = pl.program_id(1)
    xp_scr[0:pad, :] = jnp.zeros((pad, tc), F32)
    xp_scr[pad + length:pad + length + pad, :] = jnp.zeros((pad, tc), F32)
    xp_scr[pad:pad + length, :] = p_ref[0]
    w = w_ref[...]
    q_scale = jnp.where(j < n_q_tiles, HEAD_DIM ** -0.5, 1.0).astype(F32)
    is_qk = j < n_qk_tiles
    row_taps = (-1, 0, 1) if rows > 1 else (0,)
    halo = 8
    for s in range(length // strip):
        t0 = s * strip
        col = (lax.broadcasted_iota(jnp.int32, (strip, 1), 0) + t0) & (width - 1)
        base = pad + t0 - halo
        acc = jnp.zeros((strip, tc), F32)
        for dc in (-1, 0, 1):
            z = jnp.zeros((strip + 2 * halo, tc), F32)
            for dr in row_taps:
                tap = (dr + 1) * 3 + (dc + 1)
                z = z + xp_scr[base + dr * width:base + dr * width + strip + 2 * halo, :] * w[tap:tap + 1, :]
            z = z[halo + dc:halo + dc + strip, :]
            if dc == -1:
                z = jnp.where(col >= 1, z, 0.0)
            elif dc == 1:
                z = jnp.where(col <= width - 2, z, 0.0)
            acc = acc + z
        y = _silu(acc)
        for hh in range(tc // HEAD_DIM):
            yh = y[:, hh * HEAD_DIM:(hh + 1) * HEAD_DIM]
            inv = lax.rsqrt(jnp.sum(yh * yh, axis=-1, keepdims=True) + EPS) * q_scale
            o_ref[0, t0:t0 + strip, hh * HEAD_DIM:(hh + 1) * HEAD_DIM] = yh * jnp.where(is_qk, inv, 1.0)


def _conv(p3, conv_w9, rows, width, n_conv, n_qk):
    b, length, _ = p3.shape
    tc = 512
    pad = width + 8 if rows > 1 else 8
    strip = min(256, length)
    kern = functools.partial(_conv_kernel, rows=rows, width=width, strip=strip,
                             n_q_tiles=(n_qk // 2) // tc, n_qk_tiles=n_qk // tc)
    return pl.pallas_call(
        kern,
        grid=(b, n_conv // tc),
        in_specs=[pl.BlockSpec((1, length, tc), lambda i, j: (i, 0, j)),
                  pl.BlockSpec((9, tc), lambda i, j: (0, j))],
        out_specs=pl.BlockSpec((1, length, tc), lambda i, j: (i, 0, j)),
        out_shape=jax.ShapeDtypeStruct((b, length, n_conv), F32),
        scratch_shapes=[pltpu.VMEM((length + 2 * pad, tc), F32)],
        compiler_params=_params(("parallel", "parallel"), 40),
        name="grid_conv",
    )(p3, conv_w9)


def _unit_tri_inverse(a_list, strict_list, row, col):
    n = 1
    e = None
    while n < CHUNK:
        k = n.bit_length() - 1
        pair = ((row >> (k + 1)) == (col >> (k + 1))) & ((row >> k) != (col >> k))
        l_n = [jnp.where(pair & st, a, 0.0) for a, st in zip(a_list, strict_list)]
        if e is None:
            e = [-l for l in l_n]
        else:
            e16 = [x.astype(BF16) for x in e]
            y = [l + _dot(l.astype(BF16), x16) for l, x16 in zip(l_n, e16)]
            e = [x - yy - _dot(x16, yy.astype(BF16)) for x, x16, yy in zip(e, e16, y)]
        n *= 2
    return e


def _delta_kernel(*refs, n_heads, with_init, with_output):
    chunk_refs = (refs[0:4], refs[4:8])
    alog_ref, dtb_ref = refs[8:10]
    rest = refs[10:]
    if with_init:
        s0_ref, rest = rest[0], rest[1:]
    if with_output:
        o_refs, s_scr = rest[0:2], rest[2]
    else:
        sfin_ref, s_scr = rest
    c = pl.program_id(1)

    @pl.when(c == 0)
    def _():
        if with_init:
            s_scr[...] = s0_ref[0]
        else:
            s_scr[...] = jnp.zeros_like(s_scr)

    row = lax.broadcasted_iota(jnp.int32, (CHUNK, CHUNK), 0)
    col = lax.broadcasted_iota(jnp.int32, (CHUNK, CHUNK), 1)
    incl_d = (row >= col, row <= col)
    strict_d = (row > col, row < col)

    chains = [(d, h) for d in range(2) for h in range(n_heads)]
    gc, gt, bt, decay, kh, vh, qh, strict = [], [], [], [], [], [], [], []
    for d in range(2):
        q_ref, k_ref, v_ref, gb_ref = chunk_refs[d]
        gb = gb_ref[0]
        x = gb + dtb_ref[...]
        softplus = jnp.maximum(x, 0.0) + jnp.log1p(jnp.exp(-jnp.abs(x)))
        g_all = -jnp.exp(alog_ref[...]) * softplus
        beta_all = _sigmoid(gb)
        gcum = jnp.dot(incl_d[d].astype(F32), g_all, precision=lax.Precision.HIGHEST,
                       preferred_element_type=F32)
        gtot = jnp.sum(g_all, axis=0, keepdims=True)
        gcum_t = jnp.concatenate([gcum, jnp.zeros_like(gcum)], axis=0).T
        for h in range(n_heads):
            sl = slice(h * HEAD_DIM, (h + 1) * HEAD_DIM)
            lane = d * n_heads + h
            gc.append(gcum[:, lane:lane + 1])
            gt.append(gtot[:, lane:lane + 1])
            bt.append(beta_all[:, 2 * n_heads + lane:2 * n_heads + lane + 1])
            decay.append(jnp.where(incl_d[d], jnp.exp(jnp.where(incl_d[d], gc[-1] - gcum_t[lane:lane + 1, 0:CHUNK], 0.0)), 0.0))
            kh.append(k_ref[0, :, sl])
            vh.append(v_ref[0, :, sl])
            if with_output:
                qh.append(q_ref[0, :, sl])
            strict.append(strict_d[d])
    n = range(len(chains))
    k16 = [x.astype(BF16) for x in kh]
    if with_output:
        kq = [_dot_nt(jnp.concatenate([k16[i], qh[i].astype(BF16)], axis=0), k16[i]) for i in n]
        kk = [x[:CHUNK] for x in kq]
        qk = [x[CHUNK:] for x in kq]
    else:
        kk = [_dot_nt(x, x) for x in k16]
    a_mat = [jnp.where(strict[i], bt[i] * decay[i] * kk[i], 0.0) for i in n]
    e = _unit_tri_inverse(a_mat, strict, row, col)
    eg = [jnp.exp(x) for x in gc]
    rhs = [jnp.concatenate([(bt[i] * eg[i]) * kh[i], bt[i] * vh[i]], axis=1) for i in n]
    sol = [rhs[i] + _dot(e[i].astype(BF16), rhs[i].astype(BF16)) for i in n]
    s_h = [s_scr[d, h] for d, h in chains]
    s16 = [x.astype(BF16) for x in s_h]
    if with_output:
        wq = [_dot(jnp.concatenate([sol[i][:, :HEAD_DIM].astype(BF16), (qh[i] * eg[i]).astype(BF16)], axis=0), s16[i])
              for i in n]
        ws = [x[:CHUNK] for x in wq]
        qs = [x[CHUNK:] for x in wq]
    else:
        ws = [_dot(sol[i][:, :HEAD_DIM].astype(BF16), s16[i]) for i in n]
    u16 = [(sol[i][:, HEAD_DIM:] - ws[i]).astype(BF16) for i in n]
    if with_output:
        for i, (d, h) in enumerate(chains):
            o_refs[d][0, :, h * HEAD_DIM:(h + 1) * HEAD_DIM] = qs[i] + _dot((qk[i] * decay[i]).astype(BF16), u16[i])
    for i, (d, h) in enumerate(chains):
        k_dec = kh[i] * jnp.exp(gt[i] - gc[i])
        s_scr[d, h] = jnp.exp(gt[i]) * s_h[i] + _dot_tn(k_dec.astype(BF16), u16[i])

    if not with_output:
        @pl.when(c == pl.num_programs(1) - 1)
        def _():
            sfin_ref[0] = s_scr[...]


def _delta(qkv, p3, gate_block, alog_r, dtb_r, n_heads, s0=None):
    b, length, _ = qkv.shape
    nch = length // CHUNK
    hd = n_heads * HEAD_DIM
    with_init = s0 is not None
    with_output = with_init

    in_specs, args = [], []
    for d in range(2):
        pos = (lambda c: c) if d == 0 else (lambda c: nch - 1 - c)
        in_specs += [pl.BlockSpec((1, CHUNK, hd), lambda i, c, pos=pos: (i, pos(c), 0)),
                     pl.BlockSpec((1, CHUNK, hd), lambda i, c, pos=pos: (i, pos(c), 1)),
                     pl.BlockSpec((1, CHUNK, hd), lambda i, c, pos=pos: (i, pos(c), 2)),
                     pl.BlockSpec((1, CHUNK, LANES), lambda i, c, pos=pos: (i, pos(c), gate_block))]
        args += [qkv, qkv, qkv, p3]
    in_specs += [pl.BlockSpec((1, LANES), lambda i, c: (0, 0))] * 2
    args += [alog_r, dtb_r]
    state_spec = pl.BlockSpec((1, 2, n_heads, HEAD_DIM, HEAD_DIM), lambda i, c: (i, 0, 0, 0, 0))
    if with_init:
        in_specs.append(state_spec)
        args.append(s0)
    if with_output:
        out_specs = [pl.BlockSpec((1, CHUNK, hd), lambda i, c: (i, c, 0)),
                     pl.BlockSpec((1, CHUNK, hd), lambda i, c: (i, nch - 1 - c, 0))]
        out_shape = [jax.ShapeDtypeStruct((b, length, hd), F32)] * 2
    else:
        out_specs = state_spec
        out_shape = jax.ShapeDtypeStruct((b, 2, n_heads, HEAD_DIM, HEAD_DIM), F32)
    return pl.pallas_call(
        functools.partial(_delta_kernel, n_heads=n_heads, with_init=with_init, with_output=with_output),
        grid=(b, nch),
        in_specs=in_specs,
        out_specs=out_specs,
        out_shape=out_shape,
        scratch_shapes=[pltpu.VMEM((2, n_heads, HEAD_DIM, HEAD_DIM), F32)],
        compiler_params=_params(("parallel", "arbitrary"), 40),
        name="delta_lat" if with_output else "delta_ctx",
    )(*args)


def _cmul(ar, ai, br, bi):
    return ar * br - ai * bi, ar * bi + ai * br


def _s5_tables(a_re, a_im, log_dt, b_re, b_im, c_re, c_im, d_skip):
    g, p, s = b_re.shape
    dt = jnp.exp(log_dt.astype(F32))[..., None]
    lr, li = a_re.astype(F32) * dt, a_im.astype(F32) * dt

    def powers(d, n):
        n = n.astype(F32)
        mag = jnp.exp(lr[d][..., None] * n)
        ang = li[d][..., None] * n
        return mag * jnp.cos(ang), mag * jnp.sin(ang)

    abr = jnp.exp(lr) * jnp.cos(li)
    abi = jnp.exp(lr) * jnp.sin(li)
    nr, ni = abr - 1.0, abi
    cr, ci = a_re.astype(F32), a_im.astype(F32)
    den = cr * cr + ci * ci
    fr, fi = (nr * cr + ni * ci) / den, (ni * cr - nr * ci) / den
    bbr, bbi = _cmul(fr[..., None], fi[..., None], b_re.astype(F32)[None], b_im.astype(F32)[None])
    ccr = jnp.swapaxes(c_re.astype(F32), 1, 2)
    cci = jnp.swapaxes(c_im.astype(F32), 1, 2)
    sub = jnp.arange(S5_SUB)

    def x_table(d, n):
        er, ei = powers(d, n)
        xr, xi = _cmul(bbr[d][:, :, None, :], bbi[d][:, :, None, :], er[..., None], ei[..., None])
        xr = jnp.transpose(xr, (0, 2, 3, 1)).reshape(g, S5_SUB * s, p)
        xi = jnp.transpose(xi, (0, 2, 3, 1)).reshape(g, S5_SUB * s, p)
        return jnp.concatenate([xr, xi], axis=-1)

    def y_table(d, n):
        er, ei = powers(d, n)
        yr, yi = _cmul(ccr[:, :, None, :], cci[:, :, None, :], er[..., None], ei[..., None])
        return jnp.concatenate([yr.reshape(g, p, S5_SUB * s), -yi.reshape(g, p, S5_SUB * s)], axis=1)

    xt = jnp.stack([x_table(0, S5_SUB - 1 - sub), x_table(1, sub)], axis=1)
    yt = jnp.stack([y_table(0, sub + 1), y_table(1, S5_SUB - sub)], axis=1)

    def near_taps(d, descending):
        er, ei = powers(d, sub)
        xr, xi = _cmul(bbr[d][:, :, None, :], bbi[d][:, :, None, :], er[..., None], ei[..., None])
        taps = (jnp.einsum('gpks,gpq->gskq', xr, ccr, precision=lax.Precision.HIGHEST)
                - jnp.einsum('gpks,gpq->gskq', xi, cci, precision=lax.Precision.HIGHEST))
        if descending:
            taps = taps[:, :, ::-1]
        return taps.reshape(g, s, S5_SUB * s)

    tt = jnp.stack([near_taps(0, False), near_taps(1, True),
                    jnp.tile(d_skip.astype(F32).reshape(g, 1, s), (1, s, S5_SUB))], axis=1)
    rows = []
    for d in range(2):
        er, ei = powers(d, S5_SUB * jnp.arange(CHUNK // S5_SUB + 1))
        for k in range(CHUNK // S5_SUB + 1):
            rows += [jnp.concatenate([er[..., k], er[..., k]], -1), jnp.concatenate([-ei[..., k], ei[..., k]], -1)]
    pw = jnp.stack(rows, axis=1)
    return xt, yt, tt, pw


def _s5_diag_block(tt):
    lane = lax.broadcasted_iota(jnp.int32, (S5_GROUP, LANES), 1)
    chan = lax.broadcasted_iota(jnp.int32, (S5_GROUP, LANES), 0)
    blocks = []
    for jj in range(S5_SUB):
        right = jj * S5_GROUP
        left = (S5_SUB - 1 - jj) * S5_GROUP
        fwd = tt[0] if right == 0 else jnp.where(lane >= right, pltpu.roll(tt[0], right, axis=1), 0.0)
        bwd = tt[1] if left == 0 else jnp.where(lane < LANES - left, pltpu.roll(tt[1], LANES - left, axis=1), 0.0)
        blocks.append(fwd + bwd + jnp.where(lane == right + chan, tt[2], 0.0))
    return jnp.concatenate(blocks, axis=0)


def _s5_kernel(uc_ref, u0_ref, u1_ref, xt_ref, yt_ref, tt_ref, pw_ref, y_ref, bst_scr, m_scr, h_scr, y_scr,
               *, nb, grid_rows):
    nsub = CHUNK // S5_SUB
    n_pw = 2 * (nsub + 1)
    n_lat = u0_ref.shape[1] // nb
    n_ctx = uc_ref.shape[1] // nb

    def hdot(a, b):
        return jnp.dot(a, b, precision=lax.Precision.HIGHEST, preferred_element_type=F32)

    def one_group(gi, carry):
        pw = pw_ref[gi]

        def crot(x, d, k, pw=pw):
            r0 = d * n_pw + 2 * k
            return pw[r0:r0 + 1] * x + pw[r0 + 1:r0 + 2] * pltpu.roll(x, LANES // 2, axis=1)

        xf, xb = xt_ref[gi, 0], xt_ref[gi, 1]
        xfs = [xf] + [crot(xf, 0, k) for k in range(1, nsub)]
        xbs = [xb] + [crot(xb, 1, k) for k in range(1, nsub)]
        yf1, yb1 = yt_ref[gi, 0], yt_ref[gi, 1]

        for j in range(nsub):
            bst_scr[j * LANES:(j + 1) * LANES, 0:LANES] = xfs[nsub - 1 - j].astype(BF16)
            bst_scr[j * LANES:(j + 1) * LANES, LANES:2 * LANES] = xbs[j].astype(BF16)
        bst = bst_scr[...]
        ul = jnp.concatenate([u0_ref[gi], u1_ref[gi]], axis=1)
        hc = _dot(uc_ref[gi], bst)
        hl = _dot(ul, bst)
        h_scr[0] = hl[:, 0:LANES]
        h_scr[1] = hl[:, LANES:2 * LANES]

        hf = jnp.zeros((nb, LANES), F32)
        hb = jnp.zeros((nb, LANES), F32)
        for c in range(n_ctx):
            hf = crot(hf, 0, nsub) + hc[c * nb:(c + 1) * nb, 0:LANES]
        for c in reversed(range(n_ctx)):
            hb = crot(hb, 1, nsub) + hc[c * nb:(c + 1) * nb, LANES:2 * LANES]
        for c in range(n_lat):
            rows_c = pl.ds(c, nb, stride=n_lat)
            inp = h_scr[0, rows_c, :]
            h_scr[0, rows_c, :] = hf
            hf = crot(hf, 0, nsub) + inp
        for c in reversed(range(n_lat)):
            rows_c = pl.ds(c, nb, stride=n_lat)
            inp = h_scr[1, rows_c, :]
            h_scr[1, rows_c, :] = hb
            hb = crot(hb, 1, nsub) + inp

        diag = _s5_diag_block(tt_ref[gi]).astype(BF16)
        taps_f = [None] + [hdot(xfs[dl - 1], yf1).astype(BF16) for dl in range(1, nsub)]
        taps_b = [None] + [hdot(xbs[dl - 1], yb1).astype(BF16) for dl in range(1, nsub)]
        for bj in range(nsub):
            for bi in range(nsub):
                blk = diag if bi == bj else (taps_f[bi - bj] if bi > bj else taps_b[bj - bi])
                m_scr[bj * LANES:(bj + 1) * LANES, bi * LANES:(bi + 1) * LANES] = blk
        y = _dot(ul, m_scr[...])
        hin_f, hin_b = h_scr[0], h_scr[1]
        yf16, yb16 = yf1.astype(BF16), yb1.astype(BF16)
        for bi in range(nsub):
            y_scr[gi, :, bi * LANES:(bi + 1) * LANES] = (
                y[:, bi * LANES:(bi + 1) * LANES]
                + _dot(crot(hin_f, 0, bi).astype(BF16), yf16)
                + _dot(crot(hin_b, 1, nsub - 1 - bi).astype(BF16), yb16))
        return carry

    lax.fori_loop(0, u0_ref.shape[0], one_group, 0)

    half = GRID_W // 2
    tokens = grid_rows * GRID_W
    for cc in range(2):
        for r in range(grid_rows):
            lane0 = (cc * grid_rows + r) * S5_GROUP
            piece = jnp.concatenate([y_scr[gi, :, lane0:lane0 + S5_GROUP] for gi in range(u0_ref.shape[0])], axis=1)
            for b in range(nb):
                y_ref[pl.ds(b * tokens + r * GRID_W + cc, half, stride=2), :] = piece[b * half:(b + 1) * half]


def _s5(uc, u0, u1, tables, nb, grid_rows):
    xt, yt, tt, pw = tables
    g = u0.shape[0]
    rows_l = u0.shape[1]
    kdim = 2 * u0.shape[2]
    tokens = grid_rows * GRID_W
    gstep = LANES // S5_GROUP

    def spec(a):
        nd = a.ndim
        return pl.BlockSpec((gstep,) + tuple(a.shape[1:]), lambda i: (i,) + (0,) * (nd - 1))

    arrays = [uc, u0, u1, xt, yt, tt, pw]
    return pl.pallas_call(
        functools.partial(_s5_kernel, nb=nb, grid_rows=grid_rows),
        grid=(g // gstep,),
        in_specs=[spec(a) for a in arrays],
        out_specs=pl.BlockSpec((nb * tokens, LANES), lambda i: (0, i)),
        out_shape=jax.ShapeDtypeStruct((nb * tokens, g * S5_GROUP), F32),
        scratch_shapes=[pltpu.VMEM((kdim, 2 * LANES), BF16), pltpu.VMEM((kdim, kdim), BF16),
                        pltpu.VMEM((2, rows_l, LANES), F32), pltpu.VMEM((gstep, rows_l, kdim), F32)],
        compiler_params=_params(("parallel",), 48),
        name="s5_scan",
    )(*arrays)


def _gelu_tanh(x):
    return 0.5 * x * (1.0 + jnp.tanh(0.7978845608028654 * (x + 0.044715 * (x * x * x))))


def _out_kernel(of_ref, ob_ref, z_ref, y_ref, x_ref, m_ref, dnw_ref, wglu_ref, wdn_ref, ws5_ref, out_ref, *, n_heads):
    o = of_ref[...] + ob_ref[...]
    z = z_ref[...]
    dnw = dnw_ref[...]
    parts = []
    for h in range(n_heads):
        sl = slice(h * HEAD_DIM, (h + 1) * HEAD_DIM)
        oh = o[:, sl]
        yh = oh * lax.rsqrt(jnp.mean(oh * oh, axis=-1, keepdims=True) + EPS) * dnw
        parts.append((yh * _silu(z[:, sl])).astype(BF16))
    dn = jnp.concatenate(parts, axis=1)
    t = _dot(_gelu_tanh(y_ref[...]).astype(BF16), wglu_ref[...])
    half = t.shape[1] // 2
    s5 = (t[:, :half] * _sigmoid(t[:, half:])).astype(BF16)
    acc = _dot(dn, wdn_ref[...]) + _dot(s5, ws5_ref[...])
    out_ref[...] = x_ref[...] + m_ref[0][5:6] * acc


def _out_proj(o_fwd, o_bwd, p2, z_block, y2, x2, mods, dn_norm, w_glu, w_out, n_heads, tokens_per_mod):
    t, d = x2.shape
    hd = n_heads * HEAD_DIM
    s5w = y2.shape[1]
    tm = 256
    tiles_per_mod = tokens_per_mod // tm
    return pl.pallas_call(
        functools.partial(_out_kernel, n_heads=n_heads),
        grid=(t // tm,),
        in_specs=[pl.BlockSpec((tm, hd), lambda i: (i, 0)),
                  pl.BlockSpec((tm, hd), lambda i: (i, 0)),
                  pl.BlockSpec((tm, hd), lambda i: (i, z_block)),
                  pl.BlockSpec((tm, s5w), lambda i: (i, 0)),
                  pl.BlockSpec((tm, d), lambda i: (i, 0)),
                  pl.BlockSpec((1, N_MOD, d), lambda i: (i // tiles_per_mod, 0, 0)),
                  pl.BlockSpec((1, HEAD_DIM), lambda i: (0, 0)),
                  pl.BlockSpec(w_glu.shape, lambda i: (0, 0)),
                  pl.BlockSpec((hd, d), lambda i: (0, 0)),
                  pl.BlockSpec((s5w, d), lambda i: (hd // s5w, 0))],
        out_specs=pl.BlockSpec((tm, d), lambda i: (i, 0)),
        out_shape=jax.ShapeDtypeStruct((t, d), F32),
        compiler_params=_params(("parallel",), 48),
        name="out_proj",
    )(o_fwd, o_bwd, p2, y2, x2, mods, dn_norm.reshape(1, HEAD_DIM), w_glu, w_out, w_out)


def _layer(x, ctx, m_lat, m_ctx, norm_ffn1, ffn1_up, ffn1_down, norm_mix, w_in, dn_conv, dn_a_log,
           dn_dt_bias, dn_norm, s5_a_re, s5_a_im, s5_log_dt, s5_b_re, s5_b_im, s5_c_re, s5_c_im, s5_d,
           s5_glu, w_out, norm_ffn2, ffn2_up, ffn2_down, final_norm):
    b, length, d = x.shape
    lc = ctx.shape[1]
    n_heads = dn_a_log.shape[1]
    hd = n_heads * HEAD_DIM
    n_conv = dn_conv.shape[-1]
    n_qk = n_conv - hd
    s5w = s5_d.shape[0]
    groups = s5w // S5_GROUP
    rows = length // GRID_W

    gate0 = n_conv + hd
    w_in16 = w_in.astype(BF16)
    w_s5 = w_in16[:, gate0 + 4 * n_heads:]
    n_main = gate0 + LANES
    z_block = n_conv // hd
    gate_block = gate0 // LANES
    lane_pad = jnp.zeros((LANES - 2 * n_heads,), F32)
    alog_r = jnp.concatenate([dn_a_log.astype(F32).reshape(-1), lane_pad]).reshape(1, LANES)
    dtb_r = jnp.concatenate([dn_dt_bias.astype(F32).reshape(-1), lane_pad]).reshape(1, LANES)
    conv_w9 = dn_conv.reshape(9, n_conv)

    x2 = x.reshape(b * length, d)
    c2 = ctx.reshape(b * lc, d)

    up1, down1 = ffn1_up.astype(BF16), ffn1_down.astype(BF16)
    x2 = _ffn(x2, m_lat, norm_ffn1, up1, down1, 0, length)
    c2 = _ffn(c2, m_ctx, norm_ffn1, up1, down1, 0, b * lc)

    p_lat, u_lat0, u_lat1 = _proj(x2, m_lat, norm_mix, w_in16, n_main, w_s5, 1, length, packed=True)
    p_ctx, u_ctx = _proj(c2, m_ctx, norm_mix, w_in16, n_main, w_s5, 1, b * lc, packed=False)
    p_lat3 = p_lat.reshape(b, length, n_main)
    p_ctx3 = p_ctx.reshape(b, lc, n_main)

    qkv_lat = _conv(p_lat3, conv_w9, rows, GRID_W, n_conv, n_qk)
    qkv_ctx = _conv(p_ctx3, conv_w9, 1, lc, n_conv, n_qk)
    s_ctx = _delta(qkv_ctx, p_ctx3, gate_block, alog_r, dtb_r, n_heads)
    o_fwd, o_bwd = _delta(qkv_lat, p_lat3, gate_block, alog_r, dtb_r, n_heads, s0=s_ctx)

    tables = _s5_tables(s5_a_re, s5_a_im, s5_log_dt, s5_b_re, s5_b_im, s5_c_re, s5_c_im, s5_d)
    u_ctx = u_ctx.astype(BF16).reshape(b, lc // CHUNK, CHUNK, groups, S5_GROUP)
    u_ctx = jnp.transpose(u_ctx, (3, 1, 0, 2, 4)).reshape(groups, (lc // CHUNK) * b, CHUNK * S5_GROUP)
    y2 = _s5(u_ctx, u_lat0, u_lat1, tables, b, rows)

    x2 = _out_proj(o_fwd.reshape(b * length, hd), o_bwd.reshape(b * length, hd), p_lat, z_block, y2, x2, m_lat, dn_norm,
                   s5_glu.astype(BF16), w_out.astype(BF16), n_heads, length)

    x2 = _ffn(x2, m_lat, norm_ffn2, ffn2_up.astype(BF16), ffn2_down.astype(BF16), 2, length,
              final_w=final_norm)
    return x2.reshape(b, length, d)


def kernel(x, c, ctx, c_ctx, w_mod, b_mod, norm_ffn1, ffn1_up, ffn1_down, norm_mix, w_in, dn_conv, dn_a_log, dn_dt_bias, dn_norm, s5_a_re, s5_a_im, s5_log_dt, s5_b_re, s5_b_im, s5_c_re, s5_c_im, s5_d, s5_glu, w_out, norm_ffn2, ffn2_up, ffn2_down, final_norm):
    depth = w_mod.shape[0]
    assert depth == 1, "the context stream update of deeper stacks is not implemented"
    b, _, d = x.shape
    cond = jnp.concatenate([c, c_ctx[None], jnp.zeros((16 - b - 1, d), c.dtype)], axis=0)
    m = _ada(cond, w_mod[0], b_mod[0]).reshape(16, N_MOD, d)
    return _layer(x, ctx, m[:b], m[b:b + 1], norm_ffn1[0], ffn1_up[0], ffn1_down[0], norm_mix[0], w_in[0],
                  dn_conv[0], dn_a_log[0], dn_dt_bias[0], dn_norm[0], s5_a_re[0], s5_a_im[0], s5_log_dt[0],
                  s5_b_re[0], s5_b_im[0], s5_c_re[0], s5_c_im[0], s5_d[0], s5_glu[0], w_out[0],
                  norm_ffn2[0], ffn2_up[0], ffn2_down[0], final_norm)
```

```python
import functools

import jax
import jax.numpy as jnp
from jax import lax
from jax.experimental import pallas as pl
from jax.experimental.pallas import tpu as pltpu

F32 = jnp.float32
BF16 = jnp.bfloat16
EPS = 1e-6
N_MOD = 9
GRID_W = 64
CHUNK = 64
HEAD_DIM = 128
S5_GROUP = 16
S5_SUB = 8
LANES = 128
MIB = 1024 * 1024


def _params(semantics, vmem_mib):
    return pltpu.CompilerParams(dimension_semantics=semantics, vmem_limit_bytes=vmem_mib * MIB)


def _sigmoid(x):
    return 1.0 / (1.0 + jnp.exp(-x))


def _silu(x):
    return x * _sigmoid(x)


def _dot(a, b):
    return jnp.dot(a, b, preferred_element_type=F32)


def _dot_nt(a, b):
    return lax.dot_general(a, b, (((1,), (1,)), ((), ())), preferred_element_type=F32)


def _dot_tn(a, b):
    return lax.dot_general(a, b, (((0,), (0,)), ((), ())), preferred_element_type=F32)


def _modnorm(x, norm_w, scale, shift):
    y = x * lax.rsqrt(jnp.mean(x * x, axis=-1, keepdims=True) + EPS) * norm_w
    return y * (1.0 + scale) + shift


def _ada_kernel(c_ref, w_ref, b_ref, o_ref):
    a = _silu(c_ref[...]).astype(BF16)
    o_ref[...] = _dot(a, w_ref[...].astype(BF16)) + b_ref[...]


def _ada(cond, w_mod, b_mod):
    rows, d = cond.shape
    n = w_mod.shape[1]
    tn = 1024
    return pl.pallas_call(
        _ada_kernel,
        grid=(n // tn,),
        in_specs=[pl.BlockSpec((rows, d), lambda j: (0, 0)),
                  pl.BlockSpec((d, tn), lambda j: (0, j)),
                  pl.BlockSpec((1, tn), lambda j: (0, j))],
        out_specs=pl.BlockSpec((rows, tn), lambda j: (0, j)),
        out_shape=jax.ShapeDtypeStruct((rows, n), F32),
        compiler_params=_params(("parallel",), 40),
        name="ada_mod",
    )(cond, w_mod, b_mod.reshape(1, n))


def _ffn_kernel(x_ref, m_ref, nw_ref, wg_ref, wu_ref, wd_ref, *rest, sub, final):
    if final:
        fin_ref, o_ref, h_scr, acc_scr = rest
    else:
        o_ref, h_scr, acc_scr = rest
    f = pl.program_id(1)

    @pl.when(f == 0)
    def _():
        m = m_ref[0]
        h = _modnorm(x_ref[...], nw_ref[...], m[3 * sub + 1:3 * sub + 2], m[3 * sub:3 * sub + 1])
        h_scr[...] = h.astype(BF16)
        acc_scr[...] = jnp.zeros_like(acc_scr)

    h = h_scr[...]
    g = _dot(h, wg_ref[...])
    u = _dot(h, wu_ref[...])
    acc_scr[...] += _dot((_silu(g) * u).astype(BF16), wd_ref[...])

    @pl.when(f == pl.num_programs(1) - 1)
    def _():
        m = m_ref[0]
        y = x_ref[...] + 0.5 * m[3 * sub + 2:3 * sub + 3] * acc_scr[...]
        if final:
            y = y * lax.rsqrt(jnp.mean(y * y, axis=-1, keepdims=True) + EPS) * fin_ref[...]
        o_ref[...] = y


def _ffn(x2, mods, norm_w, w_up, w_down, sub, tokens_per_mod, final_w=None):
    t, d = x2.shape
    f = w_down.shape[0]
    tm, tf = 512, 512
    nf = f // tf
    tiles_per_mod = tokens_per_mod // tm
    final = final_w is not None
    in_specs = [pl.BlockSpec((tm, d), lambda i, j: (i, 0)),
                pl.BlockSpec((1, N_MOD, d), lambda i, j: (i // tiles_per_mod, 0, 0)),
                pl.BlockSpec((1, d), lambda i, j: (0, 0)),
                pl.BlockSpec((d, tf), lambda i, j: (0, j)),
                pl.BlockSpec((d, tf), lambda i, j: (0, nf + j)),
                pl.BlockSpec((tf, d), lambda i, j: (j, 0))]
    args = [x2, mods, norm_w.reshape(1, d), w_up, w_up, w_down]
    if final:
        in_specs.append(pl.BlockSpec((1, d), lambda i, j: (0, 0)))
        args.append(final_w.reshape(1, d))
    return pl.pallas_call(
        functools.partial(_ffn_kernel, sub=sub, final=final),
        grid=(t // tm, nf),
        in_specs=in_specs,
        out_specs=pl.BlockSpec((tm, d), lambda i, j: (i, 0)),
        out_shape=jax.ShapeDtypeStruct((t, d), F32),
        scratch_shapes=[pltpu.VMEM((tm, d), BF16), pltpu.VMEM((tm, d), F32)],
        compiler_params=_params(("parallel", "arbitrary"), 48),
        name="ffn_final" if final else "ffn",
    )(*args)


def _proj_kernel(x_ref, m_ref, nw_ref, w_ref, ws5_ref, *rest, sub, packed):
    if packed:
        o_ref, u0_ref, u1_ref, h_scr, a_scr = rest
    else:
        o_ref, s5_ref, h_scr = rest

    @pl.when(pl.program_id(1) == 0)
    def _():
        m = m_ref[0]
        h = _modnorm(x_ref[...], nw_ref[...], m[3 * sub + 1:3 * sub + 2], m[3 * sub:3 * sub + 1])
        h_scr[...] = h.astype(BF16)
        s5 = _dot(h_scr[...], ws5_ref[...])
        if packed:
            _s5_pack(s5, a_scr, (u0_ref, u1_ref))
        else:
            s5_ref[...] = s5

    o_ref[...] = _dot(h_scr[...], w_ref[...])


def _s5_pack(a, a_scr, u_refs):
    tm, width = a.shape
    half = GRID_W // 2
    n_rl = tm // GRID_W
    for k in range(width // LANES):
        a_scr[k] = a[:, k * LANES:(k + 1) * LANES]
    for cc in range(2):
        acc = jnp.concatenate([a_scr[k, pl.ds(cc, tm // 2, stride=2), :] for k in range(width // LANES)], axis=1)
        for g in range(width // S5_GROUP):
            piece = jnp.concatenate([acc[rl * half:(rl + 1) * half, g * S5_GROUP:(g + 1) * S5_GROUP]
                                     for rl in range(n_rl)], axis=1)
            u_refs[cc][g] = piece.astype(BF16)


def _proj(x2, mods, norm_w, w, n_main, w_s5, sub, tokens_per_mod, packed):
    t, d = x2.shape
    tm, tn = 1024, 896
    assert n_main % tn == 0
    tiles_per_mod = tokens_per_mod // tm
    s5w = w_s5.shape[1]
    in_specs = [pl.BlockSpec((tm, d), lambda i, j: (i, 0)),
                pl.BlockSpec((1, N_MOD, d), lambda i, j: (i // tiles_per_mod, 0, 0)),
                pl.BlockSpec((1, d), lambda i, j: (0, 0)),
                pl.BlockSpec((d, tn), lambda i, j: (0, j)),
                pl.BlockSpec((d, s5w), lambda i, j: (0, 0))]
    args = [x2, mods, norm_w.reshape(1, d), w, w_s5]
    out_specs = [pl.BlockSpec((tm, tn), lambda i, j: (i, j))]
    out_shape = [jax.ShapeDtypeStruct((t, n_main), F32)]
    scratch = [pltpu.VMEM((tm, d), BF16)]
    if packed:
        groups = s5w // S5_GROUP
        half = GRID_W // 2
        lanes_per_tile = (tm // GRID_W) * S5_GROUP
        rows_total = tokens_per_mod // GRID_W
        n_b = t // tokens_per_mod
        u_spec = pl.BlockSpec((groups, half, lanes_per_tile), lambda i, j: (0, i // tiles_per_mod, i % tiles_per_mod))
        u_shape = jax.ShapeDtypeStruct((groups, n_b * half, rows_total * S5_GROUP), BF16)
        out_specs += [u_spec, u_spec]
        out_shape += [u_shape, u_shape]
        scratch.append(pltpu.VMEM((s5w // LANES, tm, LANES), F32))
    else:
        out_specs.append(pl.BlockSpec((tm, s5w), lambda i, j: (i, 0)))
        out_shape.append(jax.ShapeDtypeStruct((t, s5w), F32))
    return pl.pallas_call(
        functools.partial(_proj_kernel, sub=sub, packed=packed),
        grid=(t // tm, n_main // tn),
        in_specs=in_specs,
        out_specs=out_specs,
        out_shape=out_shape,
        scratch_shapes=scratch,
        compiler_params=_params(("parallel", "arbitrary"), 52),
        name="in_proj_s5" if packed else "in_proj",
    )(*args)


def _conv_kernel(p_ref, w_ref, o_ref, xp_scr, *, rows, width, strip, n_q_tiles, n_qk_tiles):
    length = rows * width
    pad = (xp_scr.shape[0] - length) // 2
    tc = p_ref.shape[2]
    j = pl.program_id(1)
    xp_scr[0:pad, :] = jnp.zeros((pad, tc), F32)
    xp_scr[pad + length:pad + length + pad, :] = jnp.zeros((pad, tc), F32)
    xp_scr[pad:pad + length, :] = p_ref[0]
    w = w_ref[...]
    q_scale = jnp.where(j < n_q_tiles, HEAD_DIM ** -0.5, 1.0).astype(F32)
    is_qk = j < n_qk_tiles
    row_taps = (-1, 0, 1) if rows > 1 else (0,)
    halo = 8
    for s in range(length // strip):
        t0 = s * strip
        col = (lax.broadcasted_iota(jnp.int32, (strip, 1), 0) + t0) & (width - 1)
        base = pad + t0 - halo
        acc = jnp.zeros((strip, tc), F32)
        for dc in (-1, 0, 1):
            z = jnp.zeros((strip + 2 * halo, tc), F32)
            for dr in row_taps:
                tap = (dr + 1) * 3 + (dc + 1)
                z = z + xp_scr[base + dr * width:base + dr * width + strip + 2 * halo, :] * w[tap:tap + 1, :]
            z = z[halo + dc:halo + dc + strip, :]
            if dc == -1:
                z = jnp.where(col >= 1, z, 0.0)
            elif dc == 1:
                z = jnp.where(col <= width - 2, z, 0.0)
            acc = acc + z
        y = _silu(acc)
        for hh in range(tc // HEAD_DIM):
            yh = y[:, hh * HEAD_DIM:(hh + 1) * HEAD_DIM]
            inv = lax.rsqrt(jnp.sum(yh * yh, axis=-1, keepdims=True) + EPS) * q_scale
            o_ref[0, t0:t0 + strip, hh * HEAD_DIM:(hh + 1) * HEAD_DIM] = yh * jnp.where(is_qk, inv, 1.0)


def _conv(p3, conv_w9, rows, width, n_conv, n_qk):
    b, length, _ = p3.shape
    tc = 512
    pad = width + 8 if rows > 1 else 8
    strip = min(256, length)
    kern = functools.partial(_conv_kernel, rows=rows, width=width, strip=strip,
                             n_q_tiles=(n_qk // 2) // tc, n_qk_tiles=n_qk // tc)
    return pl.pallas_call(
        kern,
        grid=(b, n_conv // tc),
        in_specs=[pl.BlockSpec((1, length, tc), lambda i, j: (i, 0, j)),
                  pl.BlockSpec((9, tc), lambda i, j: (0, j))],
        out_specs=pl.BlockSpec((1, length, tc), lambda i, j: (i, 0, j)),
        out_shape=jax.ShapeDtypeStruct((b, length, n_conv), F32),
        scratch_shapes=[pltpu.VMEM((length + 2 * pad, tc), F32)],
        compiler_params=_params(("parallel", "parallel"), 40),
        name="grid_conv",
    )(p3, conv_w9)


def _unit_tri_inverse(a_list, strict_list, row, col):
    n = 1
    e = None
    while n < CHUNK:
        k = n.bit_length() - 1
        pair = ((row >> (k + 1)) == (col >> (k + 1))) & ((row >> k) != (col >> k))
        l_n = [jnp.where(pair & st, a, 0.0) for a, st in zip(a_list, strict_list)]
        if e is None:
            e = [-l for l in l_n]
        else:
            e16 = [x.astype(BF16) for x in e]
            y = [l + _dot(l.astype(BF16), x16) for l, x16 in zip(l_n, e16)]
            e = [x - yy - _dot(x16, yy.astype(BF16)) for x, x16, yy in zip(e, e16, y)]
        n *= 2
    return e


def _delta_kernel(*refs, n_heads, with_init, with_output):
    chunk_refs = (refs[0:4], refs[4:8])
    alog_ref, dtb_ref = refs[8:10]
    rest = refs[10:]
    if with_init:
        s0_ref, rest = rest[0], rest[1:]
    if with_output:
        o_refs, s_scr = rest[0:2], rest[2]
    else:
        sfin_ref, s_scr = rest
    c = pl.program_id(1)

    @pl.when(c == 0)
    def _():
        if with_init:
            s_scr[...] = s0_ref[0]
        else:
            s_scr[...] = jnp.zeros_like(s_scr)

    row = lax.broadcasted_iota(jnp.int32, (CHUNK, CHUNK), 0)
    col = lax.broadcasted_iota(jnp.int32, (CHUNK, CHUNK), 1)
    incl_d = (row >= col, row <= col)
    strict_d = (row > col, row < col)

    chains = [(d, h) for d in range(2) for h in range(n_heads)]
    gc, gt, bt, decay, kh, vh, qh, strict = [], [], [], [], [], [], [], []
    for d in range(2):
        q_ref, k_ref, v_ref, gb_ref = chunk_refs[d]
        gb = gb_ref[0]
        x = gb + dtb_ref[...]
        softplus = jnp.maximum(x, 0.0) + jnp.log1p(jnp.exp(-jnp.abs(x)))
        g_all = -jnp.exp(alog_ref[...]) * softplus
        beta_all = _sigmoid(gb)
        gcum = jnp.dot(incl_d[d].astype(F32), g_all, precision=lax.Precision.HIGHEST,
                       preferred_element_type=F32)
        gtot = jnp.sum(g_all, axis=0, keepdims=True)
        gcum_t = jnp.concatenate([gcum, jnp.zeros_like(gcum)], axis=0).T
        for h in range(n_heads):
            sl = slice(h * HEAD_DIM, (h + 1) * HEAD_DIM)
            lane = d * n_heads + h
            gc.append(gcum[:, lane:lane + 1])
            gt.append(gtot[:, lane:lane + 1])
            bt.append(beta_all[:, 2 * n_heads + lane:2 * n_heads + lane + 1])
            decay.append(jnp.where(incl_d[d], jnp.exp(jnp.where(incl_d[d], gc[-1] - gcum_t[lane:lane + 1, 0:CHUNK], 0.0)), 0.0))
            kh.append(k_ref[0, :, sl])
            vh.append(v_ref[0, :, sl])
            if with_output:
                qh.append(q_ref[0, :, sl])
            strict.append(strict_d[d])
    n = range(len(chains))
    k16 = [x.astype(BF16) for x in kh]
    if with_output:
        kq = [_dot_nt(jnp.concatenate([k16[i], qh[i].astype(BF16)], axis=0), k16[i]) for i in n]
        kk = [x[:CHUNK] for x in kq]
        qk = [x[CHUNK:] for x in kq]
    else:
        kk = [_dot_nt(x, x) for x in k16]
    a_mat = [jnp.where(strict[i], bt[i] * decay[i] * kk[i], 0.0) for i in n]
    e = _unit_tri_inverse(a_mat, strict, row, col)
    eg = [jnp.exp(x) for x in gc]
    rhs = [jnp.concatenate([(bt[i] * eg[i]) * kh[i], bt[i] * vh[i]], axis=1) for i in n]
    sol = [rhs[i] + _dot(e[i].astype(BF16), rhs[i].astype(BF16)) for i in n]
    s_h = [s_scr[d, h] for d, h in chains]
    s16 = [x.astype(BF16) for x in s_h]
    if with_output:
        wq = [_dot(jnp.concatenate([sol[i][:, :HEAD_DIM].astype(BF16), (qh[i] * eg[i]).astype(BF16)], axis=0), s16[i])
              for i in n]
        ws = [x[:CHUNK] for x in wq]
        qs = [x[CHUNK:] for x in wq]
    else:
        ws = [_dot(sol[i][:, :HEAD_DIM].astype(BF16), s16[i]) for i in n]
    u16 = [(sol[i][:, HEAD_DIM:] - ws[i]).astype(BF16) for i in n]
    if with_output:
        for i, (d, h) in enumerate(chains):
            o = qs[i] + _dot((qk[i] * decay[i]).astype(BF16), u16[i])
            o_refs[d][0, :, h * HEAD_DIM:(h + 1) * HEAD_DIM] = o.astype(o_refs[d].dtype)
    for i, (d, h) in enumerate(chains):
        k_dec = kh[i] * jnp.exp(gt[i] - gc[i])
        s_scr[d, h] = jnp.exp(gt[i]) * s_h[i] + _dot_tn(k_dec.astype(BF16), u16[i])

    if not with_output:
        @pl.when(c == pl.num_programs(1) - 1)
        def _():
            sfin_ref[0] = s_scr[...]


def _delta(qkv, p3, gate_block, alog_r, dtb_r, n_heads, s0=None):
    b, length, _ = qkv.shape
    nch = length // CHUNK
    hd = n_heads * HEAD_DIM
    with_init = s0 is not None
    with_output = with_init

    in_specs, args = [], []
    for d in range(2):
        pos = (lambda c: c) if d == 0 else (lambda c: nch - 1 - c)
        in_specs += [pl.BlockSpec((1, CHUNK, hd), lambda i, c, pos=pos: (i, pos(c), 0)),
                     pl.BlockSpec((1, CHUNK, hd), lambda i, c, pos=pos: (i, pos(c), 1)),
                     pl.BlockSpec((1, CHUNK, hd), lambda i, c, pos=pos: (i, pos(c), 2)),
                     pl.BlockSpec((1, CHUNK, LANES), lambda i, c, pos=pos: (i, pos(c), gate_block))]
        args += [qkv, qkv, qkv, p3]
    in_specs += [pl.BlockSpec((1, LANES), lambda i, c: (0, 0))] * 2
    args += [alog_r, dtb_r]
    state_spec = pl.BlockSpec((1, 2, n_heads, HEAD_DIM, HEAD_DIM), lambda i, c: (i, 0, 0, 0, 0))
    if with_init:
        in_specs.append(state_spec)
        args.append(s0)
    if with_output:
        out_specs = [pl.BlockSpec((1, CHUNK, hd), lambda i, c: (i, c, 0)),
                     pl.BlockSpec((1, CHUNK, hd), lambda i, c: (i, nch - 1 - c, 0))]
        out_shape = [jax.ShapeDtypeStruct((b, length, hd), BF16)] * 2
    else:
        out_specs = state_spec
        out_shape = jax.ShapeDtypeStruct((b, 2, n_heads, HEAD_DIM, HEAD_DIM), F32)
    return pl.pallas_call(
        functools.partial(_delta_kernel, n_heads=n_heads, with_init=with_init, with_output=with_output),
        grid=(b, nch),
        in_specs=in_specs,
        out_specs=out_specs,
        out_shape=out_shape,
        scratch_shapes=[pltpu.VMEM((2, n_heads, HEAD_DIM, HEAD_DIM), F32)],
        compiler_params=_params(("parallel", "arbitrary"), 40),
        name="delta_lat" if with_output else "delta_ctx",
    )(*args)


def _cmul(ar, ai, br, bi):
    return ar * br - ai * bi, ar * bi + ai * br


def _s5_tables(a_re, a_im, log_dt, b_re, b_im, c_re, c_im, d_skip):
    g, p, s = b_re.shape
    dt = jnp.exp(log_dt.astype(F32))[..., None]
    lr, li = a_re.astype(F32) * dt, a_im.astype(F32) * dt

    def powers(d, n):
        n = n.astype(F32)
        mag = jnp.exp(lr[d][..., None] * n)
        ang = li[d][..., None] * n
        return mag * jnp.cos(ang), mag * jnp.sin(ang)

    abr = jnp.exp(lr) * jnp.cos(li)
    abi = jnp.exp(lr) * jnp.sin(li)
    nr, ni = abr - 1.0, abi
    cr, ci = a_re.astype(F32), a_im.astype(F32)
    den = cr * cr + ci * ci
    fr, fi = (nr * cr + ni * ci) / den, (ni * cr - nr * ci) / den
    bbr, bbi = _cmul(fr[..., None], fi[..., None], b_re.astype(F32)[None], b_im.astype(F32)[None])
    ccr = jnp.swapaxes(c_re.astype(F32), 1, 2)
    cci = jnp.swapaxes(c_im.astype(F32), 1, 2)
    sub = jnp.arange(S5_SUB)

    def x_table(d, n):
        er, ei = powers(d, n)
        xr, xi = _cmul(bbr[d][:, :, None, :], bbi[d][:, :, None, :], er[..., None], ei[..., None])
        xr = jnp.transpose(xr, (0, 2, 3, 1)).reshape(g, S5_SUB * s, p)
        xi = jnp.transpose(xi, (0, 2, 3, 1)).reshape(g, S5_SUB * s, p)
        return jnp.concatenate([xr, xi], axis=-1)

    def y_table(d, n):
        er, ei = powers(d, n)
        yr, yi = _cmul(ccr[:, :, None, :], cci[:, :, None, :], er[..., None], ei[..., None])
        return jnp.concatenate([yr.reshape(g, p, S5_SUB * s), -yi.reshape(g, p, S5_SUB * s)], axis=1)

    xt = jnp.stack([x_table(0, S5_SUB - 1 - sub), x_table(1, sub)], axis=1)
    yt = jnp.stack([y_table(0, sub + 1), y_table(1, S5_SUB - sub)], axis=1)

    def near_taps(d, descending):
        er, ei = powers(d, sub)
        xr, xi = _cmul(bbr[d][:, :, None, :], bbi[d][:, :, None, :], er[..., None], ei[..., None])
        taps = (jnp.einsum('gpks,gpq->gskq', xr, ccr, precision=lax.Precision.HIGHEST)
                - jnp.einsum('gpks,gpq->gskq', xi, cci, precision=lax.Precision.HIGHEST))
        if descending:
            taps = taps[:, :, ::-1]
        return taps.reshape(g, s, S5_SUB * s)

    tt = jnp.stack([near_taps(0, False), near_taps(1, True),
                    jnp.tile(d_skip.astype(F32).reshape(g, 1, s), (1, s, S5_SUB))], axis=1)
    rows = []
    for d in range(2):
        er, ei = powers(d, S5_SUB * jnp.arange(CHUNK // S5_SUB + 1))
        for k in range(CHUNK // S5_SUB + 1):
            rows += [jnp.concatenate([er[..., k], er[..., k]], -1), jnp.concatenate([-ei[..., k], ei[..., k]], -1)]
    pw = jnp.stack(rows, axis=1)
    return xt, yt, tt, pw


def _s5_diag_block(tt):
    lane = lax.broadcasted_iota(jnp.int32, (S5_GROUP, LANES), 1)
    chan = lax.broadcasted_iota(jnp.int32, (S5_GROUP, LANES), 0)
    blocks = []
    for jj in range(S5_SUB):
        right = jj * S5_GROUP
        left = (S5_SUB - 1 - jj) * S5_GROUP
        fwd = tt[0] if right == 0 else jnp.where(lane >= right, pltpu.roll(tt[0], right, axis=1), 0.0)
        bwd = tt[1] if left == 0 else jnp.where(lane < LANES - left, pltpu.roll(tt[1], LANES - left, axis=1), 0.0)
        blocks.append(fwd + bwd + jnp.where(lane == right + chan, tt[2], 0.0))
    return jnp.concatenate(blocks, axis=0)


def _s5_kernel(uc_ref, u0_ref, u1_ref, xt_ref, yt_ref, tt_ref, pw_ref, y_ref, bst_scr, m_scr, h_scr, y_scr,
               *, nb, grid_rows):
    nsub = CHUNK // S5_SUB
    n_pw = 2 * (nsub + 1)
    n_lat = u0_ref.shape[1] // nb
    n_ctx = uc_ref.shape[1] // nb

    def hdot(a, b):
        return jnp.dot(a, b, precision=lax.Precision.HIGHEST, preferred_element_type=F32)

    def one_group(gi, carry):
        pw = pw_ref[gi]

        def crot(x, d, k, pw=pw):
            r0 = d * n_pw + 2 * k
            return pw[r0:r0 + 1] * x + pw[r0 + 1:r0 + 2] * pltpu.roll(x, LANES // 2, axis=1)

        xf, xb = xt_ref[gi, 0], xt_ref[gi, 1]
        xfs = [xf] + [crot(xf, 0, k) for k in range(1, nsub)]
        xbs = [xb] + [crot(xb, 1, k) for k in range(1, nsub)]
        yf1, yb1 = yt_ref[gi, 0], yt_ref[gi, 1]

        for j in range(nsub):
            bst_scr[j * LANES:(j + 1) * LANES, 0:LANES] = xfs[nsub - 1 - j].astype(BF16)
            bst_scr[j * LANES:(j + 1) * LANES, LANES:2 * LANES] = xbs[j].astype(BF16)
        bst = bst_scr[...]
        ul = jnp.concatenate([u0_ref[gi], u1_ref[gi]], axis=1)
        hc = _dot(uc_ref[gi], bst)
        hl = _dot(ul, bst)
        h_scr[0] = hl[:, 0:LANES]
        h_scr[1] = hl[:, LANES:2 * LANES]

        hf = jnp.zeros((nb, LANES), F32)
        hb = jnp.zeros((nb, LANES), F32)
        for c in range(n_ctx):
            hf = crot(hf, 0, nsub) + hc[c * nb:(c + 1) * nb, 0:LANES]
        for c in reversed(range(n_ctx)):
            hb = crot(hb, 1, nsub) + hc[c * nb:(c + 1) * nb, LANES:2 * LANES]
        for c in range(n_lat):
            rows_c = pl.ds(c, nb, stride=n_lat)
            inp = h_scr[0, rows_c, :]
            h_scr[0, rows_c, :] = hf
            hf = crot(hf, 0, nsub) + inp
        for c in reversed(range(n_lat)):
            rows_c = pl.ds(c, nb, stride=n_lat)
            inp = h_scr[1, rows_c, :]
            h_scr[1, rows_c, :] = hb
            hb = crot(hb, 1, nsub) + inp

        diag = _s5_diag_block(tt_ref[gi]).astype(BF16)
        taps_f = [None] + [hdot(xfs[dl - 1], yf1).astype(BF16) for dl in range(1, nsub)]
        taps_b = [None] + [hdot(xbs[dl - 1], yb1).astype(BF16) for dl in range(1, nsub)]
        for bj in range(nsub):
            for bi in range(nsub):
                blk = diag if bi == bj else (taps_f[bi - bj] if bi > bj else taps_b[bj - bi])
                m_scr[bj * LANES:(bj + 1) * LANES, bi * LANES:(bi + 1) * LANES] = blk
        y = _dot(ul, m_scr[...])
        hin_f, hin_b = h_scr[0], h_scr[1]
        yf16, yb16 = yf1.astype(BF16), yb1.astype(BF16)
        for bi in range(nsub):
            y_scr[gi, :, bi * LANES:(bi + 1) * LANES] = (
                y[:, bi * LANES:(bi + 1) * LANES]
                + _dot(crot(hin_f, 0, bi).astype(BF16), yf16)
                + _dot(crot(hin_b, 1, nsub - 1 - bi).astype(BF16), yb16))
        return carry

    lax.fori_loop(0, u0_ref.shape[0], one_group, 0)

    half = GRID_W // 2
    tokens = grid_rows * GRID_W
    for cc in range(2):
        for r in range(grid_rows):
            lane0 = (cc * grid_rows + r) * S5_GROUP
            piece = jnp.concatenate([y_scr[gi, :, lane0:lane0 + S5_GROUP] for gi in range(u0_ref.shape[0])], axis=1)
            for b in range(nb):
                y_ref[pl.ds(b * tokens + r * GRID_W + cc, half, stride=2), :] = piece[b * half:(b + 1) * half]


def _s5(uc, u0, u1, tables, nb, grid_rows):
    xt, yt, tt, pw = tables
    g = u0.shape[0]
    rows_l = u0.shape[1]
    kdim = 2 * u0.shape[2]
    tokens = grid_rows * GRID_W
    gstep = LANES // S5_GROUP

    def spec(a):
        nd = a.ndim
        return pl.BlockSpec((gstep,) + tuple(a.shape[1:]), lambda i: (i,) + (0,) * (nd - 1))

    arrays = [uc, u0, u1, xt, yt, tt, pw]
    return pl.pallas_call(
        functools.partial(_s5_kernel, nb=nb, grid_rows=grid_rows),
        grid=(g // gstep,),
        in_specs=[spec(a) for a in arrays],
        out_specs=pl.BlockSpec((nb * tokens, LANES), lambda i: (0, i)),
        out_shape=jax.ShapeDtypeStruct((nb * tokens, g * S5_GROUP), F32),
        scratch_shapes=[pltpu.VMEM((kdim, 2 * LANES), BF16), pltpu.VMEM((kdim, kdim), BF16),
                        pltpu.VMEM((2, rows_l, LANES), F32), pltpu.VMEM((gstep, rows_l, kdim), F32)],
        compiler_params=_params(("parallel",), 48),
        name="s5_scan",
    )(*arrays)


def _gelu_tanh(x):
    return 0.5 * x * (1.0 + jnp.tanh(0.7978845608028654 * (x + 0.044715 * (x * x * x))))


def _out_kernel(of_ref, ob_ref, z_ref, y_ref, x_ref, m_ref, dnw_ref, wglu_ref, wdn_ref, ws5_ref, out_ref, *, n_heads):
    o = of_ref[...].astype(F32) + ob_ref[...].astype(F32)
    z = z_ref[...]
    dnw = dnw_ref[...]
    parts = []
    for h in range(n_heads):
        sl = slice(h * HEAD_DIM, (h + 1) * HEAD_DIM)
        oh = o[:, sl]
        yh = oh * lax.rsqrt(jnp.mean(oh * oh, axis=-1, keepdims=True) + EPS) * dnw
        parts.append((yh * _silu(z[:, sl])).astype(BF16))
    dn = jnp.concatenate(parts, axis=1)
    t = _dot(_gelu_tanh(y_ref[...]).astype(BF16), wglu_ref[...])
    half = t.shape[1] // 2
    s5 = (t[:, :half] * _sigmoid(t[:, half:])).astype(BF16)
    acc = _dot(dn, wdn_ref[...]) + _dot(s5, ws5_ref[...])
    out_ref[...] = x_ref[...] + m_ref[0][5:6] * acc


def _out_proj(o_fwd, o_bwd, p2, z_block, y2, x2, mods, dn_norm, w_glu, w_out, n_heads, tokens_per_mod):
    t, d = x2.shape
    hd = n_heads * HEAD_DIM
    s5w = y2.shape[1]
    tm = 256
    tiles_per_mod = tokens_per_mod // tm
    return pl.pallas_call(
        functools.partial(_out_kernel, n_heads=n_heads),
        grid=(t // tm,),
        in_specs=[pl.BlockSpec((tm, hd), lambda i: (i, 0)),
                  pl.BlockSpec((tm, hd), lambda i: (i, 0)),
                  pl.BlockSpec((tm, hd), lambda i: (i, z_block)),
                  pl.BlockSpec((tm, s5w), lambda i: (i, 0)),
                  pl.BlockSpec((tm, d), lambda i: (i, 0)),
                  pl.BlockSpec((1, N_MOD, d), lambda i: (i // tiles_per_mod, 0, 0)),
                  pl.BlockSpec((1, HEAD_DIM), lambda i: (0, 0)),
                  pl.BlockSpec(w_glu.shape, lambda i: (0, 0)),
                  pl.BlockSpec((hd, d), lambda i: (0, 0)),
                  pl.BlockSpec((s5w, d), lambda i: (hd // s5w, 0))],
        out_specs=pl.BlockSpec((tm, d), lambda i: (i, 0)),
        out_shape=jax.ShapeDtypeStruct((t, d), F32),
        compiler_params=_params(("parallel",), 48),
        name="out_proj",
    )(o_fwd, o_bwd, p2, y2, x2, mods, dn_norm.reshape(1, HEAD_DIM), w_glu, w_out, w_out)


def _layer(x, ctx, m_lat, m_ctx, norm_ffn1, ffn1_up, ffn1_down, norm_mix, w_in, dn_conv, dn_a_log,
           dn_dt_bias, dn_norm, s5_a_re, s5_a_im, s5_log_dt, s5_b_re, s5_b_im, s5_c_re, s5_c_im, s5_d,
           s5_glu, w_out, norm_ffn2, ffn2_up, ffn2_down, final_norm):
    b, length, d = x.shape
    lc = ctx.shape[1]
    n_heads = dn_a_log.shape[1]
    hd = n_heads * HEAD_DIM
    n_conv = dn_conv.shape[-1]
    n_qk = n_conv - hd
    s5w = s5_d.shape[0]
    groups = s5w // S5_GROUP
    rows = length // GRID_W

    gate0 = n_conv + hd
    w_in16 = w_in.astype(BF16)
    w_s5 = w_in16[:, gate0 + 4 * n_heads:]
    n_main = gate0 + LANES
    z_block = n_conv // hd
    gate_block = gate0 // LANES
    lane_pad = jnp.zeros((LANES - 2 * n_heads,), F32)
    alog_r = jnp.concatenate([dn_a_log.astype(F32).reshape(-1), lane_pad]).reshape(1, LANES)
    dtb_r = jnp.concatenate([dn_dt_bias.astype(F32).reshape(-1), lane_pad]).reshape(1, LANES)
    conv_w9 = dn_conv.reshape(9, n_conv)

    x2 = x.reshape(b * length, d)
    c2 = ctx.reshape(b * lc, d)

    up1, down1 = ffn1_up.astype(BF16), ffn1_down.astype(BF16)
    x2 = _ffn(x2, m_lat, norm_ffn1, up1, down1, 0, length)
    c2 = _ffn(c2, m_ctx, norm_ffn1, up1, down1, 0, b * lc)

    p_lat, u_lat0, u_lat1 = _proj(x2, m_lat, norm_mix, w_in16, n_main, w_s5, 1, length, packed=True)
    p_ctx, u_ctx = _proj(c2, m_ctx, norm_mix, w_in16, n_main, w_s5, 1, b * lc, packed=False)
    p_lat3 = p_lat.reshape(b, length, n_main)
    p_ctx3 = p_ctx.reshape(b, lc, n_main)

    qkv_lat = _conv(p_lat3, conv_w9, rows, GRID_W, n_conv, n_qk)
    qkv_ctx = _conv(p_ctx3, conv_w9, 1, lc, n_conv, n_qk)
    s_ctx = _delta(qkv_ctx, p_ctx3, gate_block, alog_r, dtb_r, n_heads)
    o_fwd, o_bwd = _delta(qkv_lat, p_lat3, gate_block, alog_r, dtb_r, n_heads, s0=s_ctx)

    tables = _s5_tables(s5_a_re, s5_a_im, s5_log_dt, s5_b_re, s5_b_im, s5_c_re, s5_c_im, s5_d)
    u_ctx = u_ctx.astype(BF16).reshape(b, lc // CHUNK, CHUNK, groups, S5_GROUP)
    u_ctx = jnp.transpose(u_ctx, (3, 1, 0, 2, 4)).reshape(groups, (lc // CHUNK) * b, CHUNK * S5_GROUP)
    y2 = _s5(u_ctx, u_lat0, u_lat1, tables, b, rows)

    x2 = _out_proj(o_fwd.reshape(b * length, hd), o_bwd.reshape(b * length, hd), p_lat, z_block, y2, x2, m_lat, dn_norm,
                   s5_glu.astype(BF16), w_out.astype(BF16), n_heads, length)

    x2 = _ffn(x2, m_lat, norm_ffn2, ffn2_up.astype(BF16), ffn2_down.astype(BF16), 2, length,
              final_w=final_norm)
    return x2.reshape(b, length, d)


def kernel(x, c, ctx, c_ctx, w_mod, b_mod, norm_ffn1, ffn1_up, ffn1_down, norm_mix, w_in, dn_conv, dn_a_log, dn_dt_bias, dn_norm, s5_a_re, s5_a_im, s5_log_dt, s5_b_re, s5_b_im, s5_c_re, s5_c_im, s5_d, s5_glu, w_out, norm_ffn2, ffn2_up, ffn2_down, final_norm):
    depth = w_mod.shape[0]
    assert depth == 1, "the context stream update of deeper stacks is not implemented"
    b, _, d = x.shape
    cond = jnp.concatenate([c, c_ctx[None], jnp.zeros((16 - b - 1, d), c.dtype)], axis=0)
    m = _ada(cond, w_mod[0], b_mod[0]).reshape(16, N_MOD, d)
    return _layer(x, ctx, m[:b], m[b:b + 1], norm_ffn1[0], ffn1_up[0], ffn1_down[0], norm_mix[0], w_in[0],
                  dn_conv[0], dn_a_log[0], dn_dt_bias[0], dn_norm[0], s5_a_re[0], s5_a_im[0], s5_log_dt[0],
                  s5_b_re[0], s5_b_im[0], s5_c_re[0], s5_c_im[0], s5_d[0], s5_glu[0], w_out[0],
                  norm_ffn2[0], ffn2_up[0], ffn2_down[0], final_norm)
```

```python
import functools

import jax
import jax.numpy as jnp
from jax import lax
from jax.experimental import pallas as pl
from jax.experimental.pallas import tpu as pltpu

F32 = jnp.float32
BF16 = jnp.bfloat16
EPS = 1e-6
N_MOD = 9
GRID_W = 64
CHUNK = 64
HEAD_DIM = 128
S5_GROUP = 16
S5_SUB = 8
LANES = 128
MIB = 1024 * 1024


def _params(semantics, vmem_mib):
    return pltpu.CompilerParams(dimension_semantics=semantics, vmem_limit_bytes=vmem_mib * MIB)


def _sigmoid(x):
    return 1.0 / (1.0 + jnp.exp(-x))


def _silu(x):
    return x * _sigmoid(x)


def _dot(a, b):
    return jnp.dot(a, b, preferred_element_type=F32)


def _dot_nt(a, b):
    return lax.dot_general(a, b, (((1,), (1,)), ((), ())), preferred_element_type=F32)


def _dot_tn(a, b):
    return lax.dot_general(a, b, (((0,), (0,)), ((), ())), preferred_element_type=F32)


def _modnorm(x, norm_w, scale, shift):
    y = x * lax.rsqrt(jnp.mean(x * x, axis=-1, keepdims=True) + EPS) * norm_w
    return y * (1.0 + scale) + shift


def _ada_kernel(c_ref, w_ref, b_ref, o_ref):
    a = _silu(c_ref[...]).astype(BF16)
    o_ref[...] = _dot(a, w_ref[...].astype(BF16)) + b_ref[...]


def _ada(cond, w_mod, b_mod):
    rows, d = cond.shape
    n = w_mod.shape[1]
    tn = 1024
    return pl.pallas_call(
        _ada_kernel,
        grid=(n // tn,),
        in_specs=[pl.BlockSpec((rows, d), lambda j: (0, 0)),
                  pl.BlockSpec((d, tn), lambda j: (0, j)),
                  pl.BlockSpec((1, tn), lambda j: (0, j))],
        out_specs=pl.BlockSpec((rows, tn), lambda j: (0, j)),
        out_shape=jax.ShapeDtypeStruct((rows, n), F32),
        compiler_params=_params(("parallel",), 40),
        name="ada_mod",
    )(cond, w_mod, b_mod.reshape(1, n))


def _ffn_kernel(x_ref, m_ref, nw_ref, wg_ref, wu_ref, wd_ref, *rest, sub, final):
    if final:
        fin_ref, o_ref, h_scr, acc_scr = rest
    else:
        o_ref, h_scr, acc_scr = rest
    f = pl.program_id(1)

    @pl.when(f == 0)
    def _():
        m = m_ref[0]
        h = _modnorm(x_ref[...], nw_ref[...], m[3 * sub + 1:3 * sub + 2], m[3 * sub:3 * sub + 1])
        h_scr[...] = h.astype(BF16)
        acc_scr[...] = jnp.zeros_like(acc_scr)

    h = h_scr[...]
    g = _dot(h, wg_ref[...])
    u = _dot(h, wu_ref[...])
    acc_scr[...] += _dot((_silu(g) * u).astype(BF16), wd_ref[...])

    @pl.when(f == pl.num_programs(1) - 1)
    def _():
        m = m_ref[0]
        y = x_ref[...] + 0.5 * m[3 * sub + 2:3 * sub + 3] * acc_scr[...]
        if final:
            y = y * lax.rsqrt(jnp.mean(y * y, axis=-1, keepdims=True) + EPS) * fin_ref[...]
        o_ref[...] = y


def _ffn(x2, mods, norm_w, w_up, w_down, sub, tokens_per_mod, final_w=None):
    t, d = x2.shape
    f = w_down.shape[0]
    tm, tf = 512, 512
    nf = f // tf
    tiles_per_mod = tokens_per_mod // tm
    final = final_w is not None
    in_specs = [pl.BlockSpec((tm, d), lambda i, j: (i, 0)),
                pl.BlockSpec((1, N_MOD, d), lambda i, j: (i // tiles_per_mod, 0, 0)),
                pl.BlockSpec((1, d), lambda i, j: (0, 0)),
                pl.BlockSpec((d, tf), lambda i, j: (0, j)),
                pl.BlockSpec((d, tf), lambda i, j: (0, nf + j)),
                pl.BlockSpec((tf, d), lambda i, j: (j, 0))]
    args = [x2, mods, norm_w.reshape(1, d), w_up, w_up, w_down]
    if final:
        in_specs.append(pl.BlockSpec((1, d), lambda i, j: (0, 0)))
        args.append(final_w.reshape(1, d))
    return pl.pallas_call(
        functools.partial(_ffn_kernel, sub=sub, final=final),
        grid=(t // tm, nf),
        in_specs=in_specs,
        out_specs=pl.BlockSpec((tm, d), lambda i, j: (i, 0)),
        out_shape=jax.ShapeDtypeStruct((t, d), F32),
        scratch_shapes=[pltpu.VMEM((tm, d), BF16), pltpu.VMEM((tm, d), F32)],
        compiler_params=_params(("parallel", "arbitrary"), 48),
        name="ffn_final" if final else "ffn",
    )(*args)


def _proj_kernel(x_ref, m_ref, nw_ref, w_ref, ws5_ref, *rest, sub, packed):
    if packed:
        o_ref, u0_ref, u1_ref, h_scr, a_scr = rest
    else:
        o_ref, s5_ref, h_scr = rest

    @pl.when(pl.program_id(1) == 0)
    def _():
        m = m_ref[0]
        h = _modnorm(x_ref[...], nw_ref[...], m[3 * sub + 1:3 * sub + 2], m[3 * sub:3 * sub + 1])
        h_scr[...] = h.astype(BF16)
        s5 = _dot(h_scr[...], ws5_ref[...])
        if packed:
            _s5_pack(s5, a_scr, (u0_ref, u1_ref))
        else:
            s5_ref[...] = s5

    o_ref[...] = _dot(h_scr[...], w_ref[...])


def _s5_pack(a, a_scr, u_refs):
    tm, width = a.shape
    half = GRID_W // 2
    n_rl = tm // GRID_W
    for k in range(width // LANES):
        a_scr[k] = a[:, k * LANES:(k + 1) * LANES]
    for cc in range(2):
        acc = jnp.concatenate([a_scr[k, pl.ds(cc, tm // 2, stride=2), :] for k in range(width // LANES)], axis=1)
        for g in range(width // S5_GROUP):
            piece = jnp.concatenate([acc[rl * half:(rl + 1) * half, g * S5_GROUP:(g + 1) * S5_GROUP]
                                     for rl in range(n_rl)], axis=1)
            u_refs[cc][g] = piece.astype(BF16)


def _proj(x2, mods, norm_w, w, n_main, w_s5, sub, tokens_per_mod, packed):
    t, d = x2.shape
    tm, tn = 1024, 896
    assert n_main % tn == 0
    tiles_per_mod = tokens_per_mod // tm
    s5w = w_s5.shape[1]
    in_specs = [pl.BlockSpec((tm, d), lambda i, j: (i, 0)),
                pl.BlockSpec((1, N_MOD, d), lambda i, j: (i // tiles_per_mod, 0, 0)),
                pl.BlockSpec((1, d), lambda i, j: (0, 0)),
                pl.BlockSpec((d, tn), lambda i, j: (0, j)),
                pl.BlockSpec((d, s5w), lambda i, j: (0, 0))]
    args = [x2, mods, norm_w.reshape(1, d), w, w_s5]
    out_specs = [pl.BlockSpec((tm, tn), lambda i, j: (i, j))]
    out_shape = [jax.ShapeDtypeStruct((t, n_main), F32)]
    scratch = [pltpu.VMEM((tm, d), BF16)]
    if packed:
        groups = s5w // S5_GROUP
        half = GRID_W // 2
        lanes_per_tile = (tm // GRID_W) * S5_GROUP
        rows_total = tokens_per_mod // GRID_W
        n_b = t // tokens_per_mod
        u_spec = pl.BlockSpec((groups, half, lanes_per_tile), lambda i, j: (0, i // tiles_per_mod, i % tiles_per_mod))
        u_shape = jax.ShapeDtypeStruct((groups, n_b * half, rows_total * S5_GROUP), BF16)
        out_specs += [u_spec, u_spec]
        out_shape += [u_shape, u_shape]
        scratch.append(pltpu.VMEM((s5w // LANES, tm, LANES), F32))
    else:
        out_specs.append(pl.BlockSpec((tm, s5w), lambda i, j: (i, 0)))
        out_shape.append(jax.ShapeDtypeStruct((t, s5w), F32))
    return pl.pallas_call(
        functools.partial(_proj_kernel, sub=sub, packed=packed),
        grid=(t // tm, n_main // tn),
        in_specs=in_specs,
        out_specs=out_specs,
        out_shape=out_shape,
        scratch_shapes=scratch,
        compiler_params=_params(("parallel", "arbitrary"), 52),
        name="in_proj_s5" if packed else "in_proj",
    )(*args)


def _conv_kernel(p_ref, w_ref, o_ref, xp_scr, *, rows, width, strip, n_q_tiles, n_qk_tiles):
    length = rows * width
    pad = (xp_scr.shape[0] - length) // 2
    tc = p_ref.shape[2]
    j = pl.program_id(1)
    xp_scr[0:pad, :] = jnp.zeros((pad, tc), F32)
    xp_scr[pad + length:pad + length + pad, :] = jnp.zeros((pad, tc), F32)
    xp_scr[pad:pad + length, :] = p_ref[0]
    w = w_ref[...]
    q_scale = jnp.where(j < n_q_tiles, HEAD_DIM ** -0.5, 1.0).astype(F32)
    is_qk = j < n_qk_tiles
    row_taps = (-1, 0, 1) if rows > 1 else (0,)
    halo = 8
    for s in range(length // strip):
        t0 = s * strip
        col = (lax.broadcasted_iota(jnp.int32, (strip, 1), 0) + t0) & (width - 1)
        base = pad + t0 - halo
        acc = jnp.zeros((strip, tc), F32)
        for dc in (-1, 0, 1):
            z = jnp.zeros((strip + 2 * halo, tc), F32)
            for dr in row_taps:
                tap = (dr + 1) * 3 + (dc + 1)
                z = z + xp_scr[base + dr * width:base + dr * width + strip + 2 * halo, :] * w[tap:tap + 1, :]
            z = z[halo + dc:halo + dc + strip, :]
            if dc == -1:
                z = jnp.where(col >= 1, z, 0.0)
            elif dc == 1:
                z = jnp.where(col <= width - 2, z, 0.0)
            acc = acc + z
        y = _silu(acc)
        for hh in range(tc // HEAD_DIM):
            yh = y[:, hh * HEAD_DIM:(hh + 1) * HEAD_DIM]
            inv = lax.rsqrt(jnp.sum(yh * yh, axis=-1, keepdims=True) + EPS) * q_scale
            o_ref[0, t0:t0 + strip, hh * HEAD_DIM:(hh + 1) * HEAD_DIM] = yh * jnp.where(is_qk, inv, 1.0)


def _conv(p3, conv_w9, rows, width, n_conv, n_qk):
    b, length, _ = p3.shape
    tc = 512
    pad = width + 8 if rows > 1 else 8
    strip = min(256, length)
    kern = functools.partial(_conv_kernel, rows=rows, width=width, strip=strip,
                             n_q_tiles=(n_qk // 2) // tc, n_qk_tiles=n_qk // tc)
    return pl.pallas_call(
        kern,
        grid=(b, n_conv // tc),
        in_specs=[pl.BlockSpec((1, length, tc), lambda i, j: (i, 0, j)),
                  pl.BlockSpec((9, tc), lambda i, j: (0, j))],
        out_specs=pl.BlockSpec((1, length, tc), lambda i, j: (i, 0, j)),
        out_shape=jax.ShapeDtypeStruct((b, length, n_conv), F32),
        scratch_shapes=[pltpu.VMEM((length + 2 * pad, tc), F32)],
        compiler_params=_params(("parallel", "parallel"), 40),
        name="grid_conv",
    )(p3, conv_w9)


def _unit_tri_inverse(a_list, strict_list, row, col):
    n = 1
    e = None
    while n < CHUNK:
        k = n.bit_length() - 1
        pair = ((row >> (k + 1)) == (col >> (k + 1))) & ((row >> k) != (col >> k))
        l_n = [jnp.where(pair & st, a, 0.0) for a, st in zip(a_list, strict_list)]
        if e is None:
            e = [-l for l in l_n]
        else:
            e16 = [x.astype(BF16) for x in e]
            y = [l + _dot(l.astype(BF16), x16) for l, x16 in zip(l_n, e16)]
            e = [x - yy - _dot(x16, yy.astype(BF16)) for x, x16, yy in zip(e, e16, y)]
        n *= 2
    return e


def _delta_kernel(*refs, n_heads, with_init, with_output):
    chunk_refs = (refs[0:4], refs[4:8])
    alog_ref, dtb_ref = refs[8:10]
    rest = refs[10:]
    if with_init:
        s0_ref, rest = rest[0], rest[1:]
    if with_output:
        o_refs, s_scr = rest[0:2], rest[2]
    else:
        sfin_ref, s_scr = rest
    c = pl.program_id(1)

    @pl.when(c == 0)
    def _():
        if with_init:
            s_scr[...] = s0_ref[0]
        else:
            s_scr[...] = jnp.zeros_like(s_scr)

    row = lax.broadcasted_iota(jnp.int32, (CHUNK, CHUNK), 0)
    col = lax.broadcasted_iota(jnp.int32, (CHUNK, CHUNK), 1)
    incl_d = (row >= col, row <= col)
    strict_d = (row > col, row < col)

    chains = [(d, h) for d in range(2) for h in range(n_heads)]
    gc, gt, bt, decay, kh, vh, qh, strict = [], [], [], [], [], [], [], []
    for d in range(2):
        q_ref, k_ref, v_ref, gb_ref = chunk_refs[d]
        gb = gb_ref[0]
        x = gb + dtb_ref[...]
        softplus = jnp.maximum(x, 0.0) + jnp.log1p(jnp.exp(-jnp.abs(x)))
        g_all = -jnp.exp(alog_ref[...]) * softplus
        beta_all = _sigmoid(gb)
        gcum = jnp.dot(incl_d[d].astype(F32), g_all, precision=lax.Precision.HIGHEST,
                       preferred_element_type=F32)
        gtot = jnp.sum(g_all, axis=0, keepdims=True)
        gcum_t = jnp.concatenate([gcum, jnp.zeros_like(gcum)], axis=0).T
        for h in range(n_heads):
            sl = slice(h * HEAD_DIM, (h + 1) * HEAD_DIM)
            lane = d * n_heads + h
            gc.append(gcum[:, lane:lane + 1])
            gt.append(gtot[:, lane:lane + 1])
            bt.append(beta_all[:, 2 * n_heads + lane:2 * n_heads + lane + 1])
            decay.append(jnp.where(incl_d[d], jnp.exp(jnp.where(incl_d[d], gc[-1] - gcum_t[lane:lane + 1, 0:CHUNK], 0.0)), 0.0))
            kh.append(k_ref[0, :, sl])
            vh.append(v_ref[0, :, sl])
            if with_output:
                qh.append(q_ref[0, :, sl])
            strict.append(strict_d[d])
    n = range(len(chains))
    k16 = [x.astype(BF16) for x in kh]
    if with_output:
        kq = [_dot_nt(jnp.concatenate([k16[i], qh[i].astype(BF16)], axis=0), k16[i]) for i in n]
        kk = [x[:CHUNK] for x in kq]
        qk = [x[CHUNK:] for x in kq]
    else:
        kk = [_dot_nt(x, x) for x in k16]
    a_mat = [jnp.where(strict[i], bt[i] * decay[i] * kk[i], 0.0) for i in n]
    e = _unit_tri_inverse(a_mat, strict, row, col)
    eg = [jnp.exp(x) for x in gc]
    rhs = [jnp.concatenate([(bt[i] * eg[i]) * kh[i], bt[i] * vh[i]], axis=1) for i in n]
    sol = [rhs[i] + _dot(e[i].astype(BF16), rhs[i].astype(BF16)) for i in n]
    s_h = [s_scr[d, h] for d, h in chains]
    s16 = [x.astype(BF16) for x in s_h]
    if with_output:
        wq = [_dot(jnp.concatenate([sol[i][:, :HEAD_DIM].astype(BF16), (qh[i] * eg[i]).astype(BF16)], axis=0), s16[i])
              for i in n]
        ws = [x[:CHUNK] for x in wq]
        qs = [x[CHUNK:] for x in wq]
    else:
        ws = [_dot(sol[i][:, :HEAD_DIM].astype(BF16), s16[i]) for i in n]
    u16 = [(sol[i][:, HEAD_DIM:] - ws[i]).astype(BF16) for i in n]
    if with_output:
        for i, (d, h) in enumerate(chains):
            o = qs[i] + _dot((qk[i] * decay[i]).astype(BF16), u16[i])
            o_refs[d][0, :, h * HEAD_DIM:(h + 1) * HEAD_DIM] = o.astype(o_refs[d].dtype)
    for i, (d, h) in enumerate(chains):
        k_dec = kh[i] * jnp.exp(gt[i] - gc[i])
        s_scr[d, h] = jnp.exp(gt[i]) * s_h[i] + _dot_tn(k_dec.astype(BF16), u16[i])

    if not with_output:
        @pl.when(c == pl.num_programs(1) - 1)
        def _():
            sfin_ref[0] = s_scr[...]


def _delta(qkv, p3, gate_block, alog_r, dtb_r, n_heads, s0=None):
    b, length, _ = qkv.shape
    nch = length // CHUNK
    hd = n_heads * HEAD_DIM
    with_init = s0 is not None
    with_output = with_init

    in_specs, args = [], []
    for d in range(2):
        pos = (lambda c: c) if d == 0 else (lambda c: nch - 1 - c)
        in_specs += [pl.BlockSpec((1, CHUNK, hd), lambda i, c, pos=pos: (i, pos(c), 0)),
                     pl.BlockSpec((1, CHUNK, hd), lambda i, c, pos=pos: (i, pos(c), 1)),
                     pl.BlockSpec((1, CHUNK, hd), lambda i, c, pos=pos: (i, pos(c), 2)),
                     pl.BlockSpec((1, CHUNK, LANES), lambda i, c, pos=pos: (i, pos(c), gate_block))]
        args += [qkv, qkv, qkv, p3]
    in_specs += [pl.BlockSpec((1, LANES), lambda i, c: (0, 0))] * 2
    args += [alog_r, dtb_r]
    state_spec = pl.BlockSpec((1, 2, n_heads, HEAD_DIM, HEAD_DIM), lambda i, c: (i, 0, 0, 0, 0))
    if with_init:
        in_specs.append(state_spec)
        args.append(s0)
    if with_output:
        out_specs = [pl.BlockSpec((1, CHUNK, hd), lambda i, c: (i, c, 0)),
                     pl.BlockSpec((1, CHUNK, hd), lambda i, c: (i, nch - 1 - c, 0))]
        out_shape = [jax.ShapeDtypeStruct((b, length, hd), BF16)] * 2
    else:
        out_specs = state_spec
        out_shape = jax.ShapeDtypeStruct((b, 2, n_heads, HEAD_DIM, HEAD_DIM), F32)
    return pl.pallas_call(
        functools.partial(_delta_kernel, n_heads=n_heads, with_init=with_init, with_output=with_output),
        grid=(b, nch),
        in_specs=in_specs,
        out_specs=out_specs,
        out_shape=out_shape,
        scratch_shapes=[pltpu.VMEM((2, n_heads, HEAD_DIM, HEAD_DIM), F32)],
        compiler_params=_params(("parallel", "arbitrary"), 40),
        name="delta_lat" if with_output else "delta_ctx",
    )(*args)


def _cmul(ar, ai, br, bi):
    return ar * br - ai * bi, ar * bi + ai * br


def _s5_tables(a_re, a_im, log_dt, b_re, b_im, c_re, c_im, d_skip):
    g, p, s = b_re.shape
    dt = jnp.exp(log_dt.astype(F32))[..., None]
    lr, li = a_re.astype(F32) * dt, a_im.astype(F32) * dt

    def powers(d, n):
        n = n.astype(F32)
        mag = jnp.exp(lr[d][..., None] * n)
        ang = li[d][..., None] * n
        return mag * jnp.cos(ang), mag * jnp.sin(ang)

    abr = jnp.exp(lr) * jnp.cos(li)
    abi = jnp.exp(lr) * jnp.sin(li)
    nr, ni = abr - 1.0, abi
    cr, ci = a_re.astype(F32), a_im.astype(F32)
    den = cr * cr + ci * ci
    fr, fi = (nr * cr + ni * ci) / den, (ni * cr - nr * ci) / den
    bbr, bbi = _cmul(fr[..., None], fi[..., None], b_re.astype(F32)[None], b_im.astype(F32)[None])
    ccr = jnp.swapaxes(c_re.astype(F32), 1, 2)
    cci = jnp.swapaxes(c_im.astype(F32), 1, 2)
    sub = jnp.arange(S5_SUB)

    def x_table(d, n):
        er, ei = powers(d, n)
        xr, xi = _cmul(bbr[d][:, :, None, :], bbi[d][:, :, None, :], er[..., None], ei[..., None])
        xr = jnp.transpose(xr, (0, 2, 3, 1)).reshape(g, S5_SUB * s, p)
        xi = jnp.transpose(xi, (0, 2, 3, 1)).reshape(g, S5_SUB * s, p)
        return jnp.concatenate([xr, xi], axis=-1)

    def y_table(d, n):
        er, ei = powers(d, n)
        yr, yi = _cmul(ccr[:, :, None, :], cci[:, :, None, :], er[..., None], ei[..., None])
        return jnp.concatenate([yr.reshape(g, p, S5_SUB * s), -yi.reshape(g, p, S5_SUB * s)], axis=1)

    xt = jnp.stack([x_table(0, S5_SUB - 1 - sub), x_table(1, sub)], axis=1)
    yt = jnp.stack([y_table(0, sub + 1), y_table(1, S5_SUB - sub)], axis=1)

    def near_taps(d, descending):
        er, ei = powers(d, sub)
        xr, xi = _cmul(bbr[d][:, :, None, :], bbi[d][:, :, None, :], er[..., None], ei[..., None])
        taps = (jnp.einsum('gpks,gpq->gskq', xr, ccr, precision=lax.Precision.HIGHEST)
                - jnp.einsum('gpks,gpq->gskq', xi, cci, precision=lax.Precision.HIGHEST))
        if descending:
            taps = taps[:, :, ::-1]
        return taps.reshape(g, s, S5_SUB * s)

    tt = jnp.stack([near_taps(0, False), near_taps(1, True),
                    jnp.tile(d_skip.astype(F32).reshape(g, 1, s), (1, s, S5_SUB))], axis=1)
    rows = []
    for d in range(2):
        er, ei = powers(d, S5_SUB * jnp.arange(CHUNK // S5_SUB + 1))
        for k in range(CHUNK // S5_SUB + 1):
            rows += [jnp.concatenate([er[..., k], er[..., k]], -1), jnp.concatenate([-ei[..., k], ei[..., k]], -1)]
    pw = jnp.stack(rows, axis=1)
    return xt, yt, tt, pw


def _s5_diag_block(tt):
    lane = lax.broadcasted_iota(jnp.int32, (S5_GROUP, LANES), 1)
    chan = lax.broadcasted_iota(jnp.int32, (S5_GROUP, LANES), 0)
    blocks = []
    for jj in range(S5_SUB):
        right = jj * S5_GROUP
        left = (S5_SUB - 1 - jj) * S5_GROUP
        fwd = tt[0] if right == 0 else jnp.where(lane >= right, pltpu.roll(tt[0], right, axis=1), 0.0)
        bwd = tt[1] if left == 0 else jnp.where(lane < LANES - left, pltpu.roll(tt[1], LANES - left, axis=1), 0.0)
        blocks.append(fwd + bwd + jnp.where(lane == right + chan, tt[2], 0.0))
    return jnp.concatenate(blocks, axis=0)


def _s5_kernel(uc_ref, u0_ref, u1_ref, xt_ref, yt_ref, tt_ref, pw_ref, y_ref, bst_scr, m_scr, h_scr, y_scr,
               *, nb, grid_rows):
    nsub = CHUNK // S5_SUB
    n_pw = 2 * (nsub + 1)
    n_lat = u0_ref.shape[1] // nb
    n_ctx = uc_ref.shape[1] // nb

    def hdot(a, b):
        return jnp.dot(a, b, precision=lax.Precision.HIGHEST, preferred_element_type=F32)

    def one_group(gi, carry):
        pw = pw_ref[gi]

        def crot(x, d, k, pw=pw):
            r0 = d * n_pw + 2 * k
            return pw[r0:r0 + 1] * x + pw[r0 + 1:r0 + 2] * pltpu.roll(x, LANES // 2, axis=1)

        xf, xb = xt_ref[gi, 0], xt_ref[gi, 1]
        xfs = [xf] + [crot(xf, 0, k) for k in range(1, nsub)]
        xbs = [xb] + [crot(xb, 1, k) for k in range(1, nsub)]
        yf1, yb1 = yt_ref[gi, 0], yt_ref[gi, 1]

        for j in range(nsub):
            bst_scr[j * LANES:(j + 1) * LANES, 0:LANES] = xfs[nsub - 1 - j].astype(BF16)
            bst_scr[j * LANES:(j + 1) * LANES, LANES:2 * LANES] = xbs[j].astype(BF16)
        bst = bst_scr[...]
        ul = jnp.concatenate([u0_ref[gi], u1_ref[gi]], axis=1)
        hc = _dot(uc_ref[gi], bst)
        hl = _dot(ul, bst)
        h_scr[0] = hl[:, 0:LANES]
        h_scr[1] = hl[:, LANES:2 * LANES]

        hf = jnp.zeros((nb, LANES), F32)
        hb = jnp.zeros((nb, LANES), F32)
        for c in range(n_ctx):
            hf = crot(hf, 0, nsub) + hc[c * nb:(c + 1) * nb, 0:LANES]
        for c in reversed(range(n_ctx)):
            hb = crot(hb, 1, nsub) + hc[c * nb:(c + 1) * nb, LANES:2 * LANES]
        for c in range(n_lat):
            rows_c = pl.ds(c, nb, stride=n_lat)
            inp = h_scr[0, rows_c, :]
            h_scr[0, rows_c, :] = hf
            hf = crot(hf, 0, nsub) + inp
        for c in reversed(range(n_lat)):
            rows_c = pl.ds(c, nb, stride=n_lat)
            inp = h_scr[1, rows_c, :]
            h_scr[1, rows_c, :] = hb
            hb = crot(hb, 1, nsub) + inp

        diag = _s5_diag_block(tt_ref[gi]).astype(BF16)
        taps_f = [None] + [hdot(xfs[dl - 1], yf1).astype(BF16) for dl in range(1, nsub)]
        taps_b = [None] + [hdot(xbs[dl - 1], yb1).astype(BF16) for dl in range(1, nsub)]
        for bj in range(nsub):
            for bi in range(nsub):
                blk = diag if bi == bj else (taps_f[bi - bj] if bi > bj else taps_b[bj - bi])
                m_scr[bj * LANES:(bj + 1) * LANES, bi * LANES:(bi + 1) * LANES] = blk
        y = _dot(ul, m_scr[...])
        hin_f, hin_b = h_scr[0], h_scr[1]
        yf16, yb16 = yf1.astype(BF16), yb1.astype(BF16)
        for bi in range(nsub):
            y_scr[gi, :, bi * LANES:(bi + 1) * LANES] = (
                y[:, bi * LANES:(bi + 1) * LANES]
                + _dot(crot(hin_f, 0, bi).astype(BF16), yf16)
                + _dot(crot(hin_b, 1, nsub - 1 - bi).astype(BF16), yb16))
        return carry

    lax.fori_loop(0, u0_ref.shape[0], one_group, 0)

    half = GRID_W // 2
    tokens = grid_rows * GRID_W
    for cc in range(2):
        for r in range(grid_rows):
            lane0 = (cc * grid_rows + r) * S5_GROUP
            piece = jnp.concatenate([y_scr[gi, :, lane0:lane0 + S5_GROUP] for gi in range(u0_ref.shape[0])], axis=1)
            for b in range(nb):
                y_ref[pl.ds(b * tokens + r * GRID_W + cc, half, stride=2), :] = piece[b * half:(b + 1) * half]


def _s5(uc, u0, u1, tables, nb, grid_rows):
    xt, yt, tt, pw = tables
    g = u0.shape[0]
    rows_l = u0.shape[1]
    kdim = 2 * u0.shape[2]
    tokens = grid_rows * GRID_W
    gstep = LANES // S5_GROUP

    def spec(a):
        nd = a.ndim
        return pl.BlockSpec((gstep,) + tuple(a.shape[1:]), lambda i: (i,) + (0,) * (nd - 1))

    arrays = [uc, u0, u1, xt, yt, tt, pw]
    return pl.pallas_call(
        functools.partial(_s5_kernel, nb=nb, grid_rows=grid_rows),
        grid=(g // gstep,),
        in_specs=[spec(a) for a in arrays],
        out_specs=pl.BlockSpec((nb * tokens, LANES), lambda i: (0, i)),
        out_shape=jax.ShapeDtypeStruct((nb * tokens, g * S5_GROUP), F32),
        scratch_shapes=[pltpu.VMEM((kdim, 2 * LANES), BF16), pltpu.VMEM((kdim, kdim), BF16),
                        pltpu.VMEM((2, rows_l, LANES), F32), pltpu.VMEM((gstep, rows_l, kdim), F32)],
        compiler_params=_params(("parallel",), 48),
        name="s5_scan",
    )(*arrays)


def _gelu_tanh(x):
    return 0.5 * x * (1.0 + jnp.tanh(0.7978845608028654 * (x + 0.044715 * (x * x * x))))


def _out_kernel(of_ref, ob_ref, z_ref, y_ref, x_ref, m_ref, dnw_ref, wglu_ref, wdn_ref, ws5_ref, out_ref, *, n_heads):
    o = of_ref[...].astype(F32) + ob_ref[...].astype(F32)
    z = z_ref[...]
    dnw = dnw_ref[...]
    parts = []
    for h in range(n_heads):
        sl = slice(h * HEAD_DIM, (h + 1) * HEAD_DIM)
        oh = o[:, sl]
        yh = oh * lax.rsqrt(jnp.mean(oh * oh, axis=-1, keepdims=True) + EPS) * dnw
        parts.append((yh * _silu(z[:, sl])).astype(BF16))
    dn = jnp.concatenate(parts, axis=1)
    t = _dot(_gelu_tanh(y_ref[...]).astype(BF16), wglu_ref[...])
    half = t.shape[1] // 2
    s5 = (t[:, :half] * _sigmoid(t[:, half:])).astype(BF16)
    acc = _dot(dn, wdn_ref[...]) + _dot(s5, ws5_ref[...])
    out_ref[...] = x_ref[...] + m_ref[0][5:6] * acc


def _out_proj(o_fwd, o_bwd, p2, z_block, y2, x2, mods, dn_norm, w_glu, w_out, n_heads, tokens_per_mod):
    t, d = x2.shape
    hd = n_heads * HEAD_DIM
    s5w = y2.shape[1]
    tm = 512
    tiles_per_mod = tokens_per_mod // tm
    once = pl.Buffered(1)
    return pl.pallas_call(
        functools.partial(_out_kernel, n_heads=n_heads),
        grid=(t // tm,),
        in_specs=[pl.BlockSpec((tm, hd), lambda i: (i, 0)),
                  pl.BlockSpec((tm, hd), lambda i: (i, 0)),
                  pl.BlockSpec((tm, hd), lambda i: (i, z_block)),
                  pl.BlockSpec((tm, s5w), lambda i: (i, 0)),
                  pl.BlockSpec((tm, d), lambda i: (i, 0)),
                  pl.BlockSpec((1, N_MOD, d), lambda i: (i // tiles_per_mod, 0, 0)),
                  pl.BlockSpec((1, HEAD_DIM), lambda i: (0, 0)),
                  pl.BlockSpec(w_glu.shape, lambda i: (0, 0), pipeline_mode=once),
                  pl.BlockSpec((hd, d), lambda i: (0, 0), pipeline_mode=once),
                  pl.BlockSpec((s5w, d), lambda i: (hd // s5w, 0), pipeline_mode=once)],
        out_specs=pl.BlockSpec((tm, d), lambda i: (i, 0)),
        out_shape=jax.ShapeDtypeStruct((t, d), F32),
        compiler_params=_params(("parallel",), 56),
        name="out_proj",
    )(o_fwd, o_bwd, p2, y2, x2, mods, dn_norm.reshape(1, HEAD_DIM), w_glu, w_out, w_out)


def _layer(x, ctx, m_lat, m_ctx, norm_ffn1, ffn1_up, ffn1_down, norm_mix, w_in, dn_conv, dn_a_log,
           dn_dt_bias, dn_norm, s5_a_re, s5_a_im, s5_log_dt, s5_b_re, s5_b_im, s5_c_re, s5_c_im, s5_d,
           s5_glu, w_out, norm_ffn2, ffn2_up, ffn2_down, final_norm):
    b, length, d = x.shape
    lc = ctx.shape[1]
    n_heads = dn_a_log.shape[1]
    hd = n_heads * HEAD_DIM
    n_conv = dn_conv.shape[-1]
    n_qk = n_conv - hd
    s5w = s5_d.shape[0]
    groups = s5w // S5_GROUP
    rows = length // GRID_W

    gate0 = n_conv + hd
    w_in16 = w_in.astype(BF16)
    w_s5 = w_in16[:, gate0 + 4 * n_heads:]
    n_main = gate0 + LANES
    z_block = n_conv // hd
    gate_block = gate0 // LANES
    lane_pad = jnp.zeros((LANES - 2 * n_heads,), F32)
    alog_r = jnp.concatenate([dn_a_log.astype(F32).reshape(-1), lane_pad]).reshape(1, LANES)
    dtb_r = jnp.concatenate([dn_dt_bias.astype(F32).reshape(-1), lane_pad]).reshape(1, LANES)
    conv_w9 = dn_conv.reshape(9, n_conv)

    x2 = x.reshape(b * length, d)
    c2 = ctx.reshape(b * lc, d)

    up1, down1 = ffn1_up.astype(BF16), ffn1_down.astype(BF16)
    x2 = _ffn(x2, m_lat, norm_ffn1, up1, down1, 0, length)
    c2 = _ffn(c2, m_ctx, norm_ffn1, up1, down1, 0, b * lc)

    p_lat, u_lat0, u_lat1 = _proj(x2, m_lat, norm_mix, w_in16, n_main, w_s5, 1, length, packed=True)
    p_ctx, u_ctx = _proj(c2, m_ctx, norm_mix, w_in16, n_main, w_s5, 1, b * lc, packed=False)
    p_lat3 = p_lat.reshape(b, length, n_main)
    p_ctx3 = p_ctx.reshape(b, lc, n_main)

    qkv_lat = _conv(p_lat3, conv_w9, rows, GRID_W, n_conv, n_qk)
    qkv_ctx = _conv(p_ctx3, conv_w9, 1, lc, n_conv, n_qk)
    s_ctx = _delta(qkv_ctx, p_ctx3, gate_block, alog_r, dtb_r, n_heads)
    o_fwd, o_bwd = _delta(qkv_lat, p_lat3, gate_block, alog_r, dtb_r, n_heads, s0=s_ctx)

    tables = _s5_tables(s5_a_re, s5_a_im, s5_log_dt, s5_b_re, s5_b_im, s5_c_re, s5_c_im, s5_d)
    u_ctx = u_ctx.astype(BF16).reshape(b, lc // CHUNK, CHUNK, groups, S5_GROUP)
    u_ctx = jnp.transpose(u_ctx, (3, 1, 0, 2, 4)).reshape(groups, (lc // CHUNK) * b, CHUNK * S5_GROUP)
    y2 = _s5(u_ctx, u_lat0, u_lat1, tables, b, rows)

    x2 = _out_proj(o_fwd.reshape(b * length, hd), o_bwd.reshape(b * length, hd), p_lat, z_block, y2, x2, m_lat, dn_norm,
                   s5_glu.astype(BF16), w_out.astype(BF16), n_heads, length)

    x2 = _ffn(x2, m_lat, norm_ffn2, ffn2_up.astype(BF16), ffn2_down.astype(BF16), 2, length,
              final_w=final_norm)
    return x2.reshape(b, length, d)


def kernel(x, c, ctx, c_ctx, w_mod, b_mod, norm_ffn1, ffn1_up, ffn1_down, norm_mix, w_in, dn_conv, dn_a_log, dn_dt_bias, dn_norm, s5_a_re, s5_a_im, s5_log_dt, s5_b_re, s5_b_im, s5_c_re, s5_c_im, s5_d, s5_glu, w_out, norm_ffn2, ffn2_up, ffn2_down, final_norm):
    depth = w_mod.shape[0]
    assert depth == 1, "the context stream update of deeper stacks is not implemented"
    b, _, d = x.shape
    cond = jnp.concatenate([c, c_ctx[None], jnp.zeros((16 - b - 1, d), c.dtype)], axis=0)
    m = _ada(cond, w_mod[0], b_mod[0]).reshape(16, N_MOD, d)
    return _layer(x, ctx, m[:b], m[b:b + 1], norm_ffn1[0], ffn1_up[0], ffn1_down[0], norm_mix[0], w_in[0],
                  dn_conv[0], dn_a_log[0], dn_dt_bias[0], dn_norm[0], s5_a_re[0], s5_a_im[0], s5_log_dt[0],
                  s5_b_re[0], s5_b_im[0], s5_c_re[0], s5_c_im[0], s5_d[0], s5_glu[0], w_out[0],
                  norm_ffn2[0], ffn2_up[0], ffn2_down[0], final_norm)
```
